```python
import math
import jax, jax.numpy as jnp
from jax import lax
import numpy as np

D_MODEL = 1024
BATCH = 2
SEQ = 8192
DEPTH = 2

GRID_W = 64
CTX_LEN = 256
EPS = 1e-6
NEG_INF = -1e30

NA_HEADS = 8
NA_HEAD_DIM = 64
NA_WIN_H = 8
NA_WIN_W = 16
NA_QBLOCK = NA_WIN_W
NA_KBLOCK = 2 * NA_WIN_W
NA_WIDTH = NA_HEADS * NA_HEAD_DIM
SG_GROUPS = 8
SG_GROUP_DIM = 64
SG_CHUNK = 128
SG_WIDTH = SG_GROUPS * SG_GROUP_DIM
AB_IN = 3 * NA_WIDTH + 2 * SG_WIDTH
AB_OUT = NA_WIDTH + SG_WIDTH

RET_HEADS = 4
RET_QK_DIM = 256
RET_V_DIM = 512
RET_CHUNK = 128
RET_QK_WIDTH = RET_HEADS * RET_QK_DIM
RET_V_WIDTH = RET_HEADS * RET_V_DIM
RET_IN = 2 * RET_QK_WIDTH + 2 * RET_V_WIDTH
ROPE_BASE = 10000.0

N_EXPERTS = 16
EC_CAPACITY = 2
D_FF_EXPERT = 2816

N_EVEN = (DEPTH + 1) // 2
N_ODD = DEPTH // 2

kernel_name = "hybrid_natten_sgu_retention_ecmoe_dit"

F32 = jnp.float32


def rmsnorm(x, g):
    xf = x.astype(F32)
    y = xf * lax.rsqrt(jnp.mean(xf * xf, axis=-1, keepdims=True) + EPS)
    return (y * g.astype(F32)).astype(x.dtype)


def adaln(cond, w, b):
    m = jax.nn.silu(cond) @ w + b
    return jnp.split(m[..., None, :], 6, axis=-1)


def modulate(h, shift, scale):
    return h * (1 + scale) + shift


def to_heads(t, n_heads):
    b, l, _ = t.shape
    return t.reshape(b, l, n_heads, -1).transpose(0, 2, 1, 3)


def from_heads(t):
    b, h, l, d = t.shape
    return t.transpose(0, 2, 1, 3).reshape(b, l, h * d)


def axial_rope_tables(rows, dim):
    axis_dim = dim // 2
    inv = 1.0 / (ROPE_BASE ** (jnp.arange(0, axis_dim, 2, dtype=F32) / axis_dim))
    t = jnp.arange(rows * GRID_W)
    r = (t // GRID_W).astype(F32)
    col = (t % GRID_W).astype(F32)
    ang = jnp.concatenate([r[:, None] * inv, col[:, None] * inv], axis=-1)
    return jnp.cos(ang), jnp.sin(ang)


def apply_rope(x, cos, sin):
    x1, x2 = jnp.split(x, 2, axis=-1)
    return jnp.concatenate([x1 * cos - x2 * sin, x1 * sin + x2 * cos], axis=-1).astype(x.dtype)


def dense_attention(q, k, v):
    s = jnp.einsum('bhqd,bhkd->bhqk', q, k).astype(F32) * (q.shape[-1] ** -0.5)
    p = jax.nn.softmax(s, axis=-1).astype(v.dtype)
    return jnp.einsum('bhqk,bhkd->bhqd', p, v)


def neighbourhood_attention(q, k, v, k_ctx, v_ctx, rpb):
    B, H, L, d = q.shape
    rows = L // GRID_W
    kh = min(NA_WIN_H, rows)
    ncb = GRID_W // NA_QBLOCK
    scale = d ** -0.5
    qcol = np.arange(GRID_W).reshape(ncb, NA_QBLOCK)
    cstart = np.clip(qcol - NA_WIN_W // 2, 0, GRID_W - NA_WIN_W)
    kc0 = np.minimum(cstart[:, 0], GRID_W - NA_KBLOCK)
    kcol = kc0[:, None] + np.arange(NA_KBLOCK)
    col_ok = (kcol[:, None, :] >= cstart[:, :, None]) & (kcol[:, None, :] < cstart[:, :, None] + NA_WIN_W)
    dcol = np.clip(kcol[:, None, :] - qcol[:, :, None], 1 - NA_WIN_W, NA_WIN_W - 1) + NA_WIN_W - 1
    mask = jnp.asarray(col_ok)[:, :, None, :]
    qg = q.reshape(B, H, rows, ncb, NA_QBLOCK, d)
    kg = k.reshape(B, H, rows, GRID_W, d)
    vg = v.reshape(B, H, rows, GRID_W, d)
    nk = kh * NA_KBLOCK

    def row_block(i):
        r0 = jnp.clip(i - kh // 2, 0, rows - kh)
        k_blk = lax.dynamic_slice_in_dim(kg, r0, kh, axis=2)[:, :, :, kcol]
        v_blk = lax.dynamic_slice_in_dim(vg, r0, kh, axis=2)[:, :, :, kcol]
        q_row = lax.dynamic_index_in_dim(qg, i, axis=2, keepdims=False)
        s_nb = jnp.einsum('bhcqd,bhrckd->bhcqrk', q_row, k_blk).astype(F32) * scale
        drow = r0 + jnp.arange(kh) - i + NA_WIN_H - 1
        bias = rpb[:, drow][:, :, dcol].astype(F32)
        s_nb = jnp.where(mask, s_nb + jnp.transpose(bias, (0, 2, 3, 1, 4)), NEG_INF)
        s_c = jnp.einsum('bhcqd,bhkd->bhcqk', q_row, k_ctx).astype(F32) * scale
        s = jnp.concatenate([s_nb.reshape(B, H, ncb, NA_QBLOCK, nk), s_c], axis=-1)
        p = jax.nn.softmax(s, axis=-1).astype(v.dtype)
        p_nb = p[..., :nk].reshape(B, H, ncb, NA_QBLOCK, kh, NA_KBLOCK)
        p_c = p[..., nk:]
        return (jnp.einsum('bhcqrk,bhrckd->bhcqd', p_nb, v_blk)
                + jnp.einsum('bhcqk,bhkd->bhcqd', p_c, v_ctx))

    out = lax.map(row_block, jnp.arange(rows))
    return jnp.moveaxis(out, 0, 2).reshape(B, H, L, d)


def spatial_gating(u, v, norm_g, w_s, b_s):
    b, l, _ = v.shape
    vn = rmsnorm(v, norm_g).reshape(b, l // SG_CHUNK, SG_CHUNK, SG_GROUPS, SG_GROUP_DIM)
    mixed = jnp.einsum('gpq,bnqgc->bnpgc', w_s, vn) + b_s.T[:, :, None]
    return u * mixed.reshape(b, l, SG_WIDTH)


def na_sgu_mixer(h, h_ctx, w_in, w_out, rpb, sgu_g, sgu_w, sgu_b, with_ctx_out):
    o1, o3, o4 = NA_WIDTH, 3 * NA_WIDTH, 3 * NA_WIDTH + SG_WIDTH
    z = h @ w_in
    q = to_heads(z[..., :o1], NA_HEADS)
    k = to_heads(z[..., o1:2 * o1], NA_HEADS)
    v = to_heads(z[..., 2 * o1:o3], NA_HEADS)
    u = jax.nn.gelu(z[..., o3:o4])
    gv = jax.nn.gelu(z[..., o4:])
    zc_kv = h_ctx @ w_in[:, o1:o3]
    k_c = to_heads(zc_kv[..., :NA_WIDTH], NA_HEADS)
    v_c = to_heads(zc_kv[..., NA_WIDTH:], NA_HEADS)
    a = from_heads(neighbourhood_attention(q, k, v, k_c, v_c, rpb))
    bsg = spatial_gating(u, gv, sgu_g, sgu_w, sgu_b)
    y = jnp.concatenate([a, bsg], axis=-1) @ w_out
    if not with_ctx_out:
        return y, None
    q_c = to_heads(h_ctx @ w_in[:, :o1], NA_HEADS)
    zc_ug = jax.nn.gelu(h_ctx @ w_in[:, o3:])
    a_c = from_heads(dense_attention(q_c, k_c, v_c))
    b_c = spatial_gating(zc_ug[..., :SG_WIDTH], zc_ug[..., SG_WIDTH:], sgu_g, sgu_w, sgu_b)
    y_c = jnp.concatenate([a_c, b_c], axis=-1) @ w_out
    return y, y_c


def retention_scan(q, k, v, log_gamma, s0):
    B, H, L, dk = q.shape
    dv = v.shape[-1]
    C = RET_CHUNK
    n = L // C
    chunks = lambda t: jnp.moveaxis(t.astype(F32).reshape(B, H, n, C, t.shape[-1]), 2, 0)
    idx = jnp.arange(C, dtype=F32)
    diff = idx[:, None] - idx[None, :]
    decay_in = jnp.where(diff >= 0, jnp.exp(jnp.maximum(diff, 0.0) * log_gamma[:, None, None]), 0.0)
    q_decay = jnp.exp((idx + 1) * log_gamma[:, None])[:, :, None]
    k_decay = jnp.exp((C - 1 - idx) * log_gamma[:, None])[:, :, None]
    chunk_decay = jnp.exp(C * log_gamma)[:, None, None]

    def step(s, inp):
        qj, kj, vj = inp
        att = jnp.einsum('bhqd,bhkd->bhqk', qj, kj) * decay_in
        o = jnp.einsum('bhqk,bhkv->bhqv', att, vj) + jnp.einsum('bhqd,bhdv->bhqv', qj, s) * q_decay
        s_new = s * chunk_decay + jnp.einsum('bhkd,bhkv->bhdv', kj * k_decay, vj)
        return s_new, o

    s_fin, o = lax.scan(step, s0, (chunks(q), chunks(k), chunks(v)))
    return jnp.moveaxis(o, 0, 2).reshape(B, H, L, dv), s_fin


def retention_final_state(k, v, log_gamma):
    L = k.shape[2]
    w = jnp.exp((L - 1 - jnp.arange(L, dtype=F32))[None, :] * log_gamma[:, None])
    return jnp.einsum('bhld,bhlv,hl->bhdv', k.astype(F32), v.astype(F32), w)


def retention_output(o, g, w_out):
    of = o.astype(F32)
    of = of * lax.rsqrt(jnp.mean(of * of, axis=-1, keepdims=True) + EPS)
    y = from_heads(of.astype(g.dtype))
    return (jax.nn.silu(g) * y) @ w_out


def retention_mixer(h, h_ctx, w_in, w_out, decay_logit, cos, sin, with_ctx_out):
    o1, o2, o3 = RET_QK_WIDTH, 2 * RET_QK_WIDTH, 2 * RET_QK_WIDTH + RET_V_WIDTH
    kscale = RET_QK_DIM ** -0.5
    z = h @ w_in
    q = apply_rope(to_heads(z[..., :o1], RET_HEADS), cos, sin)
    k = apply_rope(to_heads(z[..., o1:o2], RET_HEADS), cos, sin) * kscale
    v = to_heads(z[..., o2:o3], RET_HEADS)
    g = z[..., o3:]
    zc_kv = h_ctx @ w_in[:, o1:o3]
    k_c = to_heads(zc_kv[..., :RET_QK_WIDTH], RET_HEADS) * kscale
    v_c = to_heads(zc_kv[..., RET_QK_WIDTH:], RET_HEADS)
    if with_ctx_out:
        q_c = to_heads(h_ctx @ w_in[:, :o1], RET_HEADS)
        g_c = h_ctx @ w_in[:, o3:]
    log_gamma = jax.nn.log_sigmoid(decay_logit.astype(F32))
    bc = h_ctx.shape[0]
    lat_outs = []
    ctx_outs = []
    for direction in range(2):
        f = (lambda t: jnp.flip(t, axis=2)) if direction == 1 else (lambda t: t)
        lg = log_gamma[direction]
        if with_ctx_out:
            s0 = jnp.zeros((bc, RET_HEADS, RET_QK_DIM, RET_V_DIM), F32)
            oc, s_c = retention_scan(f(q_c), f(k_c), f(v_c), lg, s0)
            ctx_outs.append(f(oc))
        else:
            s_c = retention_final_state(f(k_c), f(v_c), lg)
        ol, _ = retention_scan(f(q), f(k), f(v), lg, s_c)
        lat_outs.append(f(ol))
    y = retention_output(lat_outs[0] + lat_outs[1], g, w_out)
    if not with_ctx_out:
        return y, None
    y_c = retention_output(ctx_outs[0] + ctx_outs[1], g_c, w_out)
    return y, y_c


def expert_choice_ffn(h, w_router, w_gate, w_up, w_down):
    B, N, _ = h.shape
    cap = EC_CAPACITY * N // N_EXPERTS
    aff = jax.nn.softmax(jnp.einsum('bnd,de->bne', h, w_router).astype(F32), axis=-1)
    gate, idx = lax.top_k(jnp.swapaxes(aff, 1, 2), cap)
    bidx = jnp.arange(B)[:, None, None]
    xe = h[bidx, idx]
    hid = jax.nn.silu(jnp.einsum('becd,edf->becf', xe, w_gate)) * jnp.einsum('becd,edf->becf', xe, w_up)
    ye = jnp.einsum('becf,efd->becd', hid, w_down) * gate[..., None].astype(h.dtype)
    return jnp.zeros_like(h).at[bidx, idx].add(ye)


def setup_inputs(seed: int = 0) -> dict:
    key = jax.random.key(seed)
    ks = jax.random.split(key, 24)
    nrm = lambda k, shape, fan_in: jax.random.normal(k, shape, F32) * (fan_in ** -0.5)
    noise = lambda k, shape, s: jax.random.normal(k, shape, F32) * s
    base_logit = jnp.log(2.0 ** (5.0 + jnp.arange(RET_HEADS, dtype=F32)) - 1.0)
    return {
        "x": jax.random.normal(ks[0], (BATCH, SEQ, D_MODEL), F32),
        "c": jax.random.normal(ks[1], (BATCH, D_MODEL), F32),
        "ctx": jax.random.normal(ks[2], (BATCH, CTX_LEN, D_MODEL), F32),
        "c_ctx": jax.random.normal(ks[3], (D_MODEL,), F32),
        "ada_w": nrm(ks[4], (DEPTH, D_MODEL, 6 * D_MODEL), D_MODEL),
        "ada_b": noise(ks[5], (DEPTH, 6 * D_MODEL), 0.01),
        "norm1_g": 1.0 + noise(ks[6], (DEPTH, D_MODEL), 0.02),
        "norm2_g": 1.0 + noise(ks[7], (DEPTH, D_MODEL), 0.02),
        "ab_w_in": nrm(ks[8], (N_EVEN, D_MODEL, AB_IN), D_MODEL),
        "ab_w_out": nrm(ks[9], (N_EVEN, AB_OUT, D_MODEL), AB_OUT),
        "na_rpb": noise(ks[10], (N_EVEN, NA_HEADS, 2 * NA_WIN_H - 1, 2 * NA_WIN_W - 1), 0.02),
        "sgu_norm_g": 1.0 + noise(ks[11], (N_EVEN, SG_WIDTH), 0.02),
        "sgu_w": nrm(ks[12], (N_EVEN, SG_GROUPS, SG_CHUNK, SG_CHUNK), SG_CHUNK),
        "sgu_b": 1.0 + noise(ks[13], (N_EVEN, SG_GROUPS, SG_CHUNK), 0.01),
        "ret_w_in": nrm(ks[14], (N_ODD, D_MODEL, RET_IN), D_MODEL),
        "ret_w_out": nrm(ks[15], (N_ODD, RET_V_WIDTH, D_MODEL), RET_V_WIDTH),
        "ret_decay_logit": base_logit + noise(ks[16], (N_ODD, 2, RET_HEADS), 0.05),
        "moe_router": nrm(ks[17], (DEPTH, D_MODEL, N_EXPERTS), D_MODEL),
        "moe_w_gate": nrm(ks[18], (DEPTH, N_EXPERTS, D_MODEL, D_FF_EXPERT), D_MODEL),
        "moe_w_up": nrm(ks[19], (DEPTH, N_EXPERTS, D_MODEL, D_FF_EXPERT), D_MODEL),
        "moe_w_down": nrm(ks[20], (DEPTH, N_EXPERTS, D_FF_EXPERT, D_MODEL), D_FF_EXPERT),
        "final_norm_g": 1.0 + noise(ks[21], (D_MODEL,), 0.02),
    }


def reference(x, c, ctx, c_ctx, ada_w, ada_b, norm1_g, norm2_g, ab_w_in, ab_w_out, na_rpb,
              sgu_norm_g, sgu_w, sgu_b, ret_w_in, ret_w_out, ret_decay_logit,
              moe_router, moe_w_gate, moe_w_up, moe_w_down, final_norm_g):
    L = x.shape[1]
    rows = L // GRID_W
    cos, sin = axial_rope_tables(rows, RET_QK_DIM)
    xc = ctx
    for layer in range(DEPTH):
        last = layer == DEPTH - 1
        sh1, sc1, g1, sh2, sc2, g2 = adaln(c, ada_w[layer], ada_b[layer])
        csh1, csc1, cg1, csh2, csc2, cg2 = adaln(c_ctx, ada_w[layer], ada_b[layer])
        h = modulate(rmsnorm(x, norm1_g[layer]), sh1, sc1)
        hc = modulate(rmsnorm(xc, norm1_g[layer]), csh1, csc1)
        if layer % 2 == 0:
            j = layer // 2
            y, y_c = na_sgu_mixer(h, hc, ab_w_in[j], ab_w_out[j], na_rpb[j],
                                  sgu_norm_g[j], sgu_w[j], sgu_b[j], not last)
        else:
            j = layer // 2
            y, y_c = retention_mixer(h, hc, ret_w_in[j], ret_w_out[j], ret_decay_logit[j],
                                     cos, sin, not last)
        x = x + g1 * y
        h2 = modulate(rmsnorm(x, norm2_g[layer]), sh2, sc2)
        x = x + g2 * expert_choice_ffn(h2, moe_router[layer], moe_w_gate[layer],
                                       moe_w_up[layer], moe_w_down[layer])
        if not last:
            xc = xc + cg1 * y_c
            hc2 = modulate(rmsnorm(xc, norm2_g[layer]), csh2, csc2)
            xc = xc + cg2 * expert_choice_ffn(hc2, moe_router[layer], moe_w_gate[layer],
                                              moe_w_up[layer], moe_w_down[layer])
    return rmsnorm(x, final_norm_g)
```

```python
import functools

import numpy as np
import jax
import jax.numpy as jnp
from jax import lax
from jax.experimental import pallas as pl
from jax.experimental.pallas import tpu as pltpu

F32 = jnp.float32
BF16 = jnp.bfloat16

D_MODEL = 1024
BATCH = 2
SEQ = 8192
DEPTH = 2
GRID_W = 64
CTX_LEN = 256
EPS = 1e-6
NEG_INF = -1e30

NA_HEADS = 8
NA_HEAD_DIM = 64
NA_WIN_H = 8
NA_WIN_W = 16
NA_WIDTH = NA_HEADS * NA_HEAD_DIM
SG_GROUPS = 8
SG_GROUP_DIM = 64
SG_CHUNK = 128
SG_WIDTH = SG_GROUPS * SG_GROUP_DIM
AB_IN = 3 * NA_WIDTH + 2 * SG_WIDTH

RET_HEADS = 4
RET_QK_DIM = 256
RET_V_DIM = 512
RET_QK_WIDTH = RET_HEADS * RET_QK_DIM
RET_V_WIDTH = RET_HEADS * RET_V_DIM
RET_IN = 2 * RET_QK_WIDTH + 2 * RET_V_WIDTH
ROPE_BASE = 10000.0

N_EXPERTS = 16
EC_CAPACITY = 2
D_FF_EXPERT = 2816

ROWS = SEQ // GRID_W
R_LAT = BATCH * SEQ
R_CTX = BATCH * CTX_LEN
R_ALL = R_LAT + R_CTX
N_GROUPS = BATCH + 1

LANES = 128
TM = 512
VMEM_LIMIT = 56 * 1024 * 1024

NA_QROWS = 8
NA_KROWS = 16
NA_QT = NA_QROWS * GRID_W
NA_KT = 256
NA_NKB = NA_KROWS * GRID_W // NA_KT
RET_CHUNK = 256
MOE_TF = 256
MOE_ROW_SPLIT = 4


def _group_of_tile(i):
    return jnp.minimum(i // (SEQ // TM), BATCH)


def _cparams(sem, vmem=VMEM_LIMIT):
    return pltpu.CompilerParams(dimension_semantics=sem, vmem_limit_bytes=vmem)


def _dot(a, b):
    return jnp.dot(a, b, preferred_element_type=F32)


def _dot_nt(a, b):
    return lax.dot_general(a, b, (((1,), (1,)), ((), ())), preferred_element_type=F32)


def _dot_tn(a, b):
    return lax.dot_general(a, b, (((0,), (0,)), ((), ())), preferred_element_type=F32)


def _sigmoid(x):
    return 1.0 / (1.0 + jnp.exp(-x))


def _gelu_tanh(x):
    c = np.float32(np.sqrt(2.0 / np.pi))
    return 0.5 * x * (1.0 + jnp.tanh(c * (x + np.float32(0.044715) * (x * x * x))))


def _adaln_kernel(c_ref, w_ref, b_ref, o_ref):
    cv = c_ref[...]
    s = (cv * _sigmoid(cv)).astype(BF16)
    o_ref[...] = _dot(s, w_ref[...].astype(BF16)) + b_ref[...]


def _adaln(cvec, ada_w, ada_b):
    tn = 1024
    n = 6 * D_MODEL
    return pl.pallas_call(
        _adaln_kernel,
        grid=(DEPTH, n // tn),
        in_specs=[
            pl.BlockSpec((8, D_MODEL), lambda l, j: (0, 0)),
            pl.BlockSpec((None, D_MODEL, tn), lambda l, j: (l, 0, j)),
            pl.BlockSpec((None, 1, tn), lambda l, j: (l, 0, j)),
        ],
        out_specs=pl.BlockSpec((None, 8, tn), lambda l, j: (l, 0, j)),
        out_shape=jax.ShapeDtypeStruct((DEPTH, 8, n), F32),
        compiler_params=_cparams(("arbitrary", "arbitrary")),
        name="adaln",
    )(cvec, ada_w, ada_b.reshape(DEPTH, 1, n))


def _inproj_kernel(x_ref, g_ref, sh_ref, sc_ref, w_ref, z_ref, hb_ref):
    @pl.when(pl.program_id(1) == 0)
    def _():
        x = x_ref[...]
        y = x * lax.rsqrt(jnp.mean(x * x, axis=-1, keepdims=True) + EPS) * g_ref[...]
        hb_ref[...] = (y * (1.0 + sc_ref[...]) + sh_ref[...]).astype(BF16)

    z_ref[...] = _dot(hb_ref[...], w_ref[...]).astype(z_ref.dtype)


def _inproj(xall, norm_g, shift, scale, w_bf16, tn):
    n = w_bf16.shape[1]
    grp = lambda i, j: (_group_of_tile(i), 0, 0)
    return pl.pallas_call(
        _inproj_kernel,
        grid=(R_ALL // TM, n // tn),
        in_specs=[
            pl.BlockSpec((TM, D_MODEL), lambda i, j: (i, 0)),
            pl.BlockSpec((1, D_MODEL), lambda i, j: (0, 0)),
            pl.BlockSpec((None, 1, D_MODEL), grp),
            pl.BlockSpec((None, 1, D_MODEL), grp),
            pl.BlockSpec((D_MODEL, tn), lambda i, j: (0, j)),
        ],
        out_specs=pl.BlockSpec((TM, tn), lambda i, j: (i, j)),
        out_shape=jax.ShapeDtypeStruct((R_ALL, n), BF16),
        scratch_shapes=[pltpu.VMEM((TM, D_MODEL), BF16)],
        compiler_params=_cparams(("arbitrary", "arbitrary")),
        name="inproj",
    )(xall, norm_g.reshape(1, D_MODEL), shift, scale, w_bf16)


def _na_bias_tables():
    kinds = (0, 1, ROWS // NA_QROWS - 1)
    flat = np.zeros((3, NA_QROWS, GRID_W, NA_KROWS, GRID_W), np.int32)
    ok = np.zeros((3, NA_QROWS, GRID_W, NA_KROWS, GRID_W), bool)
    a = np.arange(NA_QROWS)[:, None, None, None]
    qc = np.arange(GRID_W)[None, :, None, None]
    r = np.arange(NA_KROWS)[None, None, :, None]
    kc = np.arange(GRID_W)[None, None, None, :]
    for v, j in enumerate(kinds):
        kb = np.clip(NA_QROWS * j - NA_WIN_H // 2, 0, ROWS - NA_KROWS)
        i = NA_QROWS * j + a
        r0 = np.clip(i - NA_WIN_H // 2, 0, ROWS - NA_WIN_H)
        krow = kb + r
        row_ok = (krow >= r0) & (krow < r0 + NA_WIN_H)
        drow = np.clip(krow - i + NA_WIN_H - 1, 0, 2 * NA_WIN_H - 2)
        cstart = np.clip(qc - NA_WIN_W // 2, 0, GRID_W - NA_WIN_W)
        col_ok = (kc >= cstart) & (kc < cstart + NA_WIN_W)
        dcol = np.clip(kc - qc, 1 - NA_WIN_W, NA_WIN_W - 1) + NA_WIN_W - 1
        flat[v] = drow * (2 * NA_WIN_W - 1) + dcol
        ok[v] = row_ok & col_ok
    shape = (3, NA_QT, NA_KROWS * GRID_W)
    return flat.reshape(shape), ok.reshape(shape)


_NA_FLAT, _NA_OK = _na_bias_tables()


def _na_bias(rpb):
    table = rpb.reshape(NA_HEADS, -1).astype(F32)
    gathered = jnp.take(table, jnp.asarray(_NA_FLAT), axis=1)
    return jnp.where(jnp.asarray(_NA_OK)[None], gathered, NEG_INF)


def _pair_attention(q, keys, vals, biases):
    lane = lax.broadcasted_iota(jnp.int32, (1, LANES), 1)
    scale = np.float32(NA_HEAD_DIM ** -0.5)
    out = jnp.zeros((q.shape[0], LANES), F32)
    for hh in range(2):
        in_head = (lane < NA_HEAD_DIM) if hh == 0 else (lane >= NA_HEAD_DIM)
        qm = jnp.where(in_head, q, jnp.zeros_like(q))
        s = []
        for t, k in enumerate(keys):
            st = _dot_nt(qm, k) * scale
            if biases[t] is not None:
                st = st + biases[t](hh)
            s.append(st)
        m = s[0].max(axis=-1, keepdims=True)
        for st in s[1:]:
            m = jnp.maximum(m, st.max(axis=-1, keepdims=True))
        den = jnp.zeros_like(m)
        acc = jnp.zeros((q.shape[0], LANES), F32)
        for st, v in zip(s, vals):
            p = jnp.exp(st - m)
            den = den + p.sum(axis=-1, keepdims=True)
            acc = acc + _dot(p.astype(BF16), v)
        out = jnp.where(in_head, acc / den, out)
    return out


def _na_kernel(q_ref, k0, k1, k2, k3, v0, v1, v2, v3, kc_ref, vc_ref, bias_ref, o_ref):
    keys = [k0[...], k1[...], k2[...], k3[...], kc_ref[...]]
    vals = [v0[...], v1[...], v2[...], v3[...], vc_ref[...]]
    biases = [
        (lambda hh, t=t: bias_ref[hh, :, NA_KT * t:NA_KT * (t + 1)]) for t in range(NA_NKB)
    ] + [None]
    o_ref[...] = _pair_attention(q_ref[...], keys, vals, biases).astype(o_ref.dtype)


def _na_attention(z, bias):
    nj = ROWS // NA_QROWS
    kblocks = SEQ // NA_KT
    qcol, kcol, vcol = 0, NA_WIDTH // LANES, 2 * NA_WIDTH // LANES

    def kbase(j):
        return jnp.clip(2 * j - 1, 0, kblocks - NA_NKB)

    def kspec(t, col):
        return pl.BlockSpec((NA_KT, LANES),
                            lambda p, j, b: (b * kblocks + kbase(j) + t, col + p))

    ctx_row = lambda b: R_LAT // CTX_LEN + b
    kind = lambda j: jnp.minimum(j, 1) + jnp.maximum(j - (nj - 2), 0)
    return pl.pallas_call(
        _na_kernel,
        grid=(NA_HEADS // 2, nj, BATCH),
        in_specs=[pl.BlockSpec((NA_QT, LANES), lambda p, j, b: (b * nj + j, qcol + p))]
        + [kspec(t, kcol) for t in range(NA_NKB)]
        + [kspec(t, vcol) for t in range(NA_NKB)]
        + [pl.BlockSpec((CTX_LEN, LANES), lambda p, j, b: (ctx_row(b), kcol + p)),
           pl.BlockSpec((CTX_LEN, LANES), lambda p, j, b: (ctx_row(b), vcol + p)),
           pl.BlockSpec((2, None, NA_QT, NA_KROWS * GRID_W), lambda p, j, b: (p, kind(j), 0, 0))],
        out_specs=pl.BlockSpec((NA_QT, LANES), lambda p, j, b: (b * nj + j, p)),
        out_shape=jax.ShapeDtypeStruct((R_LAT, NA_WIDTH), BF16),
        compiler_params=_cparams(("arbitrary", "arbitrary", "arbitrary")),
        name="na_attention",
    )(*([z] * (3 + 2 * NA_NKB)), bias)


def _ctx_attn_kernel(q_ref, k_ref, v_ref, o_ref):
    o_ref[...] = _pair_attention(q_ref[...], [k_ref[...]], [v_ref[...]], [None]).astype(o_ref.dtype)


def _ctx_attention(z):
    qcol, kcol, vcol = 0, NA_WIDTH // LANES, 2 * NA_WIDTH // LANES
    row = lambda b: R_LAT // CTX_LEN + b
    return pl.pallas_call(
        _ctx_attn_kernel,
        grid=(BATCH, NA_HEADS // 2),
        in_specs=[pl.BlockSpec((CTX_LEN, LANES), lambda b, p: (row(b), qcol + p)),
                  pl.BlockSpec((CTX_LEN, LANES), lambda b, p: (row(b), kcol + p)),
                  pl.BlockSpec((CTX_LEN, LANES), lambda b, p: (row(b), vcol + p))],
        out_specs=pl.BlockSpec((CTX_LEN, LANES), lambda b, p: (b, p)),
        out_shape=jax.ShapeDtypeStruct((R_CTX, NA_WIDTH), BF16),
        compiler_params=_cparams(("arbitrary", "arbitrary")),
        name="ctx_attention",
    )(z, z, z)


def _sgu_kernel(u_ref, v_ref, ng_ref, w_ref, bt_ref, o_ref):
    lane = lax.broadcasted_iota(jnp.int32, (1, LANES), 1)
    first = lane < SG_GROUP_DIM
    for ch in range(TM // SG_CHUNK):
        rows = slice(ch * SG_CHUNK, (ch + 1) * SG_CHUNK)
        u = _gelu_tanh(u_ref[rows, :].astype(F32))
        v = _gelu_tanh(v_ref[rows, :].astype(F32))
        vn = v * lax.rsqrt(jnp.mean(v * v, axis=-1, keepdims=True) + EPS) * ng_ref[...]
        vnb = vn.astype(BF16)
        for pr in range(SG_GROUPS // 2):
            cols = slice(pr * LANES, (pr + 1) * LANES)
            slab = vnb[:, cols]
            mixed = jnp.where(first, _dot(w_ref[2 * pr], slab), _dot(w_ref[2 * pr + 1], slab))
            o_ref[rows, cols] = (u[:, cols] * (mixed + bt_ref[:, cols])).astype(o_ref.dtype)


def _sgu(z, norm_g, w_bf16, bias_t):
    ucol = 3 * NA_WIDTH // SG_WIDTH
    return pl.pallas_call(
        _sgu_kernel,
        grid=(R_ALL // TM,),
        in_specs=[pl.BlockSpec((TM, SG_WIDTH), lambda i: (i, ucol)),
                  pl.BlockSpec((TM, SG_WIDTH), lambda i: (i, ucol + 1)),
                  pl.BlockSpec((1, SG_WIDTH), lambda i: (0, 0)),
                  pl.BlockSpec((SG_GROUPS, SG_CHUNK, SG_CHUNK), lambda i: (0, 0, 0)),
                  pl.BlockSpec((SG_CHUNK, SG_WIDTH), lambda i: (0, 0))],
        out_specs=pl.BlockSpec((TM, SG_WIDTH), lambda i: (i, 0)),
        out_shape=jax.ShapeDtypeStruct((R_ALL, SG_WIDTH), BF16),
        compiler_params=_cparams(("arbitrary",)),
        name="sgu",
    )(z, z, norm_g.reshape(1, SG_WIDTH), w_bf16, bias_t)


def _outproj_kernel(*refs, n_lhs):
    lhs = refs[:n_lhs]
    ws = refs[n_lhs:2 * n_lhs]
    x_ref, g1_ref, ng_ref, sh_ref, sc_ref, wr_ref, xo_ref, h2_ref, aff_ref = refs[2 * n_lhs:]
    y = _dot(lhs[0][...], ws[0][...])
    for a, w in zip(lhs[1:], ws[1:]):
        y = y + _dot(a[...], w[...])
    xn = x_ref[...] + g1_ref[...] * y
    xo_ref[...] = xn
    hn = xn * lax.rsqrt(jnp.mean(xn * xn, axis=-1, keepdims=True) + EPS) * ng_ref[...]
    h2 = (hn * (1.0 + sc_ref[...]) + sh_ref[...]).astype(BF16)
    h2_ref[...] = h2
    logits = _dot(h2, wr_ref[...])
    lane = lax.broadcasted_iota(jnp.int32, (1, LANES), 1)
    logits = jnp.where(lane < N_EXPERTS, logits, NEG_INF)
    e = jnp.exp(logits - logits.max(axis=-1, keepdims=True))
    aff_ref[...] = e / e.sum(axis=-1, keepdims=True)


def _outproj(lhs_list, w_list, xall, gate1, norm_g, shift, scale, w_router_pad, n_rows):
    n_lhs = len(lhs_list)
    grp = lambda i: (_group_of_tile(i), 0, 0)
    row = lambda i: (i, 0)
    const2 = lambda i: (0, 0)
    in_specs = (
        [pl.BlockSpec((TM, a.shape[1]), row) for a in lhs_list]
        + [pl.BlockSpec(w.shape, const2) for w in w_list]
        + [pl.BlockSpec((TM, D_MODEL), row),
           pl.BlockSpec((None, 1, D_MODEL), grp),
           pl.BlockSpec((1, D_MODEL), const2),
           pl.BlockSpec((None, 1, D_MODEL), grp),
           pl.BlockSpec((None, 1, D_MODEL), grp),
           pl.BlockSpec((D_MODEL, LANES), const2)])
    return pl.pallas_call(
        functools.partial(_outproj_kernel, n_lhs=n_lhs),
        grid=(n_rows // TM,),
        in_specs=in_specs,
        out_specs=[pl.BlockSpec((TM, D_MODEL), row),
                   pl.BlockSpec((TM, D_MODEL), row),
                   pl.BlockSpec((TM, LANES), row)],
        out_shape=[jax.ShapeDtypeStruct((n_rows, D_MODEL), F32),
                   jax.ShapeDtypeStruct((n_rows, D_MODEL), BF16),
                   jax.ShapeDtypeStruct((n_rows, LANES), F32)],
        compiler_params=_cparams(("arbitrary",)),
        name="outproj",
    )(*lhs_list, *w_list, xall, gate1, norm_g.reshape(1, D_MODEL), shift, scale, w_router_pad)


def _moe_kernel(x_ref, gate_ref, wg_ref, wu_ref, wd_ref, o_ref, wg_s, wu_s, wd_s, *, rows):
    f = pl.program_id(1)
    wg_s[...] = wg_ref[...].astype(BF16)
    wu_s[...] = wu_ref[...].astype(BF16)
    wd_s[...] = wd_ref[...].astype(BF16)
    rc = rows // MOE_ROW_SPLIT
    for r in range(MOE_ROW_SPLIT):
        sl = slice(r * rc, (r + 1) * rc)
        x = x_ref[sl, :]
        a = _dot(x, wg_s[...])
        b = _dot(x, wu_s[...])
        hid = (a * _sigmoid(a) * b).astype(BF16)
        part = _dot(hid, wd_s[...])

        @pl.when(f == 0)
        def _():
            o_ref[sl, :] = part

        @pl.when(f > 0)
        def _():
            o_ref[sl, :] += part

    @pl.when(f == pl.num_programs(1) - 1)
    def _():
        o_ref[...] = o_ref[...] * gate_ref[...]


def _moe_ffn(xe, gates, w_gate, w_up, w_down):
    rows = xe.shape[1]
    nf = D_FF_EXPERT // MOE_TF
    return pl.pallas_call(
        functools.partial(_moe_kernel, rows=rows),
        grid=(N_EXPERTS, nf),
        in_specs=[pl.BlockSpec((None, rows, D_MODEL), lambda e, f: (e, 0, 0)),
                  pl.BlockSpec((None, rows, 1), lambda e, f: (e, 0, 0)),
                  pl.BlockSpec((None, D_MODEL, MOE_TF), lambda e, f: (e, 0, f)),
                  pl.BlockSpec((None, D_MODEL, MOE_TF), lambda e, f: (e, 0, f)),
                  pl.BlockSpec((None, MOE_TF, D_MODEL), lambda e, f: (e, f, 0))],
        out_specs=pl.BlockSpec((None, rows, D_MODEL), lambda e, f: (e, 0, 0)),
        out_shape=jax.ShapeDtypeStruct((N_EXPERTS, rows, D_MODEL), F32),
        scratch_shapes=[pltpu.VMEM((D_MODEL, MOE_TF), BF16),
                        pltpu.VMEM((D_MODEL, MOE_TF), BF16),
                        pltpu.VMEM((MOE_TF, D_MODEL), BF16)],
        compiler_params=_cparams(("arbitrary", "arbitrary")),
        name="moe_ffn",
    )(xe, gates, w_gate, w_up, w_down)


def _expert_choice(h2, aff, w_gate, w_up, w_down, with_ctx):
    cap = EC_CAPACITY * SEQ // N_EXPERTS
    aff_lat = aff[:R_LAT, :N_EXPERTS].reshape(BATCH, SEQ, N_EXPERTS)
    gate, idx = lax.top_k(jnp.swapaxes(aff_lat, 1, 2), cap)
    flat = idx + (jnp.arange(BATCH, dtype=idx.dtype) * SEQ)[:, None, None]
    flat = jnp.swapaxes(flat, 0, 1).reshape(N_EXPERTS, BATCH * cap)
    gate = jnp.swapaxes(gate, 0, 1).reshape(N_EXPERTS, BATCH * cap)
    if with_ctx:
        cap_c = EC_CAPACITY * CTX_LEN // N_EXPERTS
        aff_c = aff[R_LAT:, :N_EXPERTS].reshape(BATCH, CTX_LEN, N_EXPERTS)
        gate_c, idx_c = lax.top_k(jnp.swapaxes(aff_c, 1, 2), cap_c)
        flat_c = idx_c + R_LAT + (jnp.arange(BATCH, dtype=idx_c.dtype) * CTX_LEN)[:, None, None]
        flat = jnp.concatenate(
            [flat, jnp.swapaxes(flat_c, 0, 1).reshape(N_EXPERTS, BATCH * cap_c)], axis=1)
        gate = jnp.concatenate(
            [gate, jnp.swapaxes(gate_c, 0, 1).reshape(N_EXPERTS, BATCH * cap_c)], axis=1)
    xe = h2[flat]
    ye = _moe_ffn(xe, gate[..., None], w_gate, w_up, w_down)
    n_rows = h2.shape[0]
    return jnp.zeros((n_rows, D_MODEL), F32).at[flat.reshape(-1)].add(ye.reshape(-1, D_MODEL))


def _residual_kernel(x_ref, g_ref, m_ref, ng_ref, o_ref, *, final_norm):
    xn = x_ref[...] + g_ref[...] * m_ref[...]
    if final_norm:
        xn = xn * lax.rsqrt(jnp.mean(xn * xn, axis=-1, keepdims=True) + EPS) * ng_ref[...]
    o_ref[...] = xn


def _residual(x1, gate2, moe, norm_g, final_norm):
    n_rows = x1.shape[0]
    row = lambda i: (i, 0)
    return pl.pallas_call(
        functools.partial(_residual_kernel, final_norm=final_norm),
        grid=(n_rows // TM,),
        in_specs=[pl.BlockSpec((TM, D_MODEL), row),
                  pl.BlockSpec((None, 1, D_MODEL), lambda i: (_group_of_tile(i), 0, 0)),
                  pl.BlockSpec((TM, D_MODEL), row),
                  pl.BlockSpec((1, D_MODEL), lambda i: (0, 0))],
        out_specs=pl.BlockSpec((TM, D_MODEL), row),
        out_shape=jax.ShapeDtypeStruct((n_rows, D_MODEL), F32),
        compiler_params=_cparams(("arbitrary",)),
        name="residual",
    )(x1, gate2, moe, norm_g.reshape(1, D_MODEL))


def _rope(t, cos, sin):
    half = RET_QK_DIM // 2
    t1, t2 = t[:, :half], t[:, half:]
    return jnp.concatenate([t1 * cos - t2 * sin, t1 * sin + t2 * cos], axis=-1)


def _retention_kernel(lg_ref, q_ref, k_ref, v_ref, cos_ref, sin_ref, kc_ref, vc_ref, *rest,
                      backward):
    if backward:
        o_ref, s_ref, dmat_ref, qd_ref, kd_ref = rest
    else:
        ob_ref, g_ref, o_ref, s_ref, dmat_ref, qd_ref, kd_ref = rest
    c = RET_CHUNK
    h = pl.program_id(1)
    lg = lg_ref[1 if backward else 0, h]
    kscale = np.float32(RET_QK_DIM ** -0.5)

    @pl.when(pl.program_id(2) == 0)
    def _():
        ia = lax.broadcasted_iota(jnp.int32, (c, c), 0)
        ib = lax.broadcasted_iota(jnp.int32, (c, c), 1)
        diff = ((ib - ia) if backward else (ia - ib)).astype(F32)
        dmat_ref[...] = jnp.where(diff >= 0, jnp.exp(jnp.maximum(diff, 0.0) * lg), 0.0)
        pos = lax.broadcasted_iota(jnp.int32, (c, RET_QK_DIM), 0).astype(F32)
        if backward:
            qd_ref[...] = jnp.exp((c - pos) * lg)
            kd_ref[...] = jnp.exp(pos * lg) * kscale
        else:
            qd_ref[...] = jnp.exp((pos + 1.0) * lg)
            kd_ref[...] = jnp.exp((c - 1.0 - pos) * lg) * kscale
        cpos = lax.broadcasted_iota(jnp.int32, (CTX_LEN, RET_QK_DIM), 0).astype(F32)
        cw = jnp.exp(cpos * lg) if backward else jnp.exp((CTX_LEN - 1.0 - cpos) * lg)
        kcw = (kc_ref[...].astype(F32) * (cw * kscale)).astype(BF16)
        s_ref[...] = _dot_tn(kcw, vc_ref[...])

    cos = cos_ref[...]
    sin = sin_ref[...]
    q = _rope(q_ref[...].astype(F32), cos, sin)
    k = _rope(k_ref[...].astype(F32), cos, sin)
    v = v_ref[...]
    qb = q.astype(BF16)
    att = _dot_nt(qb, (k * kscale).astype(BF16)) * dmat_ref[...]
    s_old = s_ref[...]
    o = _dot(att.astype(BF16), v) + _dot((q * qd_ref[...]).astype(BF16), s_old.astype(BF16))
    chunk_decay = jnp.exp(jnp.zeros((1, RET_V_DIM), F32) + c * lg)
    s_ref[...] = s_old * chunk_decay + _dot_tn((k * kd_ref[...]).astype(BF16), v)
    if backward:
        o_ref[...] = o.astype(o_ref.dtype)
    else:
        of = o + ob_ref[...].astype(F32)
        y = of * lax.rsqrt(jnp.mean(of * of, axis=-1, keepdims=True) + EPS)
        g = g_ref[...].astype(F32)
        o_ref[...] = (g * _sigmoid(g) * y).astype(o_ref.dtype)


def _retention_pass(log_gamma, z, cos, sin, o_back):
    backward = o_back is None
    c = RET_CHUNK
    n = SEQ // c
    half = RET_QK_DIM // 2
    qk_cols = RET_QK_WIDTH // RET_QK_DIM
    v_base = 2 * RET_QK_WIDTH // RET_V_DIM
    g_base = v_base + RET_V_WIDTH // RET_V_DIM
    chunk = (lambda t: n - 1 - t) if backward else (lambda t: t)
    row = lambda b, t: b * n + chunk(t)
    ctx_row = lambda b: R_LAT // CTX_LEN + b
    in_specs = [
        pl.BlockSpec((c, RET_QK_DIM), lambda b, h, t, lg: (row(b, t), h)),
        pl.BlockSpec((c, RET_QK_DIM), lambda b, h, t, lg: (row(b, t), qk_cols + h)),
        pl.BlockSpec((c, RET_V_DIM), lambda b, h, t, lg: (row(b, t), v_base + h)),
        pl.BlockSpec((c, half), lambda b, h, t, lg: (chunk(t), 0)),
        pl.BlockSpec((c, half), lambda b, h, t, lg: (chunk(t), 0)),
        pl.BlockSpec((CTX_LEN, RET_QK_DIM), lambda b, h, t, lg: (ctx_row(b), qk_cols + h)),
        pl.BlockSpec((CTX_LEN, RET_V_DIM), lambda b, h, t, lg: (ctx_row(b), v_base + h)),
    ]
    args = [z, z, z, cos, sin, z, z]
    if not backward:
        in_specs += [
            pl.BlockSpec((c, RET_V_DIM), lambda b, h, t, lg: (row(b, t), h)),
            pl.BlockSpec((c, RET_V_DIM), lambda b, h, t, lg: (row(b, t), g_base + h)),
        ]
        args += [o_back, z]
    return pl.pallas_call(
        functools.partial(_retention_kernel, backward=backward),
        grid_spec=pltpu.PrefetchScalarGridSpec(
            num_scalar_prefetch=1,
            grid=(BATCH, RET_HEADS, n),
            in_specs=in_specs,
            out_specs=pl.BlockSpec((c, RET_V_DIM), lambda b, h, t, lg: (row(b, t), h)),
            scratch_shapes=[pltpu.VMEM((RET_QK_DIM, RET_V_DIM), F32),
                            pltpu.VMEM((c, c), F32),
                            pltpu.VMEM((c, RET_QK_DIM), F32),
                            pltpu.VMEM((c, RET_QK_DIM), F32)]),
        out_shape=jax.ShapeDtypeStruct((R_LAT, RET_V_WIDTH), BF16),
        compiler_params=_cparams(("arbitrary", "arbitrary", "arbitrary")),
        name="retention_bwd" if backward else "retention_fwd",
    )(log_gamma, *args)


def _rope_tables():
    axis_dim = RET_QK_DIM // 4
    inv = 1.0 / (ROPE_BASE ** (jnp.arange(0, 2 * axis_dim, 2, dtype=F32) / (2 * axis_dim)))
    t = jnp.arange(SEQ)
    r = (t // GRID_W).astype(F32)
    col = (t % GRID_W).astype(F32)
    ang = jnp.concatenate([r[:, None] * inv, col[:, None] * inv], axis=-1)
    return jnp.cos(ang), jnp.sin(ang)


def _split_mod(mod_layer):
    m = mod_layer[:N_GROUPS].reshape(N_GROUPS, 6, 1, D_MODEL)
    return [m[:, k] for k in range(6)]


def _pad_router(w_router):
    return jnp.pad(w_router, ((0, 0), (0, LANES - N_EXPERTS))).astype(BF16)


def kernel(x, c, ctx, c_ctx, ada_w, ada_b, norm1_g, norm2_g, ab_w_in, ab_w_out, na_rpb, sgu_norm_g, sgu_w, sgu_b, ret_w_in, ret_w_out, ret_decay_logit, moe_router, moe_w_gate, moe_w_up, moe_w_down, final_norm_g):
    assert DEPTH == 2 and x.shape == (BATCH, SEQ, D_MODEL) and ctx.shape == (BATCH, CTX_LEN, D_MODEL)

    cvec = jnp.zeros((8, D_MODEL), F32).at[:BATCH].set(c).at[BATCH].set(c_ctx)
    mod = _adaln(cvec, ada_w, ada_b)
    xall = jnp.concatenate([x.reshape(R_LAT, D_MODEL), ctx.reshape(R_CTX, D_MODEL)], axis=0)

    sh1, sc1, g1, sh2, sc2, g2 = _split_mod(mod[0])
    z = _inproj(xall, norm1_g[0], sh1, sc1, ab_w_in[0].astype(BF16), tn=AB_IN)
    a_all = jnp.concatenate([_na_attention(z, _na_bias(na_rpb[0])), _ctx_attention(z)], axis=0)
    bias_t = jnp.repeat(sgu_b[0].T, SG_GROUP_DIM, axis=1)
    bsg = _sgu(z, sgu_norm_g[0], sgu_w[0].astype(BF16), bias_t)
    w_out = ab_w_out[0].astype(BF16)
    x1, h2, aff = _outproj([a_all, bsg], [w_out[:NA_WIDTH], w_out[NA_WIDTH:]], xall, g1,
                           norm2_g[0], sh2, sc2, _pad_router(moe_router[0]), R_ALL)
    moe = _expert_choice(h2, aff, moe_w_gate[0], moe_w_up[0], moe_w_down[0], with_ctx=True)
    xall = _residual(x1, g2, moe, final_norm_g, final_norm=False)

    sh1, sc1, g1, sh2, sc2, g2 = _split_mod(mod[1])
    z = _inproj(xall, norm1_g[1], sh1, sc1, ret_w_in[0].astype(BF16), tn=2048)
    log_gamma = jax.nn.log_sigmoid(ret_decay_logit[0].astype(F32))
    cos, sin = _rope_tables()
    o_back = _retention_pass(log_gamma, z, cos, sin, None)
    ypre = _retention_pass(log_gamma, z, cos, sin, o_back)
    x1, h2, aff = _outproj([ypre], [ret_w_out[0].astype(BF16)], xall, g1,
                           norm2_g[1], sh2, sc2, _pad_router(moe_router[1]), R_LAT)
    moe = _expert_choice(h2, aff, moe_w_gate[1], moe_w_up[1], moe_w_down[1], with_ctx=False)
    out = _residual(x1, g2, moe, final_norm_g, final_norm=True)
    return out.reshape(BATCH, SEQ, D_MODEL)
```

```python
import functools

import numpy as np
import jax
import jax.numpy as jnp
from jax import lax
from jax.experimental import pallas as pl
from jax.experimental.pallas import tpu as pltpu

F32 = jnp.float32
BF16 = jnp.bfloat16

D_MODEL = 1024
BATCH = 2
SEQ = 8192
DEPTH = 2
GRID_W = 64
CTX_LEN = 256
EPS = 1e-6
NEG_INF = -1e30

NA_HEADS = 8
NA_HEAD_DIM = 64
NA_WIN_H = 8
NA_WIN_W = 16
NA_WIDTH = NA_HEADS * NA_HEAD_DIM
SG_GROUPS = 8
SG_GROUP_DIM = 64
SG_CHUNK = 128
SG_WIDTH = SG_GROUPS * SG_GROUP_DIM
AB_IN = 3 * NA_WIDTH + 2 * SG_WIDTH

RET_HEADS = 4
RET_QK_DIM = 256
RET_V_DIM = 512
RET_QK_WIDTH = RET_HEADS * RET_QK_DIM
RET_V_WIDTH = RET_HEADS * RET_V_DIM
RET_IN = 2 * RET_QK_WIDTH + 2 * RET_V_WIDTH
ROPE_BASE = 10000.0

N_EXPERTS = 16
EC_CAPACITY = 2
D_FF_EXPERT = 2816

ROWS = SEQ // GRID_W
R_LAT = BATCH * SEQ
R_CTX = BATCH * CTX_LEN
R_ALL = R_LAT + R_CTX
N_GROUPS = BATCH + 1

LANES = 128
TM = 512
VMEM_LIMIT = 56 * 1024 * 1024

NA_QROWS = 8
NA_KROWS = 16
NA_QT = NA_QROWS * GRID_W
NA_KT = 256
NA_NKB = NA_KROWS * GRID_W // NA_KT
RET_CHUNK = 256
MOE_TF = 256
MOE_ROW_SPLIT = 4


def _group_of_tile(i):
    return jnp.minimum(i // (SEQ // TM), BATCH)


def _cparams(sem, vmem=VMEM_LIMIT):
    return pltpu.CompilerParams(dimension_semantics=sem, vmem_limit_bytes=vmem)


def _dot(a, b):
    return jnp.dot(a, b, preferred_element_type=F32)


def _dot_nt(a, b):
    return lax.dot_general(a, b, (((1,), (1,)), ((), ())), preferred_element_type=F32)


def _dot_tn(a, b):
    return lax.dot_general(a, b, (((0,), (0,)), ((), ())), preferred_element_type=F32)


def _sigmoid(x):
    return 1.0 / (1.0 + jnp.exp(-x))


def _gelu_tanh(x):
    c = np.float32(np.sqrt(2.0 / np.pi))
    return 0.5 * x * (1.0 + jnp.tanh(c * (x + np.float32(0.044715) * (x * x * x))))


def _adaln_kernel(c_ref, w_ref, b_ref, o_ref):
    cv = c_ref[...]
    s = (cv * _sigmoid(cv)).astype(BF16)
    o_ref[...] = _dot(s, w_ref[...].astype(BF16)) + b_ref[...]


def _adaln(cvec, ada_w, ada_b):
    tn = 1024
    n = 6 * D_MODEL
    return pl.pallas_call(
        _adaln_kernel,
        grid=(DEPTH, n // tn),
        in_specs=[
            pl.BlockSpec((8, D_MODEL), lambda l, j: (0, 0)),
            pl.BlockSpec((None, D_MODEL, tn), lambda l, j: (l, 0, j)),
            pl.BlockSpec((None, 1, tn), lambda l, j: (l, 0, j)),
        ],
        out_specs=pl.BlockSpec((None, 8, tn), lambda l, j: (l, 0, j)),
        out_shape=jax.ShapeDtypeStruct((DEPTH, 8, n), F32),
        compiler_params=_cparams(("arbitrary", "arbitrary")),
        name="adaln",
    )(cvec, ada_w, ada_b.reshape(DEPTH, 1, n))


def _inproj_kernel(x_ref, g_ref, sh_ref, sc_ref, w_ref, z_ref, hb_ref):
    @pl.when(pl.program_id(1) == 0)
    def _():
        x = x_ref[...]
        y = x * lax.rsqrt(jnp.mean(x * x, axis=-1, keepdims=True) + EPS) * g_ref[...]
        hb_ref[...] = (y * (1.0 + sc_ref[...]) + sh_ref[...]).astype(BF16)

    z_ref[...] = _dot(hb_ref[...], w_ref[...]).astype(z_ref.dtype)


def _inproj(xall, norm_g, shift, scale, w_bf16, tn):
    n = w_bf16.shape[1]
    grp = lambda i, j: (_group_of_tile(i), 0, 0)
    return pl.pallas_call(
        _inproj_kernel,
        grid=(R_ALL // TM, n // tn),
        in_specs=[
            pl.BlockSpec((TM, D_MODEL), lambda i, j: (i, 0)),
            pl.BlockSpec((1, D_MODEL), lambda i, j: (0, 0)),
            pl.BlockSpec((None, 1, D_MODEL), grp),
            pl.BlockSpec((None, 1, D_MODEL), grp),
            pl.BlockSpec((D_MODEL, tn), lambda i, j: (0, j)),
        ],
        out_specs=pl.BlockSpec((TM, tn), lambda i, j: (i, j)),
        out_shape=jax.ShapeDtypeStruct((R_ALL, n), BF16),
        scratch_shapes=[pltpu.VMEM((TM, D_MODEL), BF16)],
        compiler_params=_cparams(("arbitrary", "arbitrary")),
        name="inproj",
    )(xall, norm_g.reshape(1, D_MODEL), shift, scale, w_bf16)


def _na_tile_codes():
    masked = 2 * NA_WIN_H - 1
    kinds = (0, 1, ROWS // NA_QROWS - 1)
    d = np.full((3, NA_QROWS, NA_KROWS), masked, np.int64)
    for v, j in enumerate(kinds):
        kb = int(np.clip(NA_QROWS * j - NA_WIN_H // 2, 0, ROWS - NA_KROWS))
        for a in range(NA_QROWS):
            i = NA_QROWS * j + a
            r0 = int(np.clip(i - NA_WIN_H // 2, 0, ROWS - NA_WIN_H))
            for r in range(NA_KROWS):
                if r0 <= kb + r < r0 + NA_WIN_H:
                    d[v, a, r] = kb + r - i + NA_WIN_H - 1
    pairs = d.reshape(-1, 2)
    uniq = sorted(set(map(tuple, pairs)))
    code = np.array([uniq.index(tuple(p)) for p in pairs], np.int32)
    return code, np.array(uniq, np.int64)


_NA_CODES, _NA_CODE_ROWS = _na_tile_codes()


def _na_bias_tiles(rpb):
    qc = np.arange(GRID_W)[:, None]
    kc = np.arange(GRID_W)[None, :]
    cstart = np.clip(qc - NA_WIN_W // 2, 0, GRID_W - NA_WIN_W)
    col_ok = (kc >= cstart) & (kc < cstart + NA_WIN_W)
    dcol = np.clip(kc - qc, 1 - NA_WIN_W, NA_WIN_W - 1) + NA_WIN_W - 1
    onehot = (dcol.reshape(1, -1) == np.arange(2 * NA_WIN_W - 1)[:, None]).astype(np.float32)
    toe = jnp.einsum('hdm,mq->hdq', rpb.astype(F32), jnp.asarray(onehot),
                     precision=lax.Precision.HIGHEST)
    toe = toe.reshape(NA_HEADS, 2 * NA_WIN_H - 1, GRID_W, GRID_W)
    toe = jnp.where(jnp.asarray(col_ok), toe, NEG_INF)
    toe = jnp.concatenate([toe, jnp.full((NA_HEADS, 1, GRID_W, GRID_W), NEG_INF, F32)], axis=1)
    return jnp.concatenate([toe[:, _NA_CODE_ROWS[:, 0]], toe[:, _NA_CODE_ROWS[:, 1]]], axis=-1)


def _pair_attention(q, keys, vals, add_bias=None):
    lane = lax.broadcasted_iota(jnp.int32, (1, LANES), 1)
    scale = np.float32(NA_HEAD_DIM ** -0.5)
    out = jnp.zeros((q.shape[0], LANES), F32)
    for hh in range(2):
        in_head = (lane < NA_HEAD_DIM) if hh == 0 else (lane >= NA_HEAD_DIM)
        qm = jnp.where(in_head, q, jnp.zeros_like(q))
        s = []
        for t, k in enumerate(keys):
            st = _dot_nt(qm, k) * scale
            if add_bias is not None:
                st = add_bias(hh, t, st)
            s.append(st)
        m = s[0].max(axis=-1, keepdims=True)
        for st in s[1:]:
            m = jnp.maximum(m, st.max(axis=-1, keepdims=True))
        den = jnp.zeros_like(m)
        acc = jnp.zeros((q.shape[0], LANES), F32)
        for st, v in zip(s, vals):
            p = jnp.exp(st - m)
            den = den + p.sum(axis=-1, keepdims=True)
            acc = acc + _dot(p.astype(BF16), v)
        out = jnp.where(in_head, acc / den, out)
    return out


def _na_kernel(code_ref, q_ref, k0, k1, k2, k3, v0, v1, v2, v3, kc_ref, vc_ref, tile_ref, o_ref):
    keys = [k0[...], k1[...], k2[...], k3[...], kc_ref[...]]
    vals = [v0[...], v1[...], v2[...], v3[...], vc_ref[...]]
    nj = pl.num_programs(1)
    j = pl.program_id(1)
    kind = jnp.minimum(j, 1) + jnp.maximum(j - (nj - 2), 0)
    tiles_per_row = NA_KROWS // 2
    code_base = kind * (NA_QROWS * tiles_per_row)

    def add_bias(hh, t, st):
        if t >= NA_NKB:
            return st
        rows = []
        for a in range(NA_QROWS):
            cols = []
            for u in range(NA_KT // LANES):
                code = code_ref[code_base + a * tiles_per_row + t * (NA_KT // LANES) + u]
                piece = st[a * GRID_W:(a + 1) * GRID_W, u * LANES:(u + 1) * LANES]
                cols.append(piece + tile_ref[hh, code])
            rows.append(jnp.concatenate(cols, axis=1))
        return jnp.concatenate(rows, axis=0)

    o_ref[...] = _pair_attention(q_ref[...], keys, vals, add_bias).astype(o_ref.dtype)


def _na_attention(z, tiles):
    nj = ROWS // NA_QROWS
    kblocks = SEQ // NA_KT
    qcol, kcol, vcol = 0, NA_WIDTH // LANES, 2 * NA_WIDTH // LANES
    n_codes = tiles.shape[1]

    def kbase(j):
        return jnp.clip(2 * j - 1, 0, kblocks - NA_NKB)

    def kspec(t, col):
        return pl.BlockSpec((NA_KT, LANES),
                            lambda p, j, b, code: (b * kblocks + kbase(j) + t, col + p))

    ctx_row = lambda b: R_LAT // CTX_LEN + b
    in_specs = (
        [pl.BlockSpec((NA_QT, LANES), lambda p, j, b, code: (b * nj + j, qcol + p))]
        + [kspec(t, kcol) for t in range(NA_NKB)]
        + [kspec(t, vcol) for t in range(NA_NKB)]
        + [pl.BlockSpec((CTX_LEN, LANES), lambda p, j, b, code: (ctx_row(b), kcol + p)),
           pl.BlockSpec((CTX_LEN, LANES), lambda p, j, b, code: (ctx_row(b), vcol + p)),
           pl.BlockSpec((2, n_codes, GRID_W, LANES), lambda p, j, b, code: (p, 0, 0, 0))])
    return pl.pallas_call(
        _na_kernel,
        grid_spec=pltpu.PrefetchScalarGridSpec(
            num_scalar_prefetch=1,
            grid=(NA_HEADS // 2, nj, BATCH),
            in_specs=in_specs,
            out_specs=pl.BlockSpec((NA_QT, LANES), lambda p, j, b, code: (b * nj + j, p))),
        out_shape=jax.ShapeDtypeStruct((R_LAT, NA_WIDTH), BF16),
        compiler_params=_cparams(("arbitrary", "arbitrary", "arbitrary")),
        name="na_attention",
    )(jnp.asarray(_NA_CODES), *([z] * (3 + 2 * NA_NKB)), tiles)


def _ctx_attn_kernel(q_ref, k_ref, v_ref, o_ref):
    o_ref[...] = _pair_attention(q_ref[...], [k_ref[...]], [v_ref[...]]).astype(o_ref.dtype)


def _ctx_attention(z):
    qcol, kcol, vcol = 0, NA_WIDTH // LANES, 2 * NA_WIDTH // LANES
    row = lambda b: R_LAT // CTX_LEN + b
    return pl.pallas_call(
        _ctx_attn_kernel,
        grid=(BATCH, NA_HEADS // 2),
        in_specs=[pl.BlockSpec((CTX_LEN, LANES), lambda b, p: (row(b), qcol + p)),
                  pl.BlockSpec((CTX_LEN, LANES), lambda b, p: (row(b), kcol + p)),
                  pl.BlockSpec((CTX_LEN, LANES), lambda b, p: (row(b), vcol + p))],
        out_specs=pl.BlockSpec((CTX_LEN, LANES), lambda b, p: (b, p)),
        out_shape=jax.ShapeDtypeStruct((R_CTX, NA_WIDTH), BF16),
        compiler_params=_cparams(("arbitrary", "arbitrary")),
        name="ctx_attention",
    )(z, z, z)


def _sgu_kernel(u_ref, v_ref, ng_ref, w_ref, bt_ref, o_ref):
    lane = lax.broadcasted_iota(jnp.int32, (1, LANES), 1)
    first = lane < SG_GROUP_DIM
    for ch in range(TM // SG_CHUNK):
        rows = slice(ch * SG_CHUNK, (ch + 1) * SG_CHUNK)
        u = _gelu_tanh(u_ref[rows, :].astype(F32))
        v = _gelu_tanh(v_ref[rows, :].astype(F32))
        vn = v * lax.rsqrt(jnp.mean(v * v, axis=-1, keepdims=True) + EPS) * ng_ref[...]
        vnb = vn.astype(BF16)
        for pr in range(SG_GROUPS // 2):
            cols = slice(pr * LANES, (pr + 1) * LANES)
            slab = vnb[:, cols]
            mixed = jnp.where(first, _dot(w_ref[2 * pr], slab), _dot(w_ref[2 * pr + 1], slab))
            o_ref[rows, cols] = (u[:, cols] * (mixed + bt_ref[:, cols])).astype(o_ref.dtype)


def _sgu(z, norm_g, w_bf16, bias_t):
    ucol = 3 * NA_WIDTH // SG_WIDTH
    return pl.pallas_call(
        _sgu_kernel,
        grid=(R_ALL // TM,),
        in_specs=[pl.BlockSpec((TM, SG_WIDTH), lambda i: (i, ucol)),
                  pl.BlockSpec((TM, SG_WIDTH), lambda i: (i, ucol + 1)),
                  pl.BlockSpec((1, SG_WIDTH), lambda i: (0, 0)),
                  pl.BlockSpec((SG_GROUPS, SG_CHUNK, SG_CHUNK), lambda i: (0, 0, 0)),
                  pl.BlockSpec((SG_CHUNK, SG_WIDTH), lambda i: (0, 0))],
        out_specs=pl.BlockSpec((TM, SG_WIDTH), lambda i: (i, 0)),
        out_shape=jax.ShapeDtypeStruct((R_ALL, SG_WIDTH), BF16),
        compiler_params=_cparams(("arbitrary",)),
        name="sgu",
    )(z, z, norm_g.reshape(1, SG_WIDTH), w_bf16, bias_t)


def _outproj_kernel(*refs, n_lhs):
    lhs = refs[:n_lhs]
    ws = refs[n_lhs:2 * n_lhs]
    x_ref, g1_ref, ng_ref, sh_ref, sc_ref, wr_ref, xo_ref, h2_ref, aff_ref = refs[2 * n_lhs:]
    y = _dot(lhs[0][...], ws[0][...])
    for a, w in zip(lhs[1:], ws[1:]):
        y = y + _dot(a[...], w[...])
    xn = x_ref[...] + g1_ref[...] * y
    xo_ref[...] = xn
    hn = xn * lax.rsqrt(jnp.mean(xn * xn, axis=-1, keepdims=True) + EPS) * ng_ref[...]
    h2 = (hn * (1.0 + sc_ref[...]) + sh_ref[...]).astype(BF16)
    h2_ref[...] = h2
    logits = _dot(h2, wr_ref[...])
    lane = lax.broadcasted_iota(jnp.int32, (1, LANES), 1)
    logits = jnp.where(lane < N_EXPERTS, logits, NEG_INF)
    e = jnp.exp(logits - logits.max(axis=-1, keepdims=True))
    aff_ref[...] = e / e.sum(axis=-1, keepdims=True)


def _outproj(lhs_list, w_list, xall, gate1, norm_g, shift, scale, w_router_pad, n_rows):
    n_lhs = len(lhs_list)
    grp = lambda i: (_group_of_tile(i), 0, 0)
    row = lambda i: (i, 0)
    const2 = lambda i: (0, 0)
    in_specs = (
        [pl.BlockSpec((TM, a.shape[1]), row) for a in lhs_list]
        + [pl.BlockSpec(w.shape, const2) for w in w_list]
        + [pl.BlockSpec((TM, D_MODEL), row),
           pl.BlockSpec((None, 1, D_MODEL), grp),
           pl.BlockSpec((1, D_MODEL), const2),
           pl.BlockSpec((None, 1, D_MODEL), grp),
           pl.BlockSpec((None, 1, D_MODEL), grp),
           pl.BlockSpec((D_MODEL, LANES), const2)])
    return pl.pallas_call(
        functools.partial(_outproj_kernel, n_lhs=n_lhs),
        grid=(n_rows // TM,),
        in_specs=in_specs,
        out_specs=[pl.BlockSpec((TM, D_MODEL), row),
                   pl.BlockSpec((TM, D_MODEL), row),
                   pl.BlockSpec((TM, LANES), row)],
        out_shape=[jax.ShapeDtypeStruct((n_rows, D_MODEL), F32),
                   jax.ShapeDtypeStruct((n_rows, D_MODEL), BF16),
                   jax.ShapeDtypeStruct((n_rows, LANES), F32)],
        compiler_params=_cparams(("arbitrary",)),
        name="outproj",
    )(*lhs_list, *w_list, xall, gate1, norm_g.reshape(1, D_MODEL), shift, scale, w_router_pad)


def _moe_kernel(x_ref, gate_ref, wg_ref, wu_ref, wd_ref, o_ref, wg_s, wu_s, wd_s, *, rows):
    f = pl.program_id(1)
    wg_s[...] = wg_ref[...].astype(BF16)
    wu_s[...] = wu_ref[...].astype(BF16)
    wd_s[...] = wd_ref[...].astype(BF16)
    rc = rows // MOE_ROW_SPLIT
    for r in range(MOE_ROW_SPLIT):
        sl = slice(r * rc, (r + 1) * rc)
        x = x_ref[sl, :]
        a = _dot(x, wg_s[...])
        b = _dot(x, wu_s[...])
        hid = (a * _sigmoid(a) * b).astype(BF16)
        part = _dot(hid, wd_s[...])

        @pl.when(f == 0)
        def _():
            o_ref[sl, :] = part

        @pl.when(f > 0)
        def _():
            o_ref[sl, :] += part

    @pl.when(f == pl.num_programs(1) - 1)
    def _():
        o_ref[...] = o_ref[...] * gate_ref[...]


def _moe_ffn(xe, gates, w_gate, w_up, w_down, layer):
    rows = xe.shape[1]
    nf = D_FF_EXPERT // MOE_TF
    return pl.pallas_call(
        functools.partial(_moe_kernel, rows=rows),
        grid=(N_EXPERTS, nf),
        in_specs=[pl.BlockSpec((None, rows, D_MODEL), lambda e, f: (e, 0, 0)),
                  pl.BlockSpec((None, rows, 1), lambda e, f: (e, 0, 0)),
                  pl.BlockSpec((None, None, D_MODEL, MOE_TF), lambda e, f: (layer, e, 0, f)),
                  pl.BlockSpec((None, None, D_MODEL, MOE_TF), lambda e, f: (layer, e, 0, f)),
                  pl.BlockSpec((None, None, MOE_TF, D_MODEL), lambda e, f: (layer, e, f, 0))],
        out_specs=pl.BlockSpec((None, rows, D_MODEL), lambda e, f: (e, 0, 0)),
        out_shape=jax.ShapeDtypeStruct((N_EXPERTS, rows, D_MODEL), F32),
        scratch_shapes=[pltpu.VMEM((D_MODEL, MOE_TF), BF16),
                        pltpu.VMEM((D_MODEL, MOE_TF), BF16),
                        pltpu.VMEM((MOE_TF, D_MODEL), BF16)],
        compiler_params=_cparams(("arbitrary", "arbitrary")),
        name="moe_ffn",
    )(xe, gates, w_gate, w_up, w_down)


def _expert_choice(h2, aff, w_gate, w_up, w_down, layer, with_ctx):
    cap = EC_CAPACITY * SEQ // N_EXPERTS
    aff_lat = aff[:R_LAT, :N_EXPERTS].reshape(BATCH, SEQ, N_EXPERTS)
    gate, idx = lax.top_k(jnp.swapaxes(aff_lat, 1, 2), cap)
    flat = idx + (jnp.arange(BATCH, dtype=idx.dtype) * SEQ)[:, None, None]
    flat = jnp.swapaxes(flat, 0, 1).reshape(N_EXPERTS, BATCH * cap)
    gate = jnp.swapaxes(gate, 0, 1).reshape(N_EXPERTS, BATCH * cap)
    if with_ctx:
        cap_c = EC_CAPACITY * CTX_LEN // N_EXPERTS
        aff_c = aff[R_LAT:, :N_EXPERTS].reshape(BATCH, CTX_LEN, N_EXPERTS)
        gate_c, idx_c = lax.top_k(jnp.swapaxes(aff_c, 1, 2), cap_c)
        flat_c = idx_c + R_LAT + (jnp.arange(BATCH, dtype=idx_c.dtype) * CTX_LEN)[:, None, None]
        flat = jnp.concatenate(
            [flat, jnp.swapaxes(flat_c, 0, 1).reshape(N_EXPERTS, BATCH * cap_c)], axis=1)
        gate = jnp.concatenate(
            [gate, jnp.swapaxes(gate_c, 0, 1).reshape(N_EXPERTS, BATCH * cap_c)], axis=1)
    xe = h2[flat]
    ye = _moe_ffn(xe, gate[..., None], w_gate, w_up, w_down, layer)
    n_rows = h2.shape[0]
    return jnp.zeros((n_rows, D_MODEL), F32).at[flat.reshape(-1)].add(ye.reshape(-1, D_MODEL))


def _residual_kernel(x_ref, g_ref, m_ref, ng_ref, o_ref, *, final_norm):
    xn = x_ref[...] + g_ref[...] * m_ref[...]
    if final_norm:
        xn = xn * lax.rsqrt(jnp.mean(xn * xn, axis=-1, keepdims=True) + EPS) * ng_ref[...]
    o_ref[...] = xn


def _residual(x1, gate2, moe, norm_g, final_norm):
    n_rows = x1.shape[0]
    row = lambda i: (i, 0)
    return pl.pallas_call(
        functools.partial(_residual_kernel, final_norm=final_norm),
        grid=(n_rows // TM,),
        in_specs=[pl.BlockSpec((TM, D_MODEL), row),
                  pl.BlockSpec((None, 1, D_MODEL), lambda i: (_group_of_tile(i), 0, 0)),
                  pl.BlockSpec((TM, D_MODEL), row),
                  pl.BlockSpec((1, D_MODEL), lambda i: (0, 0))],
        out_specs=pl.BlockSpec((TM, D_MODEL), row),
        out_shape=jax.ShapeDtypeStruct((n_rows, D_MODEL), F32),
        compiler_params=_cparams(("arbitrary",)),
        name="residual",
    )(x1, gate2, moe, norm_g.reshape(1, D_MODEL))


def _rope(t, cos, sin):
    half = RET_QK_DIM // 2
    t1, t2 = t[:, :half], t[:, half:]
    return jnp.concatenate([t1 * cos - t2 * sin, t1 * sin + t2 * cos], axis=-1)


def _retention_kernel(lg_ref, q_ref, k_ref, v_ref, cos_ref, sin_ref, kc_ref, vc_ref, *rest,
                      backward):
    if backward:
        o_ref, s_ref, dmat_ref, qd_ref, kd_ref = rest
    else:
        ob_ref, g_ref, o_ref, s_ref, dmat_ref, qd_ref, kd_ref = rest
    c = RET_CHUNK
    h = pl.program_id(1)
    lg = lg_ref[1 if backward else 0, h]
    kscale = np.float32(RET_QK_DIM ** -0.5)

    @pl.when(pl.program_id(2) == 0)
    def _():
        ia = lax.broadcasted_iota(jnp.int32, (c, c), 0)
        ib = lax.broadcasted_iota(jnp.int32, (c, c), 1)
        diff = ((ib - ia) if backward else (ia - ib)).astype(F32)
        dmat_ref[...] = jnp.where(diff >= 0, jnp.exp(jnp.maximum(diff, 0.0) * lg), 0.0)
        pos = lax.broadcasted_iota(jnp.int32, (c, RET_QK_DIM), 0).astype(F32)
        if backward:
            qd_ref[...] = jnp.exp((c - pos) * lg)
            kd_ref[...] = jnp.exp(pos * lg) * kscale
        else:
            qd_ref[...] = jnp.exp((pos + 1.0) * lg)
            kd_ref[...] = jnp.exp((c - 1.0 - pos) * lg) * kscale
        cpos = lax.broadcasted_iota(jnp.int32, (CTX_LEN, RET_QK_DIM), 0).astype(F32)
        cw = jnp.exp(cpos * lg) if backward else jnp.exp((CTX_LEN - 1.0 - cpos) * lg)
        kcw = (kc_ref[...].astype(F32) * (cw * kscale)).astype(BF16)
        s_ref[...] = _dot_tn(kcw, vc_ref[...])

    cos = cos_ref[...]
    sin = sin_ref[...]
    q = _rope(q_ref[...].astype(F32), cos, sin)
    k = _rope(k_ref[...].astype(F32), cos, sin)
    v = v_ref[...]
    qb = q.astype(BF16)
    att = _dot_nt(qb, (k * kscale).astype(BF16)) * dmat_ref[...]
    s_old = s_ref[...]
    o = _dot(att.astype(BF16), v) + _dot((q * qd_ref[...]).astype(BF16), s_old.astype(BF16))
    chunk_decay = jnp.exp(jnp.zeros((1, RET_V_DIM), F32) + c * lg)
    s_ref[...] = s_old * chunk_decay + _dot_tn((k * kd_ref[...]).astype(BF16), v)
    if backward:
        o_ref[...] = o.astype(o_ref.dtype)
    else:
        of = o + ob_ref[...].astype(F32)
        y = of * lax.rsqrt(jnp.mean(of * of, axis=-1, keepdims=True) + EPS)
        g = g_ref[...].astype(F32)
        o_ref[...] = (g * _sigmoid(g) * y).astype(o_ref.dtype)


def _retention_pass(log_gamma, z, cos, sin, o_back):
    backward = o_back is None
    c = RET_CHUNK
    n = SEQ // c
    half = RET_QK_DIM // 2
    qk_cols = RET_QK_WIDTH // RET_QK_DIM
    v_base = 2 * RET_QK_WIDTH // RET_V_DIM
    g_base = v_base + RET_V_WIDTH // RET_V_DIM
    chunk = (lambda t: n - 1 - t) if backward else (lambda t: t)
    row = lambda b, t: b * n + chunk(t)
    ctx_row = lambda b: R_LAT // CTX_LEN + b
    in_specs = [
        pl.BlockSpec((c, RET_QK_DIM), lambda b, h, t, lg: (row(b, t), h)),
        pl.BlockSpec((c, RET_QK_DIM), lambda b, h, t, lg: (row(b, t), qk_cols + h)),
        pl.BlockSpec((c, RET_V_DIM), lambda b, h, t, lg: (row(b, t), v_base + h)),
        pl.BlockSpec((c, half), lambda b, h, t, lg: (chunk(t), 0)),
        pl.BlockSpec((c, half), lambda b, h, t, lg: (chunk(t), 0)),
        pl.BlockSpec((CTX_LEN, RET_QK_DIM), lambda b, h, t, lg: (ctx_row(b), qk_cols + h)),
        pl.BlockSpec((CTX_LEN, RET_V_DIM), lambda b, h, t, lg: (ctx_row(b), v_base + h)),
    ]
    args = [z, z, z, cos, sin, z, z]
    if not backward:
        in_specs += [
            pl.BlockSpec((c, RET_V_DIM), lambda b, h, t, lg: (row(b, t), h)),
            pl.BlockSpec((c, RET_V_DIM), lambda b, h, t, lg: (row(b, t), g_base + h)),
        ]
        args += [o_back, z]
    return pl.pallas_call(
        functools.partial(_retention_kernel, backward=backward),
        grid_spec=pltpu.PrefetchScalarGridSpec(
            num_scalar_prefetch=1,
            grid=(BATCH, RET_HEADS, n),
            in_specs=in_specs,
            out_specs=pl.BlockSpec((c, RET_V_DIM), lambda b, h, t, lg: (row(b, t), h)),
            scratch_shapes=[pltpu.VMEM((RET_QK_DIM, RET_V_DIM), F32),
                            pltpu.VMEM((c, c), F32),
                            pltpu.VMEM((c, RET_QK_DIM), F32),
                            pltpu.VMEM((c, RET_QK_DIM), F32)]),
        out_shape=jax.ShapeDtypeStruct((R_LAT, RET_V_WIDTH), BF16),
        compiler_params=_cparams(("arbitrary", "arbitrary", "arbitrary")),
        name="retention_bwd" if backward else "retention_fwd",
    )(log_gamma, *args)


def _rope_tables():
    axis_dim = RET_QK_DIM // 4
    inv = 1.0 / (ROPE_BASE ** (jnp.arange(0, 2 * axis_dim, 2, dtype=F32) / (2 * axis_dim)))
    t = jnp.arange(SEQ)
    r = (t // GRID_W).astype(F32)
    col = (t % GRID_W).astype(F32)
    ang = jnp.concatenate([r[:, None] * inv, col[:, None] * inv], axis=-1)
    return jnp.cos(ang), jnp.sin(ang)


def _split_mod(mod_layer):
    m = mod_layer[:N_GROUPS].reshape(N_GROUPS, 6, 1, D_MODEL)
    return [m[:, k] for k in range(6)]


def _pad_router(w_router):
    return jnp.pad(w_router, ((0, 0), (0, LANES - N_EXPERTS))).astype(BF16)


def kernel(x, c, ctx, c_ctx, ada_w, ada_b, norm1_g, norm2_g, ab_w_in, ab_w_out, na_rpb, sgu_norm_g, sgu_w, sgu_b, ret_w_in, ret_w_out, ret_decay_logit, moe_router, moe_w_gate, moe_w_up, moe_w_down, final_norm_g):
    assert DEPTH == 2 and x.shape == (BATCH, SEQ, D_MODEL) and ctx.shape == (BATCH, CTX_LEN, D_MODEL)

    cvec = jnp.zeros((8, D_MODEL), F32).at[:BATCH].set(c).at[BATCH].set(c_ctx)
    mod = _adaln(cvec, ada_w, ada_b)
    xall = jnp.concatenate([x.reshape(R_LAT, D_MODEL), ctx.reshape(R_CTX, D_MODEL)], axis=0)

    sh1, sc1, g1, sh2, sc2, g2 = _split_mod(mod[0])
    z = _inproj(xall, norm1_g[0], sh1, sc1, ab_w_in[0].astype(BF16), tn=AB_IN)
    a_all = jnp.concatenate([_na_attention(z, _na_bias_tiles(na_rpb[0])), _ctx_attention(z)], axis=0)
    bias_t = jnp.repeat(sgu_b[0].T, SG_GROUP_DIM, axis=1)
    bsg = _sgu(z, sgu_norm_g[0], sgu_w[0].astype(BF16), bias_t)
    w_out = ab_w_out[0].astype(BF16)
    x1, h2, aff = _outproj([a_all, bsg], [w_out[:NA_WIDTH], w_out[NA_WIDTH:]], xall, g1,
                           norm2_g[0], sh2, sc2, _pad_router(moe_router[0]), R_ALL)
    moe = _expert_choice(h2, aff, moe_w_gate, moe_w_up, moe_w_down, 0, with_ctx=True)
    xall = _residual(x1, g2, moe, final_norm_g, final_norm=False)

    sh1, sc1, g1, sh2, sc2, g2 = _split_mod(mod[1])
    z = _inproj(xall, norm1_g[1], sh1, sc1, ret_w_in[0].astype(BF16), tn=2048)
    log_gamma = jax.nn.log_sigmoid(ret_decay_logit[0].astype(F32))
    cos, sin = _rope_tables()
    o_back = _retention_pass(log_gamma, z, cos, sin, None)
    ypre = _retention_pass(log_gamma, z, cos, sin, o_back)
    x1, h2, aff = _outproj([ypre], [ret_w_out[0].astype(BF16)], xall, g1,
                           norm2_g[1], sh2, sc2, _pad_router(moe_router[1]), R_LAT)
    moe = _expert_choice(h2, aff, moe_w_gate, moe_w_up, moe_w_down, 1, with_ctx=False)
    out = _residual(x1, g2, moe, final_norm_g, final_norm=True)
    return out.reshape(BATCH, SEQ, D_MODEL)
```

```python
import functools

import numpy as np
import jax
import jax.numpy as jnp
from jax import lax
from jax.experimental import pallas as pl
from jax.experimental.pallas import tpu as pltpu

F32 = jnp.float32
BF16 = jnp.bfloat16

D_MODEL = 1024
BATCH = 2
SEQ = 8192
DEPTH = 2
GRID_W = 64
CTX_LEN = 256
EPS = 1e-6
NEG_INF = -1e30

NA_HEADS = 8
NA_HEAD_DIM = 64
NA_WIN_H = 8
NA_WIN_W = 16
NA_WIDTH = NA_HEADS * NA_HEAD_DIM
SG_GROUPS = 8
SG_GROUP_DIM = 64
SG_CHUNK = 128
SG_WIDTH = SG_GROUPS * SG_GROUP_DIM
AB_IN = 3 * NA_WIDTH + 2 * SG_WIDTH

RET_HEADS = 4
RET_QK_DIM = 256
RET_V_DIM = 512
RET_QK_WIDTH = RET_HEADS * RET_QK_DIM
RET_V_WIDTH = RET_HEADS * RET_V_DIM
RET_IN = 2 * RET_QK_WIDTH + 2 * RET_V_WIDTH
ROPE_BASE = 10000.0

N_EXPERTS = 16
EC_CAPACITY = 2
D_FF_EXPERT = 2816

ROWS = SEQ // GRID_W
R_LAT = BATCH * SEQ
R_CTX = BATCH * CTX_LEN
R_ALL = R_LAT + R_CTX
N_GROUPS = BATCH + 1

LANES = 128
TM = 512
VMEM_LIMIT = 56 * 1024 * 1024

NA_QROWS = 8
NA_KROWS = 16
NA_QT = NA_QROWS * GRID_W
NA_KT = 256
NA_NKB = NA_KROWS * GRID_W // NA_KT
RET_CHUNK = 256
MOE_TF = 256
MOE_ROW_SPLIT = 4
MOE_DMA_UNROLL = 8
MOE_UPDATE_ROWS = 256


def _group_of_tile(i):
    return jnp.minimum(i // (SEQ // TM), BATCH)


def _cparams(sem, vmem=VMEM_LIMIT):
    return pltpu.CompilerParams(dimension_semantics=sem, vmem_limit_bytes=vmem)


def _dot(a, b):
    return jnp.dot(a, b, preferred_element_type=F32)


def _dot_nt(a, b):
    return lax.dot_general(a, b, (((1,), (1,)), ((), ())), preferred_element_type=F32)


def _dot_tn(a, b):
    return lax.dot_general(a, b, (((0,), (0,)), ((), ())), preferred_element_type=F32)


def _sigmoid(x):
    return 1.0 / (1.0 + jnp.exp(-x))


def _gelu_tanh(x):
    c = np.float32(np.sqrt(2.0 / np.pi))
    return 0.5 * x * (1.0 + jnp.tanh(c * (x + np.float32(0.044715) * (x * x * x))))


def _adaln_kernel(c_ref, w_ref, b_ref, o_ref):
    cv = c_ref[...]
    s = (cv * _sigmoid(cv)).astype(BF16)
    o_ref[...] = _dot(s, w_ref[...].astype(BF16)) + b_ref[...]


def _adaln(cvec, ada_w, ada_b):
    tn = 1024
    n = 6 * D_MODEL
    return pl.pallas_call(
        _adaln_kernel,
        grid=(DEPTH, n // tn),
        in_specs=[
            pl.BlockSpec((8, D_MODEL), lambda l, j: (0, 0)),
            pl.BlockSpec((None, D_MODEL, tn), lambda l, j: (l, 0, j)),
            pl.BlockSpec((None, 1, tn), lambda l, j: (l, 0, j)),
        ],
        out_specs=pl.BlockSpec((None, 8, tn), lambda l, j: (l, 0, j)),
        out_shape=jax.ShapeDtypeStruct((DEPTH, 8, n), F32),
        compiler_params=_cparams(("arbitrary", "arbitrary")),
        name="adaln",
    )(cvec, ada_w, ada_b.reshape(DEPTH, 1, n))


def _read_rows(x_ref):
    if len(x_ref.shape) == 2:
        return x_ref[...]
    return jnp.concatenate([x_ref[:, s, :] for s in range(x_ref.shape[1])], axis=-1)


def _write_token_tiled(o_ref, val):
    for s in range(o_ref.shape[1]):
        o_ref[:, s, :] = val[:, s * LANES:(s + 1) * LANES]


def _row_spec(x, index_map):
    if x.ndim == 2:
        return pl.BlockSpec((TM, D_MODEL), lambda *a: (index_map(*a), 0))
    return pl.BlockSpec((TM, D_MODEL // LANES, LANES), lambda *a: (index_map(*a), 0, 0))


def _inproj_kernel(x_ref, g_ref, sh_ref, sc_ref, w_ref, z_ref, hb_ref):
    @pl.when(pl.program_id(1) == 0)
    def _():
        x = _read_rows(x_ref)
        y = x * lax.rsqrt(jnp.mean(x * x, axis=-1, keepdims=True) + EPS) * g_ref[...]
        hb_ref[...] = (y * (1.0 + sc_ref[...]) + sh_ref[...]).astype(BF16)

    z_ref[...] = _dot(hb_ref[...], w_ref[...]).astype(z_ref.dtype)


def _inproj(xall, norm_g, shift, scale, w_bf16, tn):
    n = w_bf16.shape[1]
    grp = lambda i, j: (_group_of_tile(i), 0, 0)
    return pl.pallas_call(
        _inproj_kernel,
        grid=(R_ALL // TM, n // tn),
        in_specs=[
            _row_spec(xall, lambda i, j: i),
            pl.BlockSpec((1, D_MODEL), lambda i, j: (0, 0)),
            pl.BlockSpec((None, 1, D_MODEL), grp),
            pl.BlockSpec((None, 1, D_MODEL), grp),
            pl.BlockSpec((D_MODEL, tn), lambda i, j: (0, j)),
        ],
        out_specs=pl.BlockSpec((TM, tn), lambda i, j: (i, j)),
        out_shape=jax.ShapeDtypeStruct((R_ALL, n), BF16),
        scratch_shapes=[pltpu.VMEM((TM, D_MODEL), BF16)],
        compiler_params=_cparams(("arbitrary", "arbitrary")),
        name="inproj",
    )(xall, norm_g.reshape(1, D_MODEL), shift, scale, w_bf16)


def _na_tile_codes():
    masked = 2 * NA_WIN_H - 1
    kinds = (0, 1, ROWS // NA_QROWS - 1)
    d = np.full((3, NA_QROWS, NA_KROWS), masked, np.int64)
    for v, j in enumerate(kinds):
        kb = int(np.clip(NA_QROWS * j - NA_WIN_H // 2, 0, ROWS - NA_KROWS))
        for a in range(NA_QROWS):
            i = NA_QROWS * j + a
            r0 = int(np.clip(i - NA_WIN_H // 2, 0, ROWS - NA_WIN_H))
            for r in range(NA_KROWS):
                if r0 <= kb + r < r0 + NA_WIN_H:
                    d[v, a, r] = kb + r - i + NA_WIN_H - 1
    pairs = d.reshape(-1, 2)
    uniq = sorted(set(map(tuple, pairs)))
    code = np.array([uniq.index(tuple(p)) for p in pairs], np.int32)
    return code, np.array(uniq, np.int64)


_NA_CODES, _NA_CODE_ROWS = _na_tile_codes()


def _na_bias_tiles(rpb):
    qc = np.arange(GRID_W)[:, None]
    kc = np.arange(GRID_W)[None, :]
    cstart = np.clip(qc - NA_WIN_W // 2, 0, GRID_W - NA_WIN_W)
    col_ok = (kc >= cstart) & (kc < cstart + NA_WIN_W)
    dcol = np.clip(kc - qc, 1 - NA_WIN_W, NA_WIN_W - 1) + NA_WIN_W - 1
    onehot = (dcol.reshape(1, -1) == np.arange(2 * NA_WIN_W - 1)[:, None]).astype(np.float32)
    toe = jnp.einsum('hdm,mq->hdq', rpb.astype(F32), jnp.asarray(onehot),
                     precision=lax.Precision.HIGHEST)
    toe = toe.reshape(NA_HEADS, 2 * NA_WIN_H - 1, GRID_W, GRID_W)
    toe = jnp.where(jnp.asarray(col_ok), toe, NEG_INF)
    toe = jnp.concatenate([toe, jnp.full((NA_HEADS, 1, GRID_W, GRID_W), NEG_INF, F32)], axis=1)
    return jnp.concatenate([toe[:, _NA_CODE_ROWS[:, 0]], toe[:, _NA_CODE_ROWS[:, 1]]], axis=-1)


def _pair_attention(q, keys, vals, add_bias=None):
    lane = lax.broadcasted_iota(jnp.int32, (1, LANES), 1)
    scale = np.float32(NA_HEAD_DIM ** -0.5)
    out = jnp.zeros((q.shape[0], LANES), F32)
    for hh in range(2):
        in_head = (lane < NA_HEAD_DIM) if hh == 0 else (lane >= NA_HEAD_DIM)
        qm = jnp.where(in_head, q, jnp.zeros_like(q))
        s = []
        for t, k in enumerate(keys):
            st = _dot_nt(qm, k) * scale
            if add_bias is not None:
                st = add_bias(hh, t, st)
            s.append(st)
        m = s[0].max(axis=-1, keepdims=True)
        for st in s[1:]:
            m = jnp.maximum(m, st.max(axis=-1, keepdims=True))
        den = jnp.zeros_like(m)
        acc = jnp.zeros((q.shape[0], LANES), F32)
        for st, v in zip(s, vals):
            p = jnp.exp(st - m)
            den = den + p.sum(axis=-1, keepdims=True)
            acc = acc + _dot(p.astype(BF16), v)
        out = jnp.where(in_head, acc / den, out)
    return out


def _na_kernel(code_ref, q_ref, k0, k1, k2, k3, v0, v1, v2, v3, kc_ref, vc_ref, tile_ref, o_ref):
    keys = [k0[...], k1[...], k2[...], k3[...], kc_ref[...]]
    vals = [v0[...], v1[...], v2[...], v3[...], vc_ref[...]]
    nj = pl.num_programs(1)
    j = pl.program_id(1)
    kind = jnp.minimum(j, 1) + jnp.maximum(j - (nj - 2), 0)
    tiles_per_row = NA_KROWS // 2
    code_base = kind * (NA_QROWS * tiles_per_row)

    def add_bias(hh, t, st):
        if t >= NA_NKB:
            return st
        rows = []
        for a in range(NA_QROWS):
            cols = []
            for u in range(NA_KT // LANES):
                code = code_ref[code_base + a * tiles_per_row + t * (NA_KT // LANES) + u]
                piece = st[a * GRID_W:(a + 1) * GRID_W, u * LANES:(u + 1) * LANES]
                cols.append(piece + tile_ref[hh, code])
            rows.append(jnp.concatenate(cols, axis=1))
        return jnp.concatenate(rows, axis=0)

    o_ref[...] = _pair_attention(q_ref[...], keys, vals, add_bias).astype(o_ref.dtype)


def _na_attention(z, tiles):
    nj = ROWS // NA_QROWS
    kblocks = SEQ // NA_KT
    qcol, kcol, vcol = 0, NA_WIDTH // LANES, 2 * NA_WIDTH // LANES
    n_codes = tiles.shape[1]

    def kbase(j):
        return jnp.clip(2 * j - 1, 0, kblocks - NA_NKB)

    def kspec(t, col):
        return pl.BlockSpec((NA_KT, LANES),
                            lambda p, j, b, code: (b * kblocks + kbase(j) + t, col + p))

    ctx_row = lambda b: R_LAT // CTX_LEN + b
    in_specs = (
        [pl.BlockSpec((NA_QT, LANES), lambda p, j, b, code: (b * nj + j, qcol + p))]
        + [kspec(t, kcol) for t in range(NA_NKB)]
        + [kspec(t, vcol) for t in range(NA_NKB)]
        + [pl.BlockSpec((CTX_LEN, LANES), lambda p, j, b, code: (ctx_row(b), kcol + p)),
           pl.BlockSpec((CTX_LEN, LANES), lambda p, j, b, code: (ctx_row(b), vcol + p)),
           pl.BlockSpec((2, n_codes, GRID_W, LANES), lambda p, j, b, code: (p, 0, 0, 0))])
    return pl.pallas_call(
        _na_kernel,
        grid_spec=pltpu.PrefetchScalarGridSpec(
            num_scalar_prefetch=1,
            grid=(NA_HEADS // 2, nj, BATCH),
            in_specs=in_specs,
            out_specs=pl.BlockSpec((NA_QT, LANES), lambda p, j, b, code: (b * nj + j, p))),
        out_shape=jax.ShapeDtypeStruct((R_LAT, NA_WIDTH), BF16),
        compiler_params=_cparams(("arbitrary", "arbitrary", "arbitrary")),
        name="na_attention",
    )(jnp.asarray(_NA_CODES), *([z] * (3 + 2 * NA_NKB)), tiles)


def _ctx_attn_kernel(q_ref, k_ref, v_ref, o_ref):
    o_ref[...] = _pair_attention(q_ref[...], [k_ref[...]], [v_ref[...]]).astype(o_ref.dtype)


def _ctx_attention(z):
    qcol, kcol, vcol = 0, NA_WIDTH // LANES, 2 * NA_WIDTH // LANES
    row = lambda b: R_LAT // CTX_LEN + b
    return pl.pallas_call(
        _ctx_attn_kernel,
        grid=(BATCH, NA_HEADS // 2),
        in_specs=[pl.BlockSpec((CTX_LEN, LANES), lambda b, p: (row(b), qcol + p)),
                  pl.BlockSpec((CTX_LEN, LANES), lambda b, p: (row(b), kcol + p)),
                  pl.BlockSpec((CTX_LEN, LANES), lambda b, p: (row(b), vcol + p))],
        out_specs=pl.BlockSpec((CTX_LEN, LANES), lambda b, p: (b, p)),
        out_shape=jax.ShapeDtypeStruct((R_CTX, NA_WIDTH), BF16),
        compiler_params=_cparams(("arbitrary", "arbitrary")),
        name="ctx_attention",
    )(z, z, z)


def _sgu_kernel(u_ref, v_ref, ng_ref, w_ref, bt_ref, o_ref):
    lane = lax.broadcasted_iota(jnp.int32, (1, LANES), 1)
    first = lane < SG_GROUP_DIM
    for ch in range(TM // SG_CHUNK):
        rows = slice(ch * SG_CHUNK, (ch + 1) * SG_CHUNK)
        u = _gelu_tanh(u_ref[rows, :].astype(F32))
        v = _gelu_tanh(v_ref[rows, :].astype(F32))
        vn = v * lax.rsqrt(jnp.mean(v * v, axis=-1, keepdims=True) + EPS) * ng_ref[...]
        vnb = vn.astype(BF16)
        for pr in range(SG_GROUPS // 2):
            cols = slice(pr * LANES, (pr + 1) * LANES)
            slab = vnb[:, cols]
            mixed = jnp.where(first, _dot(w_ref[2 * pr], slab), _dot(w_ref[2 * pr + 1], slab))
            o_ref[rows, cols] = (u[:, cols] * (mixed + bt_ref[:, cols])).astype(o_ref.dtype)


def _sgu(z, norm_g, w_bf16, bias_t):
    ucol = 3 * NA_WIDTH // SG_WIDTH
    return pl.pallas_call(
        _sgu_kernel,
        grid=(R_ALL // TM,),
        in_specs=[pl.BlockSpec((TM, SG_WIDTH), lambda i: (i, ucol)),
                  pl.BlockSpec((TM, SG_WIDTH), lambda i: (i, ucol + 1)),
                  pl.BlockSpec((1, SG_WIDTH), lambda i: (0, 0)),
                  pl.BlockSpec((SG_GROUPS, SG_CHUNK, SG_CHUNK), lambda i: (0, 0, 0)),
                  pl.BlockSpec((SG_CHUNK, SG_WIDTH), lambda i: (0, 0))],
        out_specs=pl.BlockSpec((TM, SG_WIDTH), lambda i: (i, 0)),
        out_shape=jax.ShapeDtypeStruct((R_ALL, SG_WIDTH), BF16),
        compiler_params=_cparams(("arbitrary",)),
        name="sgu",
    )(z, z, norm_g.reshape(1, SG_WIDTH), w_bf16, bias_t)


def _outproj_kernel(*refs, n_lhs):
    lhs = refs[:n_lhs]
    ws = refs[n_lhs:2 * n_lhs]
    x_ref, g1_ref, ng_ref, sh_ref, sc_ref, wr_ref, xo_ref, h2_ref, aff_ref = refs[2 * n_lhs:]
    y = _dot(lhs[0][...], ws[0][...])
    for a, w in zip(lhs[1:], ws[1:]):
        y = y + _dot(a[...], w[...])
    xn = _read_rows(x_ref) + g1_ref[...] * y
    _write_token_tiled(xo_ref, xn)
    hn = xn * lax.rsqrt(jnp.mean(xn * xn, axis=-1, keepdims=True) + EPS) * ng_ref[...]
    h2 = hn * (1.0 + sc_ref[...]) + sh_ref[...]
    _write_token_tiled(h2_ref, h2)
    logits = _dot(h2.astype(BF16), wr_ref[...])
    lane = lax.broadcasted_iota(jnp.int32, (1, LANES), 1)
    logits = jnp.where(lane < N_EXPERTS, logits, NEG_INF)
    e = jnp.exp(logits - logits.max(axis=-1, keepdims=True))
    aff_ref[...] = e / e.sum(axis=-1, keepdims=True)


def _outproj(lhs_list, w_list, xall, gate1, norm_g, shift, scale, w_router_pad, n_rows):
    n_lhs = len(lhs_list)
    grp = lambda i: (_group_of_tile(i), 0, 0)
    row = lambda i: (i, 0)
    const2 = lambda i: (0, 0)
    in_specs = (
        [pl.BlockSpec((TM, a.shape[1]), row) for a in lhs_list]
        + [pl.BlockSpec(w.shape, const2) for w in w_list]
        + [_row_spec(xall, lambda i: i),
           pl.BlockSpec((None, 1, D_MODEL), grp),
           pl.BlockSpec((1, D_MODEL), const2),
           pl.BlockSpec((None, 1, D_MODEL), grp),
           pl.BlockSpec((None, 1, D_MODEL), grp),
           pl.BlockSpec((D_MODEL, LANES), const2)])
    tiled = (n_rows, D_MODEL // LANES, LANES)
    tiled_spec = pl.BlockSpec((TM, D_MODEL // LANES, LANES), lambda i: (i, 0, 0))
    return pl.pallas_call(
        functools.partial(_outproj_kernel, n_lhs=n_lhs),
        grid=(n_rows // TM,),
        in_specs=in_specs,
        out_specs=[tiled_spec, tiled_spec, pl.BlockSpec((TM, LANES), row)],
        out_shape=[jax.ShapeDtypeStruct(tiled, F32),
                   jax.ShapeDtypeStruct(tiled, F32),
                   jax.ShapeDtypeStruct((n_rows, LANES), F32)],
        compiler_params=_cparams(("arbitrary",)),
        name="outproj",
    )(*lhs_list, *w_list, xall, gate1, norm_g.reshape(1, D_MODEL), shift, scale, w_router_pad)


def _moe_segments(rows):
    cap = EC_CAPACITY * SEQ // N_EXPERTS
    segs = [(b * cap, (b + 1) * cap, b) for b in range(BATCH)]
    if rows > BATCH * cap:
        segs.append((BATCH * cap, rows, BATCH))
    return segs


def _moe_kernel(idx_ref, h_hbm, x_hbm, gate_ref, g2_ref, wg_ref, wu_ref, wd_ref, xo_hbm,
                xg, xb, acc, res, wg_s, wu_s, wd_s, sem_h, sem_r, sem_s, *, rows):
    del x_hbm
    e, f = pl.program_id(0), pl.program_id(1)
    ne, nf = pl.num_programs(0), pl.num_programs(1)
    nsub = D_MODEL // LANES

    def for_slots(expert, body):
        base = expert * rows

        def group(o, carry):
            for u in range(MOE_DMA_UNROLL):
                i = o * MOE_DMA_UNROLL + u
                body(i, idx_ref[base + i])
            return carry

        lax.fori_loop(0, rows // MOE_DMA_UNROLL, group, 0)

    def token_copy(i, n):
        return pltpu.make_async_copy(h_hbm.at[n], xg.at[i], sem_h)

    def fetch_copy(i, n):
        return pltpu.make_async_copy(xo_hbm.at[n], res.at[i], sem_r)

    def writeback_copy(i, n):
        return pltpu.make_async_copy(res.at[i], xo_hbm.at[n], sem_s)

    all_tokens = pltpu.make_async_copy(h_hbm.at[pl.ds(0, rows)], xg, sem_h)
    all_fetched = pltpu.make_async_copy(xo_hbm.at[pl.ds(0, rows)], res, sem_r)
    all_written = pltpu.make_async_copy(res, xo_hbm.at[pl.ds(0, rows)], sem_s)

    @pl.when(f == 0)
    def _():
        @pl.when(e == 0)
        def _():
            for_slots(0, lambda i, n: token_copy(i, n).start())

        all_tokens.wait()
        for s in range(nsub):
            xb[:, s * LANES:(s + 1) * LANES] = xg[:, s, :].astype(BF16)

    @pl.when((f == 1) & (e + 1 < ne))
    def _():
        for_slots(e + 1, lambda i, n: token_copy(i, n).start())

    @pl.when(f == 2)
    def _():
        @pl.when(e > 0)
        def _():
            all_written.wait()

        for_slots(e, lambda i, n: fetch_copy(i, n).start())

    wg_s[...] = wg_ref[...].astype(BF16)
    wu_s[...] = wu_ref[...].astype(BF16)
    wd_s[...] = wd_ref[...].astype(BF16)
    rc = rows // MOE_ROW_SPLIT
    for r in range(MOE_ROW_SPLIT):
        sl = slice(r * rc, (r + 1) * rc)
        x = xb[sl, :]
        a = _dot(x, wg_s[...])
        b = _dot(x, wu_s[...])
        hid = (a * _sigmoid(a) * b).astype(BF16)
        part = _dot(hid, wd_s[...])

        @pl.when(f == 0)
        def _():
            acc[sl, :] = part

        @pl.when(f > 0)
        def _():
            acc[sl, :] += part

    @pl.when(f == nf - 1)
    def _():
        all_fetched.wait()
        for lo, hi, g in _moe_segments(rows):
            for r0 in range(lo, hi, MOE_UPDATE_ROWS):
                sl = slice(r0, min(r0 + MOE_UPDATE_ROWS, hi))
                upd = acc[sl, :] * gate_ref[sl, :] * g2_ref[g]
                for s in range(nsub):
                    res[sl, s, :] = res[sl, s, :] + upd[:, s * LANES:(s + 1) * LANES]
        for_slots(e, lambda i, n: writeback_copy(i, n).start())

        @pl.when(e == ne - 1)
        def _():
            all_written.wait()


def _moe_ffn(flat_idx, h2, x1, gates, gate2, w_gate, w_up, w_down, layer):
    rows = flat_idx.shape[1]
    assert rows % MOE_DMA_UNROLL == 0 and rows % MOE_ROW_SPLIT == 0
    nf = D_FF_EXPERT // MOE_TF
    assert nf > 3
    nsub = D_MODEL // LANES
    return pl.pallas_call(
        functools.partial(_moe_kernel, rows=rows),
        grid_spec=pltpu.PrefetchScalarGridSpec(
            num_scalar_prefetch=1,
            grid=(N_EXPERTS, nf),
            in_specs=[pl.BlockSpec(memory_space=pl.ANY),
                      pl.BlockSpec(memory_space=pl.ANY),
                      pl.BlockSpec((None, rows, 1), lambda e, f, idx: (e, 0, 0)),
                      pl.BlockSpec((N_GROUPS, 1, D_MODEL), lambda e, f, idx: (0, 0, 0)),
                      pl.BlockSpec((None, None, D_MODEL, MOE_TF), lambda e, f, idx: (layer, e, 0, f)),
                      pl.BlockSpec((None, None, D_MODEL, MOE_TF), lambda e, f, idx: (layer, e, 0, f)),
                      pl.BlockSpec((None, None, MOE_TF, D_MODEL), lambda e, f, idx: (layer, e, f, 0))],
            out_specs=pl.BlockSpec(memory_space=pl.ANY),
            scratch_shapes=[pltpu.VMEM((rows, nsub, LANES), F32),
                            pltpu.VMEM((rows, D_MODEL), BF16),
                            pltpu.VMEM((rows, D_MODEL), F32),
                            pltpu.VMEM((rows, nsub, LANES), F32),
                            pltpu.VMEM((D_MODEL, MOE_TF), BF16),
                            pltpu.VMEM((D_MODEL, MOE_TF), BF16),
                            pltpu.VMEM((MOE_TF, D_MODEL), BF16),
                            pltpu.SemaphoreType.DMA,
                            pltpu.SemaphoreType.DMA,
                            pltpu.SemaphoreType.DMA]),
        out_shape=jax.ShapeDtypeStruct(x1.shape, F32),
        input_output_aliases={2: 0},
        compiler_params=_cparams(("arbitrary", "arbitrary")),
        name="moe_ffn",
    )(flat_idx.reshape(-1), h2, x1, gates, gate2, w_gate, w_up, w_down)


def _expert_choice(h2, x1, aff, gate2, w_gate, w_up, w_down, layer, with_ctx):
    cap = EC_CAPACITY * SEQ // N_EXPERTS
    aff_lat = aff[:R_LAT, :N_EXPERTS].reshape(BATCH, SEQ, N_EXPERTS)
    gate, idx = lax.top_k(jnp.swapaxes(aff_lat, 1, 2), cap)
    flat = idx + (jnp.arange(BATCH, dtype=idx.dtype) * SEQ)[:, None, None]
    flat = jnp.swapaxes(flat, 0, 1).reshape(N_EXPERTS, BATCH * cap)
    gate = jnp.swapaxes(gate, 0, 1).reshape(N_EXPERTS, BATCH * cap)
    if with_ctx:
        cap_c = EC_CAPACITY * CTX_LEN // N_EXPERTS
        aff_c = aff[R_LAT:, :N_EXPERTS].reshape(BATCH, CTX_LEN, N_EXPERTS)
        gate_c, idx_c = lax.top_k(jnp.swapaxes(aff_c, 1, 2), cap_c)
        flat_c = idx_c + R_LAT + (jnp.arange(BATCH, dtype=idx_c.dtype) * CTX_LEN)[:, None, None]
        flat = jnp.concatenate(
            [flat, jnp.swapaxes(flat_c, 0, 1).reshape(N_EXPERTS, BATCH * cap_c)], axis=1)
        gate = jnp.concatenate(
            [gate, jnp.swapaxes(gate_c, 0, 1).reshape(N_EXPERTS, BATCH * cap_c)], axis=1)
    return _moe_ffn(flat.astype(jnp.int32), h2, x1, gate[..., None], gate2,
                    w_gate, w_up, w_down, layer)


def _final_norm_kernel(x_ref, ng_ref, o_ref):
    x = _read_rows(x_ref)
    o_ref[...] = x * lax.rsqrt(jnp.mean(x * x, axis=-1, keepdims=True) + EPS) * ng_ref[...]


def _final_norm(x, norm_g):
    n_rows = x.shape[0]
    return pl.pallas_call(
        _final_norm_kernel,
        grid=(n_rows // TM,),
        in_specs=[_row_spec(x, lambda i: i),
                  pl.BlockSpec((1, D_MODEL), lambda i: (0, 0))],
        out_specs=pl.BlockSpec((TM, D_MODEL), lambda i: (i, 0)),
        out_shape=jax.ShapeDtypeStruct((n_rows, D_MODEL), F32),
        compiler_params=_cparams(("arbitrary",)),
        name="final_norm",
    )(x, norm_g.reshape(1, D_MODEL))


def _rope(t, cos, sin):
    half = RET_QK_DIM // 2
    t1, t2 = t[:, :half], t[:, half:]
    return jnp.concatenate([t1 * cos - t2 * sin, t1 * sin + t2 * cos], axis=-1)


def _retention_kernel(lg_ref, q_ref, k_ref, v_ref, cos_ref, sin_ref, kc_ref, vc_ref, *rest,
                      backward):
    if backward:
        o_ref, s_ref, dmat_ref, qd_ref, kd_ref = rest
    else:
        ob_ref, g_ref, o_ref, s_ref, dmat_ref, qd_ref, kd_ref = rest
    c = RET_CHUNK
    h = pl.program_id(1)
    lg = lg_ref[1 if backward else 0, h]
    kscale = np.float32(RET_QK_DIM ** -0.5)

    @pl.when(pl.program_id(2) == 0)
    def _():
        ia = lax.broadcasted_iota(jnp.int32, (c, c), 0)
        ib = lax.broadcasted_iota(jnp.int32, (c, c), 1)
        diff = ((ib - ia) if backward else (ia - ib)).astype(F32)
        dmat_ref[...] = jnp.where(diff >= 0, jnp.exp(jnp.maximum(diff, 0.0) * lg), 0.0)
        pos = lax.broadcasted_iota(jnp.int32, (c, RET_QK_DIM), 0).astype(F32)
        if backward:
            qd_ref[...] = jnp.exp((c - pos) * lg)
            kd_ref[...] = jnp.exp(pos * lg) * kscale
        else:
            qd_ref[...] = jnp.exp((pos + 1.0) * lg)
            kd_ref[...] = jnp.exp((c - 1.0 - pos) * lg) * kscale
        cpos = lax.broadcasted_iota(jnp.int32, (CTX_LEN, RET_QK_DIM), 0).astype(F32)
        cw = jnp.exp(cpos * lg) if backward else jnp.exp((CTX_LEN - 1.0 - cpos) * lg)
        kcw = (kc_ref[...].astype(F32) * (cw * kscale)).astype(BF16)
        s_ref[...] = _dot_tn(kcw, vc_ref[...])

    cos = cos_ref[...]
    sin = sin_ref[...]
    q = _rope(q_ref[...].astype(F32), cos, sin)
    k = _rope(k_ref[...].astype(F32), cos, sin)
    v = v_ref[...]
    qb = q.astype(BF16)
    att = _dot_nt(qb, (k * kscale).astype(BF16)) * dmat_ref[...]
    s_old = s_ref[...]
    o = _dot(att.astype(BF16), v) + _dot((q * qd_ref[...]).astype(BF16), s_old.astype(BF16))
    chunk_decay = jnp.exp(jnp.zeros((1, RET_V_DIM), F32) + c * lg)
    s_ref[...] = s_old * chunk_decay + _dot_tn((k * kd_ref[...]).astype(BF16), v)
    if backward:
        o_ref[...] = o.astype(o_ref.dtype)
    else:
        of = o + ob_ref[...].astype(F32)
        y = of * lax.rsqrt(jnp.mean(of * of, axis=-1, keepdims=True) + EPS)
        g = g_ref[...].astype(F32)
        o_ref[...] = (g * _sigmoid(g) * y).astype(o_ref.dtype)


def _retention_pass(log_gamma, z, cos, sin, o_back):
    backward = o_back is None
    c = RET_CHUNK
    n = SEQ // c
    half = RET_QK_DIM // 2
    qk_cols = RET_QK_WIDTH // RET_QK_DIM
    v_base = 2 * RET_QK_WIDTH // RET_V_DIM
    g_base = v_base + RET_V_WIDTH // RET_V_DIM
    chunk = (lambda t: n - 1 - t) if backward else (lambda t: t)
    row = lambda b, t: b * n + chunk(t)
    ctx_row = lambda b: R_LAT // CTX_LEN + b
    in_specs = [
        pl.BlockSpec((c, RET_QK_DIM), lambda b, h, t, lg: (row(b, t), h)),
        pl.BlockSpec((c, RET_QK_DIM), lambda b, h, t, lg: (row(b, t), qk_cols + h)),
        pl.BlockSpec((c, RET_V_DIM), lambda b, h, t, lg: (row(b, t), v_base + h)),
        pl.BlockSpec((c, half), lambda b, h, t, lg: (chunk(t), 0)),
        pl.BlockSpec((c, half), lambda b, h, t, lg: (chunk(t), 0)),
        pl.BlockSpec((CTX_LEN, RET_QK_DIM), lambda b, h, t, lg: (ctx_row(b), qk_cols + h)),
        pl.BlockSpec((CTX_LEN, RET_V_DIM), lambda b, h, t, lg: (ctx_row(b), v_base + h)),
    ]
    args = [z, z, z, cos, sin, z, z]
    if not backward:
        in_specs += [
            pl.BlockSpec((c, RET_V_DIM), lambda b, h, t, lg: (row(b, t), h)),
            pl.BlockSpec((c, RET_V_DIM), lambda b, h, t, lg: (row(b, t), g_base + h)),
        ]
        args += [o_back, z]
    return pl.pallas_call(
        functools.partial(_retention_kernel, backward=backward),
        grid_spec=pltpu.PrefetchScalarGridSpec(
            num_scalar_prefetch=1,
            grid=(BATCH, RET_HEADS, n),
            in_specs=in_specs,
            out_specs=pl.BlockSpec((c, RET_V_DIM), lambda b, h, t, lg: (row(b, t), h)),
            scratch_shapes=[pltpu.VMEM((RET_QK_DIM, RET_V_DIM), F32),
                            pltpu.VMEM((c, c), F32),
                            pltpu.VMEM((c, RET_QK_DIM), F32),
                            pltpu.VMEM((c, RET_QK_DIM), F32)]),
        out_shape=jax.ShapeDtypeStruct((R_LAT, RET_V_WIDTH), BF16),
        compiler_params=_cparams(("arbitrary", "arbitrary", "arbitrary")),
        name="retention_bwd" if backward else "retention_fwd",
    )(log_gamma, *args)


def _rope_tables():
    axis_dim = RET_QK_DIM // 4
    inv = 1.0 / (ROPE_BASE ** (jnp.arange(0, 2 * axis_dim, 2, dtype=F32) / (2 * axis_dim)))
    t = jnp.arange(SEQ)
    r = (t // GRID_W).astype(F32)
    col = (t % GRID_W).astype(F32)
    ang = jnp.concatenate([r[:, None] * inv, col[:, None] * inv], axis=-1)
    return jnp.cos(ang), jnp.sin(ang)


def _split_mod(mod_layer):
    m = mod_layer[:N_GROUPS].reshape(N_GROUPS, 6, 1, D_MODEL)
    return [m[:, k] for k in range(6)]


def _pad_router(w_router):
    return jnp.pad(w_router, ((0, 0), (0, LANES - N_EXPERTS))).astype(BF16)


def kernel(x, c, ctx, c_ctx, ada_w, ada_b, norm1_g, norm2_g, ab_w_in, ab_w_out, na_rpb, sgu_norm_g, sgu_w, sgu_b, ret_w_in, ret_w_out, ret_decay_logit, moe_router, moe_w_gate, moe_w_up, moe_w_down, final_norm_g):
    assert DEPTH == 2 and x.shape == (BATCH, SEQ, D_MODEL) and ctx.shape == (BATCH, CTX_LEN, D_MODEL)

    cvec = jnp.zeros((8, D_MODEL), F32).at[:BATCH].set(c).at[BATCH].set(c_ctx)
    mod = _adaln(cvec, ada_w, ada_b)
    xall = jnp.concatenate([x.reshape(R_LAT, D_MODEL), ctx.reshape(R_CTX, D_MODEL)], axis=0)

    sh1, sc1, g1, sh2, sc2, g2 = _split_mod(mod[0])
    z = _inproj(xall, norm1_g[0], sh1, sc1, ab_w_in[0].astype(BF16), tn=AB_IN)
    a_all = jnp.concatenate([_na_attention(z, _na_bias_tiles(na_rpb[0])), _ctx_attention(z)], axis=0)
    bias_t = jnp.repeat(sgu_b[0].T, SG_GROUP_DIM, axis=1)
    bsg = _sgu(z, sgu_norm_g[0], sgu_w[0].astype(BF16), bias_t)
    w_out = ab_w_out[0].astype(BF16)
    x1, h2, aff = _outproj([a_all, bsg], [w_out[:NA_WIDTH], w_out[NA_WIDTH:]], xall, g1,
                           norm2_g[0], sh2, sc2, _pad_router(moe_router[0]), R_ALL)
    xall = _expert_choice(h2, x1, aff, g2, moe_w_gate, moe_w_up, moe_w_down, 0, with_ctx=True)

    sh1, sc1, g1, sh2, sc2, g2 = _split_mod(mod[1])
    z = _inproj(xall, norm1_g[1], sh1, sc1, ret_w_in[0].astype(BF16), tn=2048)
    log_gamma = jax.nn.log_sigmoid(ret_decay_logit[0].astype(F32))
    cos, sin = _rope_tables()
    o_back = _retention_pass(log_gamma, z, cos, sin, None)
    ypre = _retention_pass(log_gamma, z, cos, sin, o_back)
    x1, h2, aff = _outproj([ypre], [ret_w_out[0].astype(BF16)], xall, g1,
                           norm2_g[1], sh2, sc2, _pad_router(moe_router[1]), R_LAT)
    x2 = _expert_choice(h2, x1, aff, g2, moe_w_gate, moe_w_up, moe_w_down, 1, with_ctx=False)
    return _final_norm(x2, final_norm_g).reshape(BATCH, SEQ, D_MODEL)
```

```python
import functools

import numpy as np
import jax
import jax.numpy as jnp
from jax import lax
from jax.experimental import pallas as pl
from jax.experimental.pallas import tpu as pltpu

F32 = jnp.float32
BF16 = jnp.bfloat16

D_MODEL = 1024
BATCH = 2
SEQ = 8192
DEPTH = 2
GRID_W = 64
CTX_LEN = 256
EPS = 1e-6
NEG_INF = -1e30

NA_HEADS = 8
NA_HEAD_DIM = 64
NA_WIN_H = 8
NA_WIN_W = 16
NA_WIDTH = NA_HEADS * NA_HEAD_DIM
SG_GROUPS = 8
SG_GROUP_DIM = 64
SG_CHUNK = 128
SG_WIDTH = SG_GROUPS * SG_GROUP_DIM
AB_IN = 3 * NA_WIDTH + 2 * SG_WIDTH

RET_HEADS = 4
RET_QK_DIM = 256
RET_V_DIM = 512
RET_QK_WIDTH = RET_HEADS * RET_QK_DIM
RET_V_WIDTH = RET_HEADS * RET_V_DIM
RET_IN = 2 * RET_QK_WIDTH + 2 * RET_V_WIDTH
ROPE_BASE = 10000.0

N_EXPERTS = 16
EC_CAPACITY = 2
D_FF_EXPERT = 2816

ROWS = SEQ // GRID_W
R_LAT = BATCH * SEQ
R_CTX = BATCH * CTX_LEN
R_ALL = R_LAT + R_CTX
N_GROUPS = BATCH + 1

LANES = 128
TM = 512
VMEM_LIMIT = 56 * 1024 * 1024

NA_QROWS = 8
NA_KROWS = 16
NA_QT = NA_QROWS * GRID_W
NA_KT = 256
NA_NKB = NA_KROWS * GRID_W // NA_KT
RET_CHUNK = 256
MOE_TF = 256
MOE_ROW_SPLIT = 4
MOE_DMA_UNROLL = 8
MOE_UPDATE_ROWS = 256
ROUTE_PREFIX_BLOCK = 256
ROUTE_SLOT_HI = 8


def _group_of_tile(i):
    return jnp.minimum(i // (SEQ // TM), BATCH)


def _cparams(sem, vmem=VMEM_LIMIT):
    return pltpu.CompilerParams(dimension_semantics=sem, vmem_limit_bytes=vmem)


def _dot(a, b):
    return jnp.dot(a, b, preferred_element_type=F32)


def _dot_nt(a, b):
    return lax.dot_general(a, b, (((1,), (1,)), ((), ())), preferred_element_type=F32)


def _dot_tn(a, b):
    return lax.dot_general(a, b, (((0,), (0,)), ((), ())), preferred_element_type=F32)


def _sigmoid(x):
    return 1.0 / (1.0 + jnp.exp(-x))


def _gelu_tanh(x):
    c = np.float32(np.sqrt(2.0 / np.pi))
    return 0.5 * x * (1.0 + jnp.tanh(c * (x + np.float32(0.044715) * (x * x * x))))


def _adaln_kernel(c_ref, w_ref, b_ref, o_ref):
    cv = c_ref[...]
    s = (cv * _sigmoid(cv)).astype(BF16)
    o_ref[...] = _dot(s, w_ref[...].astype(BF16)) + b_ref[...]


def _adaln(cvec, ada_w, ada_b):
    tn = 1024
    n = 6 * D_MODEL
    return pl.pallas_call(
        _adaln_kernel,
        grid=(DEPTH, n // tn),
        in_specs=[
            pl.BlockSpec((8, D_MODEL), lambda l, j: (0, 0)),
            pl.BlockSpec((None, D_MODEL, tn), lambda l, j: (l, 0, j)),
            pl.BlockSpec((None, 1, tn), lambda l, j: (l, 0, j)),
        ],
        out_specs=pl.BlockSpec((None, 8, tn), lambda l, j: (l, 0, j)),
        out_shape=jax.ShapeDtypeStruct((DEPTH, 8, n), F32),
        compiler_params=_cparams(("arbitrary", "arbitrary")),
        name="adaln",
    )(cvec, ada_w, ada_b.reshape(DEPTH, 1, n))


NSUB = D_MODEL // LANES


def _read_token_tiled(ref, start, rows):
    return jnp.concatenate(
        [ref[pl.ds(start * NSUB + s, rows, stride=NSUB), :] for s in range(NSUB)], axis=-1)


def _write_token_tiled(ref, start, val):
    for s in range(NSUB):
        ref[pl.ds(start * NSUB + s, val.shape[0], stride=NSUB), :] = val[:, s * LANES:(s + 1) * LANES]


def _read_rows(x_ref):
    if x_ref.shape[-1] == D_MODEL:
        return x_ref[...]
    return _read_token_tiled(x_ref, 0, x_ref.shape[0] // NSUB)


def _row_spec(x, index_map):
    if x.shape[-1] == D_MODEL:
        return pl.BlockSpec((TM, D_MODEL), lambda *a: (index_map(*a), 0))
    return pl.BlockSpec((TM * NSUB, LANES), lambda *a: (index_map(*a), 0))


def _inproj_kernel(x_ref, g_ref, sh_ref, sc_ref, w_ref, z_ref, hb_ref):
    @pl.when(pl.program_id(1) == 0)
    def _():
        x = _read_rows(x_ref)
        y = x * lax.rsqrt(jnp.mean(x * x, axis=-1, keepdims=True) + EPS) * g_ref[...]
        hb_ref[...] = (y * (1.0 + sc_ref[...]) + sh_ref[...]).astype(BF16)

    z_ref[...] = _dot(hb_ref[...], w_ref[...]).astype(z_ref.dtype)


def _inproj(xall, norm_g, shift, scale, w_bf16, tn):
    n = w_bf16.shape[1]
    grp = lambda i, j: (_group_of_tile(i), 0, 0)
    return pl.pallas_call(
        _inproj_kernel,
        grid=(R_ALL // TM, n // tn),
        in_specs=[
            _row_spec(xall, lambda i, j: i),
            pl.BlockSpec((1, D_MODEL), lambda i, j: (0, 0)),
            pl.BlockSpec((None, 1, D_MODEL), grp),
            pl.BlockSpec((None, 1, D_MODEL), grp),
            pl.BlockSpec((D_MODEL, tn), lambda i, j: (0, j)),
        ],
        out_specs=pl.BlockSpec((TM, tn), lambda i, j: (i, j)),
        out_shape=jax.ShapeDtypeStruct((R_ALL, n), BF16),
        scratch_shapes=[pltpu.VMEM((TM, D_MODEL), BF16)],
        compiler_params=_cparams(("arbitrary", "arbitrary")),
        name="inproj",
    )(xall, norm_g.reshape(1, D_MODEL), shift, scale, w_bf16)


def _na_tile_codes():
    masked = 2 * NA_WIN_H - 1
    kinds = (0, 1, ROWS // NA_QROWS - 1)
    d = np.full((3, NA_QROWS, NA_KROWS), masked, np.int64)
    for v, j in enumerate(kinds):
        kb = int(np.clip(NA_QROWS * j - NA_WIN_H // 2, 0, ROWS - NA_KROWS))
        for a in range(NA_QROWS):
            i = NA_QROWS * j + a
            r0 = int(np.clip(i - NA_WIN_H // 2, 0, ROWS - NA_WIN_H))
            for r in range(NA_KROWS):
                if r0 <= kb + r < r0 + NA_WIN_H:
                    d[v, a, r] = kb + r - i + NA_WIN_H - 1
    pairs = d.reshape(-1, 2)
    uniq = sorted(set(map(tuple, pairs)))
    code = np.array([uniq.index(tuple(p)) for p in pairs], np.int32)
    return code, np.array(uniq, np.int64)


_NA_CODES, _NA_CODE_ROWS = _na_tile_codes()


def _na_bias_tiles(rpb):
    qc = np.arange(GRID_W)[:, None]
    kc = np.arange(GRID_W)[None, :]
    cstart = np.clip(qc - NA_WIN_W // 2, 0, GRID_W - NA_WIN_W)
    col_ok = (kc >= cstart) & (kc < cstart + NA_WIN_W)
    dcol = np.clip(kc - qc, 1 - NA_WIN_W, NA_WIN_W - 1) + NA_WIN_W - 1
    onehot = (dcol.reshape(1, -1) == np.arange(2 * NA_WIN_W - 1)[:, None]).astype(np.float32)
    toe = jnp.einsum('hdm,mq->hdq', rpb.astype(F32), jnp.asarray(onehot),
                     precision=lax.Precision.HIGHEST)
    toe = toe.reshape(NA_HEADS, 2 * NA_WIN_H - 1, GRID_W, GRID_W)
    toe = jnp.where(jnp.asarray(col_ok), toe, NEG_INF)
    toe = jnp.concatenate([toe, jnp.full((NA_HEADS, 1, GRID_W, GRID_W), NEG_INF, F32)], axis=1)
    return jnp.concatenate([toe[:, _NA_CODE_ROWS[:, 0]], toe[:, _NA_CODE_ROWS[:, 1]]], axis=-1)


def _pair_attention(q, keys, vals, add_bias=None):
    lane = lax.broadcasted_iota(jnp.int32, (1, LANES), 1)
    scale = np.float32(NA_HEAD_DIM ** -0.5)
    out = jnp.zeros((q.shape[0], LANES), F32)
    for hh in range(2):
        in_head = (lane < NA_HEAD_DIM) if hh == 0 else (lane >= NA_HEAD_DIM)
        qm = jnp.where(in_head, q, jnp.zeros_like(q))
        s = []
        for t, k in enumerate(keys):
            st = _dot_nt(qm, k) * scale
            if add_bias is not None:
                st = add_bias(hh, t, st)
            s.append(st)
        m = s[0].max(axis=-1, keepdims=True)
        for st in s[1:]:
            m = jnp.maximum(m, st.max(axis=-1, keepdims=True))
        den = jnp.zeros_like(m)
        acc = jnp.zeros((q.shape[0], LANES), F32)
        for st, v in zip(s, vals):
            p = jnp.exp(st - m)
            den = den + p.sum(axis=-1, keepdims=True)
            acc = acc + _dot(p.astype(BF16), v)
        out = jnp.where(in_head, acc / den, out)
    return out


def _na_kernel(code_ref, q_ref, k0, k1, k2, k3, v0, v1, v2, v3, kc_ref, vc_ref, tile_ref, o_ref):
    keys = [k0[...], k1[...], k2[...], k3[...], kc_ref[...]]
    vals = [v0[...], v1[...], v2[...], v3[...], vc_ref[...]]
    nj = pl.num_programs(1)
    j = pl.program_id(1)
    kind = jnp.minimum(j, 1) + jnp.maximum(j - (nj - 2), 0)
    tiles_per_row = NA_KROWS // 2
    code_base = kind * (NA_QROWS * tiles_per_row)

    def add_bias(hh, t, st):
        if t >= NA_NKB:
            return st
        rows = []
        for a in range(NA_QROWS):
            cols = []
            for u in range(NA_KT // LANES):
                code = code_ref[code_base + a * tiles_per_row + t * (NA_KT // LANES) + u]
                piece = st[a * GRID_W:(a + 1) * GRID_W, u * LANES:(u + 1) * LANES]
                cols.append(piece + tile_ref[hh, code])
            rows.append(jnp.concatenate(cols, axis=1))
        return jnp.concatenate(rows, axis=0)

    o_ref[...] = _pair_attention(q_ref[...], keys, vals, add_bias).astype(o_ref.dtype)


def _na_attention(z, tiles):
    nj = ROWS // NA_QROWS
    kblocks = SEQ // NA_KT
    qcol, kcol, vcol = 0, NA_WIDTH // LANES, 2 * NA_WIDTH // LANES
    n_codes = tiles.shape[1]

    def kbase(j):
        return jnp.clip(2 * j - 1, 0, kblocks - NA_NKB)

    def kspec(t, col):
        return pl.BlockSpec((NA_KT, LANES),
                            lambda p, j, b, code: (b * kblocks + kbase(j) + t, col + p))

    ctx_row = lambda b: R_LAT // CTX_LEN + b
    in_specs = (
        [pl.BlockSpec((NA_QT, LANES), lambda p, j, b, code: (b * nj + j, qcol + p))]
        + [kspec(t, kcol) for t in range(NA_NKB)]
        + [kspec(t, vcol) for t in range(NA_NKB)]
        + [pl.BlockSpec((CTX_LEN, LANES), lambda p, j, b, code: (ctx_row(b), kcol + p)),
           pl.BlockSpec((CTX_LEN, LANES), lambda p, j, b, code: (ctx_row(b), vcol + p)),
           pl.BlockSpec((2, n_codes, GRID_W, LANES), lambda p, j, b, code: (p, 0, 0, 0))])
    return pl.pallas_call(
        _na_kernel,
        grid_spec=pltpu.PrefetchScalarGridSpec(
            num_scalar_prefetch=1,
            grid=(NA_HEADS // 2, nj, BATCH),
            in_specs=in_specs,
            out_specs=pl.BlockSpec((NA_QT, LANES), lambda p, j, b, code: (b * nj + j, p))),
        out_shape=jax.ShapeDtypeStruct((R_LAT, NA_WIDTH), BF16),
        compiler_params=_cparams(("arbitrary", "arbitrary", "arbitrary")),
        name="na_attention",
    )(jnp.asarray(_NA_CODES), *([z] * (3 + 2 * NA_NKB)), tiles)


def _ctx_attn_kernel(q_ref, k_ref, v_ref, o_ref):
    o_ref[...] = _pair_attention(q_ref[...], [k_ref[...]], [v_ref[...]]).astype(o_ref.dtype)


def _ctx_attention(z):
    qcol, kcol, vcol = 0, NA_WIDTH // LANES, 2 * NA_WIDTH // LANES
    row = lambda b: R_LAT // CTX_LEN + b
    return pl.pallas_call(
        _ctx_attn_kernel,
        grid=(BATCH, NA_HEADS // 2),
        in_specs=[pl.BlockSpec((CTX_LEN, LANES), lambda b, p: (row(b), qcol + p)),
                  pl.BlockSpec((CTX_LEN, LANES), lambda b, p: (row(b), kcol + p)),
                  pl.BlockSpec((CTX_LEN, LANES), lambda b, p: (row(b), vcol + p))],
        out_specs=pl.BlockSpec((CTX_LEN, LANES), lambda b, p: (b, p)),
        out_shape=jax.ShapeDtypeStruct((R_CTX, NA_WIDTH), BF16),
        compiler_params=_cparams(("arbitrary", "arbitrary")),
        name="ctx_attention",
    )(z, z, z)


def _sgu_kernel(u_ref, v_ref, ng_ref, w_ref, bt_ref, o_ref):
    lane = lax.broadcasted_iota(jnp.int32, (1, LANES), 1)
    first = lane < SG_GROUP_DIM
    for ch in range(TM // SG_CHUNK):
        rows = slice(ch * SG_CHUNK, (ch + 1) * SG_CHUNK)
        u = _gelu_tanh(u_ref[rows, :].astype(F32))
        v = _gelu_tanh(v_ref[rows, :].astype(F32))
        vn = v * lax.rsqrt(jnp.mean(v * v, axis=-1, keepdims=True) + EPS) * ng_ref[...]
        vnb = vn.astype(BF16)
        for pr in range(SG_GROUPS // 2):
            cols = slice(pr * LANES, (pr + 1) * LANES)
            slab = vnb[:, cols]
            mixed = jnp.where(first, _dot(w_ref[2 * pr], slab), _dot(w_ref[2 * pr + 1], slab))
            o_ref[rows, cols] = (u[:, cols] * (mixed + bt_ref[:, cols])).astype(o_ref.dtype)


def _sgu(z, norm_g, w_bf16, bias_t):
    ucol = 3 * NA_WIDTH // SG_WIDTH
    return pl.pallas_call(
        _sgu_kernel,
        grid=(R_ALL // TM,),
        in_specs=[pl.BlockSpec((TM, SG_WIDTH), lambda i: (i, ucol)),
                  pl.BlockSpec((TM, SG_WIDTH), lambda i: (i, ucol + 1)),
                  pl.BlockSpec((1, SG_WIDTH), lambda i: (0, 0)),
                  pl.BlockSpec((SG_GROUPS, SG_CHUNK, SG_CHUNK), lambda i: (0, 0, 0)),
                  pl.BlockSpec((SG_CHUNK, SG_WIDTH), lambda i: (0, 0))],
        out_specs=pl.BlockSpec((TM, SG_WIDTH), lambda i: (i, 0)),
        out_shape=jax.ShapeDtypeStruct((R_ALL, SG_WIDTH), BF16),
        compiler_params=_cparams(("arbitrary",)),
        name="sgu",
    )(z, z, norm_g.reshape(1, SG_WIDTH), w_bf16, bias_t)


def _outproj_kernel(*refs, n_lhs):
    lhs = refs[:n_lhs]
    ws = refs[n_lhs:2 * n_lhs]
    x_ref, g1_ref, ng_ref, sh_ref, sc_ref, wr_ref, xo_ref, h2_ref, aff_ref = refs[2 * n_lhs:]
    y = _dot(lhs[0][...], ws[0][...])
    for a, w in zip(lhs[1:], ws[1:]):
        y = y + _dot(a[...], w[...])
    xn = _read_rows(x_ref) + g1_ref[...] * y
    _write_token_tiled(xo_ref, 0, xn)
    hn = xn * lax.rsqrt(jnp.mean(xn * xn, axis=-1, keepdims=True) + EPS) * ng_ref[...]
    h2 = hn * (1.0 + sc_ref[...]) + sh_ref[...]
    _write_token_tiled(h2_ref, 0, h2)
    logits = _dot_nt(wr_ref[...], h2.astype(BF16))
    e = jnp.exp(logits - logits.max(axis=0, keepdims=True))
    aff_ref[...] = e / e.sum(axis=0, keepdims=True)


def _outproj(lhs_list, w_list, xall, gate1, norm_g, shift, scale, w_router_pad, n_rows):
    n_lhs = len(lhs_list)
    grp = lambda i: (_group_of_tile(i), 0, 0)
    row = lambda i: (i, 0)
    const2 = lambda i: (0, 0)
    in_specs = (
        [pl.BlockSpec((TM, a.shape[1]), row) for a in lhs_list]
        + [pl.BlockSpec(w.shape, const2) for w in w_list]
        + [_row_spec(xall, lambda i: i),
           pl.BlockSpec((None, 1, D_MODEL), grp),
           pl.BlockSpec((1, D_MODEL), const2),
           pl.BlockSpec((None, 1, D_MODEL), grp),
           pl.BlockSpec((None, 1, D_MODEL), grp),
           pl.BlockSpec((N_EXPERTS, D_MODEL), const2)])
    tiled = (n_rows * NSUB, LANES)
    tiled_spec = pl.BlockSpec((TM * NSUB, LANES), row)
    return pl.pallas_call(
        functools.partial(_outproj_kernel, n_lhs=n_lhs),
        grid=(n_rows // TM,),
        in_specs=in_specs,
        out_specs=[tiled_spec, tiled_spec, pl.BlockSpec((N_EXPERTS, TM), lambda i: (0, i))],
        out_shape=[jax.ShapeDtypeStruct(tiled, F32),
                   jax.ShapeDtypeStruct(tiled, F32),
                   jax.ShapeDtypeStruct((N_EXPERTS, n_rows), F32)],
        compiler_params=_cparams(("arbitrary",)),
        name="outproj",
    )(*lhs_list, *w_list, xall, gate1, norm_g.reshape(1, D_MODEL), shift, scale, w_router_pad)


def _moe_segments(rows):
    cap = EC_CAPACITY * SEQ // N_EXPERTS
    segs = [(b * cap, (b + 1) * cap, b) for b in range(BATCH)]
    if rows > BATCH * cap:
        segs.append((BATCH * cap, rows, BATCH))
    return segs


def _moe_kernel(idx_ref, h_hbm, x_hbm, gate_ref, g2_ref, wg_ref, wu_ref, wd_ref, xo_hbm,
                xg, xb, acc, res, wg_s, wu_s, wd_s, sem_h, sem_r, sem_s, *, rows):
    del x_hbm
    e, f = pl.program_id(0), pl.program_id(1)
    ne, nf = pl.num_programs(0), pl.num_programs(1)

    def for_slots(expert, body):
        base = expert * rows

        def group(o, carry):
            for u in range(MOE_DMA_UNROLL):
                i = o * MOE_DMA_UNROLL + u
                body(pl.ds(pl.multiple_of(i * NSUB, NSUB), NSUB), idx_ref[base + i])
            return carry

        lax.fori_loop(0, rows // MOE_DMA_UNROLL, group, 0)

    def token_copy(slot, n):
        return pltpu.make_async_copy(h_hbm.at[n], xg.at[slot, :], sem_h)

    def fetch_copy(slot, n):
        return pltpu.make_async_copy(xo_hbm.at[n], res.at[slot, :], sem_r)

    def writeback_copy(slot, n):
        return pltpu.make_async_copy(res.at[slot, :], xo_hbm.at[n], sem_s)

    all_tokens = pltpu.make_async_copy(xg, xg, sem_h)
    all_fetched = pltpu.make_async_copy(res, res, sem_r)
    all_written = pltpu.make_async_copy(res, res, sem_s)

    @pl.when(f == 0)
    def _():
        @pl.when(e == 0)
        def _():
            for_slots(0, lambda slot, n: token_copy(slot, n).start())

        all_tokens.wait()
        for r0 in range(0, rows, MOE_UPDATE_ROWS):
            nr = min(MOE_UPDATE_ROWS, rows - r0)
            xb[r0:r0 + nr, :] = _read_token_tiled(xg, r0, nr).astype(BF16)
        acc[...] = jnp.zeros_like(acc)

    @pl.when((f == 1) & (e + 1 < ne))
    def _():
        for_slots(e + 1, lambda slot, n: token_copy(slot, n).start())

    @pl.when(f == 2)
    def _():
        @pl.when(e > 0)
        def _():
            all_written.wait()

        for_slots(e, lambda slot, n: fetch_copy(slot, n).start())

    wg_s[...] = wg_ref[...].astype(BF16)
    wu_s[...] = wu_ref[...].astype(BF16)
    wd_s[...] = wd_ref[...].astype(BF16)
    rc = rows // MOE_ROW_SPLIT
    for r in range(MOE_ROW_SPLIT):
        sl = slice(r * rc, (r + 1) * rc)
        x = xb[sl, :]
        a = _dot(x, wg_s[...])
        b = _dot(x, wu_s[...])
        hid = (a * _sigmoid(a) * b).astype(BF16)
        acc[sl, :] += _dot(hid, wd_s[...])

    @pl.when(f == nf - 1)
    def _():
        all_fetched.wait()
        for lo, hi, g in _moe_segments(rows):
            for r0 in range(lo, hi, MOE_UPDATE_ROWS):
                nr = min(MOE_UPDATE_ROWS, hi - r0)
                upd = acc[r0:r0 + nr, :] * gate_ref[r0:r0 + nr, :] * g2_ref[g]
                _write_token_tiled(res, r0, _read_token_tiled(res, r0, nr) + upd)
        for_slots(e, lambda slot, n: writeback_copy(slot, n).start())

        @pl.when(e == ne - 1)
        def _():
            all_written.wait()


def _moe_ffn(flat_idx, h2, x1, gates, gate2, w_gate, w_up, w_down, layer):
    rows = flat_idx.shape[1]
    assert rows % MOE_DMA_UNROLL == 0 and rows % MOE_ROW_SPLIT == 0
    nf = D_FF_EXPERT // MOE_TF
    assert nf > 3
    n_tokens = x1.shape[0] // NSUB
    as_tokens = lambda t: t.reshape(n_tokens, NSUB, LANES)
    out = pl.pallas_call(
        functools.partial(_moe_kernel, rows=rows),
        grid_spec=pltpu.PrefetchScalarGridSpec(
            num_scalar_prefetch=1,
            grid=(N_EXPERTS, nf),
            in_specs=[pl.BlockSpec(memory_space=pl.ANY),
                      pl.BlockSpec(memory_space=pl.ANY),
                      pl.BlockSpec((None, rows, 1), lambda e, f, idx: (e, 0, 0)),
                      pl.BlockSpec((N_GROUPS, 1, D_MODEL), lambda e, f, idx: (0, 0, 0)),
                      pl.BlockSpec((None, None, D_MODEL, MOE_TF), lambda e, f, idx: (layer, e, 0, f)),
                      pl.BlockSpec((None, None, D_MODEL, MOE_TF), lambda e, f, idx: (layer, e, 0, f)),
                      pl.BlockSpec((None, None, MOE_TF, D_MODEL), lambda e, f, idx: (layer, e, f, 0))],
            out_specs=pl.BlockSpec(memory_space=pl.ANY),
            scratch_shapes=[pltpu.VMEM((rows * NSUB, LANES), F32),
                            pltpu.VMEM((rows, D_MODEL), BF16),
                            pltpu.VMEM((rows, D_MODEL), F32),
                            pltpu.VMEM((rows * NSUB, LANES), F32),
                            pltpu.VMEM((D_MODEL, MOE_TF), BF16),
                            pltpu.VMEM((D_MODEL, MOE_TF), BF16),
                            pltpu.VMEM((MOE_TF, D_MODEL), BF16),
                            pltpu.SemaphoreType.DMA,
                            pltpu.SemaphoreType.DMA,
                            pltpu.SemaphoreType.DMA]),
        out_shape=jax.ShapeDtypeStruct((n_tokens, NSUB, LANES), F32),
        input_output_aliases={2: 0},
        compiler_params=_cparams(("arbitrary", "arbitrary")),
        name="moe_ffn",
    )(flat_idx.reshape(-1), as_tokens(h2), as_tokens(x1), gates, gate2, w_gate, w_up, w_down)
    return out.reshape(x1.shape)


def _exclusive_prefix(mask, upper):
    out = []
    run = jnp.zeros((mask.shape[0], 1), F32)
    for c in range(mask.shape[1] // ROUTE_PREFIX_BLOCK):
        m = mask[:, c * ROUTE_PREFIX_BLOCK:(c + 1) * ROUTE_PREFIX_BLOCK]
        out.append(_dot(m.astype(BF16), upper) + run)
        run = run + jnp.sum(m, axis=1, keepdims=True)
    return jnp.concatenate(out, axis=1)


def _route_kernel(aff_ref, idx_ref, gate_ref, pos3, sel3, val3, *, cap):
    n = aff_ref.shape[1]
    v = aff_ref[...]
    bits = pltpu.bitcast(v, jnp.int32)
    thr = jnp.zeros((N_EXPERTS, 1), jnp.int32)
    for bit in range(30, -1, -1):
        cand = thr | (1 << bit)
        cnt = jnp.sum((bits >= cand).astype(jnp.int32), axis=1, keepdims=True)
        thr = jnp.where(cnt >= cap, cand, thr)
    ri = lax.broadcasted_iota(jnp.int32, (ROUTE_PREFIX_BLOCK, ROUTE_PREFIX_BLOCK), 0)
    ci = lax.broadcasted_iota(jnp.int32, (ROUTE_PREFIX_BLOCK, ROUTE_PREFIX_BLOCK), 1)
    upper = (ri < ci).astype(BF16)
    gt = bits > thr
    eq = bits == thr
    need = (cap - jnp.sum(gt.astype(jnp.int32), axis=1, keepdims=True)).astype(F32)
    sel = gt | (eq & (_exclusive_prefix(eq.astype(F32), upper) < need))
    sel_f = sel.astype(F32)
    pos = _exclusive_prefix(sel_f, upper)
    for e in range(N_EXPERTS):
        pos3[e] = pos[e:e + 1, :].astype(jnp.int32)
        sel3[e] = sel_f[e:e + 1, :]
        val3[e] = v[e:e + 1, :]

    tok = lax.broadcasted_iota(jnp.int32, (1, n), 1)
    tok_hi = (tok >> 7).astype(F32)
    tok_lo = (tok & (LANES - 1)).astype(F32)
    hi_row = lax.broadcasted_iota(jnp.int32, (ROUTE_SLOT_HI, 1), 0)
    lo_row = lax.broadcasted_iota(jnp.int32, (LANES, 1), 0)

    def per_expert(e, carry):
        p, s, a = pos3[e], sel3[e], val3[e]
        a1 = a.astype(BF16).astype(F32)
        a2 = (a - a1).astype(BF16).astype(F32)
        a3 = a - a1 - a2
        hot_hi = jnp.where((p >> 7) == hi_row, s, 0.0)
        hot_lo = ((p & (LANES - 1)) == lo_row).astype(BF16)
        lhs = jnp.concatenate([hot_hi * tok_hi, hot_hi * tok_lo,
                               hot_hi * a1, hot_hi * a2, hot_hi * a3], axis=0).astype(BF16)
        out = _dot_nt(lhs, hot_lo)
        h = ROUTE_SLOT_HI
        idx_ref[e] = (out[0:h] * float(LANES) + out[h:2 * h]).astype(jnp.int32)
        gate_ref[e] = (out[2 * h:3 * h] + out[3 * h:4 * h]) + out[4 * h:5 * h]
        return carry

    lax.fori_loop(0, N_EXPERTS, per_expert, 0)


def _route(aff_t, n, cap, first_block, n_sets):
    assert cap <= ROUTE_SLOT_HI * LANES and n % ROUTE_PREFIX_BLOCK == 0 and n < (1 << 14)
    shape = (n_sets, N_EXPERTS, ROUTE_SLOT_HI, LANES)
    spec = pl.BlockSpec((None, N_EXPERTS, ROUTE_SLOT_HI, LANES), lambda s: (s, 0, 0, 0))
    return pl.pallas_call(
        functools.partial(_route_kernel, cap=cap),
        grid=(n_sets,),
        in_specs=[pl.BlockSpec((N_EXPERTS, n), lambda s: (0, first_block + s))],
        out_specs=[spec, spec],
        out_shape=[jax.ShapeDtypeStruct(shape, jnp.int32), jax.ShapeDtypeStruct(shape, F32)],
        scratch_shapes=[pltpu.VMEM((N_EXPERTS, 1, n), jnp.int32),
                        pltpu.VMEM((N_EXPERTS, 1, n), F32),
                        pltpu.VMEM((N_EXPERTS, 1, n), F32)],
        compiler_params=_cparams(("arbitrary",)),
        name="route",
    )(aff_t)


def _expert_choice(h2, x1, aff_t, gate2, w_gate, w_up, w_down, layer, with_ctx):
    def slots(idx, gate, cap, row0, set_len):
        idx = idx.reshape(BATCH, N_EXPERTS, -1)[:, :, :cap]
        gate = gate.reshape(BATCH, N_EXPERTS, -1)[:, :, :cap]
        idx = idx + row0 + (jnp.arange(BATCH, dtype=jnp.int32) * set_len)[:, None, None]
        return (jnp.swapaxes(idx, 0, 1).reshape(N_EXPERTS, BATCH * cap),
                jnp.swapaxes(gate, 0, 1).reshape(N_EXPERTS, BATCH * cap))

    cap = EC_CAPACITY * SEQ // N_EXPERTS
    flat, gate = slots(*_route(aff_t, SEQ, cap, 0, BATCH), cap, 0, SEQ)
    if with_ctx:
        cap_c = EC_CAPACITY * CTX_LEN // N_EXPERTS
        flat_c, gate_c = slots(*_route(aff_t, CTX_LEN, cap_c, R_LAT // CTX_LEN, BATCH),
                               cap_c, R_LAT, CTX_LEN)
        flat = jnp.concatenate([flat, flat_c], axis=1)
        gate = jnp.concatenate([gate, gate_c], axis=1)
    return _moe_ffn(flat, h2, x1, gate[..., None], gate2, w_gate, w_up, w_down, layer)


def _final_norm_kernel(x_ref, ng_ref, o_ref):
    x = _read_rows(x_ref)
    o_ref[...] = x * lax.rsqrt(jnp.mean(x * x, axis=-1, keepdims=True) + EPS) * ng_ref[...]


def _final_norm(x, norm_g):
    n_rows = x.size // D_MODEL
    return pl.pallas_call(
        _final_norm_kernel,
        grid=(n_rows // TM,),
        in_specs=[_row_spec(x, lambda i: i),
                  pl.BlockSpec((1, D_MODEL), lambda i: (0, 0))],
        out_specs=pl.BlockSpec((TM, D_MODEL), lambda i: (i, 0)),
        out_shape=jax.ShapeDtypeStruct((n_rows, D_MODEL), F32),
        compiler_params=_cparams(("arbitrary",)),
        name="final_norm",
    )(x, norm_g.reshape(1, D_MODEL))


def _rope(t, cos, sin):
    half = RET_QK_DIM // 2
    t1, t2 = t[:, :half], t[:, half:]
    return jnp.concatenate([t1 * cos - t2 * sin, t1 * sin + t2 * cos], axis=-1)


def _retention_kernel(lg_ref, q_ref, k_ref, v_ref, cos_ref, sin_ref, kc_ref, vc_ref, *rest,
                      backward):
    if backward:
        o_ref, s_ref, dmat_ref, qd_ref, kd_ref = rest
    else:
        ob_ref, g_ref, o_ref, s_ref, dmat_ref, qd_ref, kd_ref = rest
    c = RET_CHUNK
    h = pl.program_id(1)
    lg = lg_ref[1 if backward else 0, h]
    kscale = np.float32(RET_QK_DIM ** -0.5)

    @pl.when(pl.program_id(2) == 0)
    def _():
        ia = lax.broadcasted_iota(jnp.int32, (c, c), 0)
        ib = lax.broadcasted_iota(jnp.int32, (c, c), 1)
        diff = ((ib - ia) if backward else (ia - ib)).astype(F32)
        dmat_ref[...] = jnp.where(diff >= 0, jnp.exp(jnp.maximum(diff, 0.0) * lg), 0.0)
        pos = lax.broadcasted_iota(jnp.int32, (c, RET_QK_DIM), 0).astype(F32)
        if backward:
            qd_ref[...] = jnp.exp((c - pos) * lg)
            kd_ref[...] = jnp.exp(pos * lg) * kscale
        else:
            qd_ref[...] = jnp.exp((pos + 1.0) * lg)
            kd_ref[...] = jnp.exp((c - 1.0 - pos) * lg) * kscale
        cpos = lax.broadcasted_iota(jnp.int32, (CTX_LEN, RET_QK_DIM), 0).astype(F32)
        cw = jnp.exp(cpos * lg) if backward else jnp.exp((CTX_LEN - 1.0 - cpos) * lg)
        kcw = (kc_ref[...].astype(F32) * (cw * kscale)).astype(BF16)
        s_ref[...] = _dot_tn(kcw, vc_ref[...])

    cos = cos_ref[...]
    sin = sin_ref[...]
    q = _rope(q_ref[...].astype(F32), cos, sin)
    k = _rope(k_ref[...].astype(F32), cos, sin)
    v = v_ref[...]
    qb = q.astype(BF16)
    att = _dot_nt(qb, (k * kscale).astype(BF16)) * dmat_ref[...]
    s_old = s_ref[...]
    o = _dot(att.astype(BF16), v) + _dot((q * qd_ref[...]).astype(BF16), s_old.astype(BF16))
    chunk_decay = jnp.exp(jnp.zeros((1, RET_V_DIM), F32) + c * lg)
    s_ref[...] = s_old * chunk_decay + _dot_tn((k * kd_ref[...]).astype(BF16), v)
    if backward:
        o_ref[...] = o.astype(o_ref.dtype)
    else:
        of = o + ob_ref[...].astype(F32)
        y = of * lax.rsqrt(jnp.mean(of * of, axis=-1, keepdims=True) + EPS)
        g = g_ref[...].astype(F32)
        o_ref[...] = (g * _sigmoid(g) * y).astype(o_ref.dtype)


def _retention_pass(log_gamma, z, cos, sin, o_back):
    backward = o_back is None
    c = RET_CHUNK
    n = SEQ // c
    half = RET_QK_DIM // 2
    qk_cols = RET_QK_WIDTH // RET_QK_DIM
    v_base = 2 * RET_QK_WIDTH // RET_V_DIM
    g_base = v_base + RET_V_WIDTH // RET_V_DIM
    chunk = (lambda t: n - 1 - t) if backward else (lambda t: t)
    row = lambda b, t: b * n + chunk(t)
    ctx_row = lambda b: R_LAT // CTX_LEN + b
    in_specs = [
        pl.BlockSpec((c, RET_QK_DIM), lambda b, h, t, lg: (row(b, t), h)),
        pl.BlockSpec((c, RET_QK_DIM), lambda b, h, t, lg: (row(b, t), qk_cols + h)),
        pl.BlockSpec((c, RET_V_DIM), lambda b, h, t, lg: (row(b, t), v_base + h)),
        pl.BlockSpec((c, half), lambda b, h, t, lg: (chunk(t), 0)),
        pl.BlockSpec((c, half), lambda b, h, t, lg: (chunk(t), 0)),
        pl.BlockSpec((CTX_LEN, RET_QK_DIM), lambda b, h, t, lg: (ctx_row(b), qk_cols + h)),
        pl.BlockSpec((CTX_LEN, RET_V_DIM), lambda b, h, t, lg: (ctx_row(b), v_base + h)),
    ]
    args = [z, z, z, cos, sin, z, z]
    if not backward:
        in_specs += [
            pl.BlockSpec((c, RET_V_DIM), lambda b, h, t, lg: (row(b, t), h)),
            pl.BlockSpec((c, RET_V_DIM), lambda b, h, t, lg: (row(b, t), g_base + h)),
        ]
        args += [o_back, z]
    return pl.pallas_call(
        functools.partial(_retention_kernel, backward=backward),
        grid_spec=pltpu.PrefetchScalarGridSpec(
            num_scalar_prefetch=1,
            grid=(BATCH, RET_HEADS, n),
            in_specs=in_specs,
            out_specs=pl.BlockSpec((c, RET_V_DIM), lambda b, h, t, lg: (row(b, t), h)),
            scratch_shapes=[pltpu.VMEM((RET_QK_DIM, RET_V_DIM), F32),
                            pltpu.VMEM((c, c), F32),
                            pltpu.VMEM((c, RET_QK_DIM), F32),
                            pltpu.VMEM((c, RET_QK_DIM), F32)]),
        out_shape=jax.ShapeDtypeStruct((R_LAT, RET_V_WIDTH), BF16),
        compiler_params=_cparams(("arbitrary", "arbitrary", "arbitrary")),
        name="retention_bwd" if backward else "retention_fwd",
    )(log_gamma, *args)


def _rope_tables():
    axis_dim = RET_QK_DIM // 4
    inv = 1.0 / (ROPE_BASE ** (jnp.arange(0, 2 * axis_dim, 2, dtype=F32) / (2 * axis_dim)))
    t = jnp.arange(SEQ)
    r = (t // GRID_W).astype(F32)
    col = (t % GRID_W).astype(F32)
    ang = jnp.concatenate([r[:, None] * inv, col[:, None] * inv], axis=-1)
    return jnp.cos(ang), jnp.sin(ang)


def _split_mod(mod_layer):
    m = mod_layer[:N_GROUPS].reshape(N_GROUPS, 6, 1, D_MODEL)
    return [m[:, k] for k in range(6)]


def _router_t(w_router):
    return w_router.T.astype(BF16)


def kernel(x, c, ctx, c_ctx, ada_w, ada_b, norm1_g, norm2_g, ab_w_in, ab_w_out, na_rpb, sgu_norm_g, sgu_w, sgu_b, ret_w_in, ret_w_out, ret_decay_logit, moe_router, moe_w_gate, moe_w_up, moe_w_down, final_norm_g):
    assert DEPTH == 2 and x.shape == (BATCH, SEQ, D_MODEL) and ctx.shape == (BATCH, CTX_LEN, D_MODEL)

    cvec = jnp.zeros((8, D_MODEL), F32).at[:BATCH].set(c).at[BATCH].set(c_ctx)
    mod = _adaln(cvec, ada_w, ada_b)
    xall = jnp.concatenate([x.reshape(R_LAT, D_MODEL), ctx.reshape(R_CTX, D_MODEL)], axis=0)

    sh1, sc1, g1, sh2, sc2, g2 = _split_mod(mod[0])
    z = _inproj(xall, norm1_g[0], sh1, sc1, ab_w_in[0].astype(BF16), tn=AB_IN)
    a_all = jnp.concatenate([_na_attention(z, _na_bias_tiles(na_rpb[0])), _ctx_attention(z)], axis=0)
    bias_t = jnp.repeat(sgu_b[0].T, SG_GROUP_DIM, axis=1)
    bsg = _sgu(z, sgu_norm_g[0], sgu_w[0].astype(BF16), bias_t)
    w_out = ab_w_out[0].astype(BF16)
    x1, h2, aff = _outproj([a_all, bsg], [w_out[:NA_WIDTH], w_out[NA_WIDTH:]], xall, g1,
                           norm2_g[0], sh2, sc2, _router_t(moe_router[0]), R_ALL)
    xall = _expert_choice(h2, x1, aff, g2, moe_w_gate, moe_w_up, moe_w_down, 0, with_ctx=True)

    sh1, sc1, g1, sh2, sc2, g2 = _split_mod(mod[1])
    z = _inproj(xall, norm1_g[1], sh1, sc1, ret_w_in[0].astype(BF16), tn=2048)
    log_gamma = jax.nn.log_sigmoid(ret_decay_logit[0].astype(F32))
    cos, sin = _rope_tables()
    o_back = _retention_pass(log_gamma, z, cos, sin, None)
    ypre = _retention_pass(log_gamma, z, cos, sin, o_back)
    x1, h2, aff = _outproj([ypre], [ret_w_out[0].astype(BF16)], xall, g1,
                           norm2_g[1], sh2, sc2, _router_t(moe_router[1]), R_LAT)
    x2 = _expert_choice(h2, x1, aff, g2, moe_w_gate, moe_w_up, moe_w_down, 1, with_ctx=False)
    return _final_norm(x2, final_norm_g).reshape(BATCH, SEQ, D_MODEL)
```

```python
import functools
from typing import NamedTuple

import numpy as np
import jax
import jax.numpy as jnp
from jax import lax
from jax.experimental import pallas as pl
from jax.experimental.pallas import tpu as pltpu

F32 = jnp.float32
BF16 = jnp.bfloat16

D_MODEL = 1024
BATCH = 2
SEQ = 8192
DEPTH = 2
GRID_W = 64
CTX_LEN = 256
EPS = 1e-6
NEG_INF = -1e30

NA_HEADS = 8
NA_HEAD_DIM = 64
NA_WIN_H = 8
NA_WIN_W = 16
NA_WIDTH = NA_HEADS * NA_HEAD_DIM
SG_GROUPS = 8
SG_GROUP_DIM = 64
SG_CHUNK = 128
SG_WIDTH = SG_GROUPS * SG_GROUP_DIM
AB_IN = 3 * NA_WIDTH + 2 * SG_WIDTH

RET_HEADS = 4
RET_QK_DIM = 256
RET_V_DIM = 512
RET_QK_WIDTH = RET_HEADS * RET_QK_DIM
RET_V_WIDTH = RET_HEADS * RET_V_DIM
RET_IN = 2 * RET_QK_WIDTH + 2 * RET_V_WIDTH
ROPE_BASE = 10000.0

N_EXPERTS = 16
EC_CAPACITY = 2
D_FF_EXPERT = 2816

ROWS = SEQ // GRID_W
R_LAT = BATCH * SEQ
R_CTX = BATCH * CTX_LEN
R_ALL = R_LAT + R_CTX
N_GROUPS = BATCH + 1

LANES = 128
TM = 512
VMEM_LIMIT = 56 * 1024 * 1024

NA_QROWS = 8
NA_KROWS = 16
NA_QT = NA_QROWS * GRID_W
NA_KT = 256
NA_NKB = NA_KROWS * GRID_W // NA_KT
RET_CHUNK = 256
MOE_TF = 256
MOE_NF = D_FF_EXPERT // MOE_TF
MOE_ROW_SPLIT = 4
MOE_WRITE_STEPS = 4
MOE_DMA_UNROLL = 8
MOE_UPDATE_ROWS = 256
ROUTE_PREFIX_BLOCK = 256
ROUTE_SLOT_HI = 8


def _group_of_tile(i):
    return jnp.minimum(i // (SEQ // TM), BATCH)


def _cparams(sem, vmem=VMEM_LIMIT):
    return pltpu.CompilerParams(dimension_semantics=sem, vmem_limit_bytes=vmem)


def _dot(a, b):
    return jnp.dot(a, b, preferred_element_type=F32)


def _dot_nt(a, b):
    return lax.dot_general(a, b, (((1,), (1,)), ((), ())), preferred_element_type=F32)


def _dot_tn(a, b):
    return lax.dot_general(a, b, (((0,), (0,)), ((), ())), preferred_element_type=F32)


def _sigmoid(x):
    return 1.0 / (1.0 + jnp.exp(-x))


def _gelu_tanh(x):
    c = np.float32(np.sqrt(2.0 / np.pi))
    return 0.5 * x * (1.0 + jnp.tanh(c * (x + np.float32(0.044715) * (x * x * x))))


def _adaln_kernel(c_ref, w_ref, b_ref, o_ref):
    cv = c_ref[...]
    s = (cv * _sigmoid(cv)).astype(BF16)
    o_ref[...] = _dot(s, w_ref[...].astype(BF16)) + b_ref[...]


def _adaln(cvec, ada_w, ada_b):
    tn = 1024
    n = 6 * D_MODEL
    return pl.pallas_call(
        _adaln_kernel,
        grid=(DEPTH, n // tn),
        in_specs=[
            pl.BlockSpec((8, D_MODEL), lambda l, j: (0, 0)),
            pl.BlockSpec((None, D_MODEL, tn), lambda l, j: (l, 0, j)),
            pl.BlockSpec((None, 1, tn), lambda l, j: (l, 0, j)),
        ],
        out_specs=pl.BlockSpec((None, 8, tn), lambda l, j: (l, 0, j)),
        out_shape=jax.ShapeDtypeStruct((DEPTH, 8, n), F32),
        compiler_params=_cparams(("arbitrary", "arbitrary")),
        name="adaln",
    )(cvec, ada_w, ada_b.reshape(DEPTH, 1, n))


NSUB = D_MODEL // LANES


def _read_token_tiled(ref, start, rows):
    return jnp.concatenate(
        [ref[pl.ds(start * NSUB + s, rows, stride=NSUB), :] for s in range(NSUB)], axis=-1)


def _write_token_tiled(ref, start, val):
    for s in range(NSUB):
        ref[pl.ds(start * NSUB + s, val.shape[0], stride=NSUB), :] = val[:, s * LANES:(s + 1) * LANES]


def _read_rows(x_ref):
    if x_ref.shape[-1] == D_MODEL:
        return x_ref[...]
    return _read_token_tiled(x_ref, 0, x_ref.shape[0] // NSUB)


def _row_spec(x, index_map):
    if x.shape[-1] == D_MODEL:
        return pl.BlockSpec((TM, D_MODEL), lambda *a: (index_map(*a), 0))
    return pl.BlockSpec((TM * NSUB, LANES), lambda *a: (index_map(*a), 0))


def _inproj_kernel(x_ref, g_ref, sh_ref, sc_ref, w_ref, z_ref, hb_ref):
    @pl.when(pl.program_id(1) == 0)
    def _():
        x = _read_rows(x_ref)
        y = x * lax.rsqrt(jnp.mean(x * x, axis=-1, keepdims=True) + EPS) * g_ref[...]
        hb_ref[...] = (y * (1.0 + sc_ref[...]) + sh_ref[...]).astype(BF16)

    z_ref[...] = _dot(hb_ref[...], w_ref[...]).astype(z_ref.dtype)


def _inproj(xall, norm_g, shift, scale, w_bf16, tn):
    n = w_bf16.shape[1]
    grp = lambda i, j: (_group_of_tile(i), 0, 0)
    return pl.pallas_call(
        _inproj_kernel,
        grid=(R_ALL // TM, n // tn),
        in_specs=[
            _row_spec(xall, lambda i, j: i),
            pl.BlockSpec((1, D_MODEL), lambda i, j: (0, 0)),
            pl.BlockSpec((None, 1, D_MODEL), grp),
            pl.BlockSpec((None, 1, D_MODEL), grp),
            pl.BlockSpec((D_MODEL, tn), lambda i, j: (0, j)),
        ],
        out_specs=pl.BlockSpec((TM, tn), lambda i, j: (i, j)),
        out_shape=jax.ShapeDtypeStruct((R_ALL, n), BF16),
        scratch_shapes=[pltpu.VMEM((TM, D_MODEL), BF16)],
        compiler_params=_cparams(("arbitrary", "arbitrary")),
        name="inproj",
    )(xall, norm_g.reshape(1, D_MODEL), shift, scale, w_bf16)


def _na_tile_codes():
    masked = 2 * NA_WIN_H - 1
    kinds = (0, 1, ROWS // NA_QROWS - 1)
    d = np.full((3, NA_QROWS, NA_KROWS), masked, np.int64)
    for v, j in enumerate(kinds):
        kb = int(np.clip(NA_QROWS * j - NA_WIN_H // 2, 0, ROWS - NA_KROWS))
        for a in range(NA_QROWS):
            i = NA_QROWS * j + a
            r0 = int(np.clip(i - NA_WIN_H // 2, 0, ROWS - NA_WIN_H))
            for r in range(NA_KROWS):
                if r0 <= kb + r < r0 + NA_WIN_H:
                    d[v, a, r] = kb + r - i + NA_WIN_H - 1
    pairs = d.reshape(-1, 2)
    uniq = sorted(set(map(tuple, pairs)))
    code = np.array([uniq.index(tuple(p)) for p in pairs], np.int32)
    return code, np.array(uniq, np.int64)


_NA_CODES, _NA_CODE_ROWS = _na_tile_codes()


def _na_bias_tiles(rpb):
    qc = np.arange(GRID_W)[:, None]
    kc = np.arange(GRID_W)[None, :]
    cstart = np.clip(qc - NA_WIN_W // 2, 0, GRID_W - NA_WIN_W)
    col_ok = (kc >= cstart) & (kc < cstart + NA_WIN_W)
    dcol = np.clip(kc - qc, 1 - NA_WIN_W, NA_WIN_W - 1) + NA_WIN_W - 1
    onehot = (dcol.reshape(1, -1) == np.arange(2 * NA_WIN_W - 1)[:, None]).astype(np.float32)
    toe = jnp.einsum('hdm,mq->hdq', rpb.astype(F32), jnp.asarray(onehot),
                     precision=lax.Precision.HIGHEST)
    toe = toe.reshape(NA_HEADS, 2 * NA_WIN_H - 1, GRID_W, GRID_W)
    toe = jnp.where(jnp.asarray(col_ok), toe, NEG_INF)
    toe = jnp.concatenate([toe, jnp.full((NA_HEADS, 1, GRID_W, GRID_W), NEG_INF, F32)], axis=1)
    return jnp.concatenate([toe[:, _NA_CODE_ROWS[:, 0]], toe[:, _NA_CODE_ROWS[:, 1]]], axis=-1)


def _pair_attention(q, keys, vals, add_bias=None):
    lane = lax.broadcasted_iota(jnp.int32, (1, LANES), 1)
    scale = np.float32(NA_HEAD_DIM ** -0.5)
    out = jnp.zeros((q.shape[0], LANES), F32)
    for hh in range(2):
        in_head = (lane < NA_HEAD_DIM) if hh == 0 else (lane >= NA_HEAD_DIM)
        qm = jnp.where(in_head, q, jnp.zeros_like(q))
        s = []
        for t, k in enumerate(keys):
            st = _dot_nt(qm, k) * scale
            if add_bias is not None:
                st = add_bias(hh, t, st)
            s.append(st)
        m = s[0].max(axis=-1, keepdims=True)
        for st in s[1:]:
            m = jnp.maximum(m, st.max(axis=-1, keepdims=True))
        den = jnp.zeros_like(m)
        acc = jnp.zeros((q.shape[0], LANES), F32)
        for st, v in zip(s, vals):
            p = jnp.exp(st - m)
            den = den + p.sum(axis=-1, keepdims=True)
            acc = acc + _dot(p.astype(BF16), v)
        out = jnp.where(in_head, acc / den, out)
    return out


def _na_kernel(code_ref, q_ref, k0, k1, k2, k3, v0, v1, v2, v3, kc_ref, vc_ref, tile_ref, o_ref):
    keys = [k0[...], k1[...], k2[...], k3[...], kc_ref[...]]
    vals = [v0[...], v1[...], v2[...], v3[...], vc_ref[...]]
    nj = pl.num_programs(1)
    j = pl.program_id(1)
    kind = jnp.minimum(j, 1) + jnp.maximum(j - (nj - 2), 0)
    tiles_per_row = NA_KROWS // 2
    code_base = kind * (NA_QROWS * tiles_per_row)

    def add_bias(hh, t, st):
        if t >= NA_NKB:
            return st
        rows = []
        for a in range(NA_QROWS):
            cols = []
            for u in range(NA_KT // LANES):
                code = code_ref[code_base + a * tiles_per_row + t * (NA_KT // LANES) + u]
                piece = st[a * GRID_W:(a + 1) * GRID_W, u * LANES:(u + 1) * LANES]
                cols.append(piece + tile_ref[hh, code])
            rows.append(jnp.concatenate(cols, axis=1))
        return jnp.concatenate(rows, axis=0)

    o_ref[...] = _pair_attention(q_ref[...], keys, vals, add_bias).astype(o_ref.dtype)


def _na_attention(z, tiles):
    nj = ROWS // NA_QROWS
    kblocks = SEQ // NA_KT
    qcol, kcol, vcol = 0, NA_WIDTH // LANES, 2 * NA_WIDTH // LANES
    n_codes = tiles.shape[1]

    def kbase(j):
        return jnp.clip(2 * j - 1, 0, kblocks - NA_NKB)

    def kspec(t, col):
        return pl.BlockSpec((NA_KT, LANES),
                            lambda p, j, b, code: (b * kblocks + kbase(j) + t, col + p))

    ctx_row = lambda b: R_LAT // CTX_LEN + b
    in_specs = (
        [pl.BlockSpec((NA_QT, LANES), lambda p, j, b, code: (b * nj + j, qcol + p))]
        + [kspec(t, kcol) for t in range(NA_NKB)]
        + [kspec(t, vcol) for t in range(NA_NKB)]
        + [pl.BlockSpec((CTX_LEN, LANES), lambda p, j, b, code: (ctx_row(b), kcol + p)),
           pl.BlockSpec((CTX_LEN, LANES), lambda p, j, b, code: (ctx_row(b), vcol + p)),
           pl.BlockSpec((2, n_codes, GRID_W, LANES), lambda p, j, b, code: (p, 0, 0, 0))])
    return pl.pallas_call(
        _na_kernel,
        grid_spec=pltpu.PrefetchScalarGridSpec(
            num_scalar_prefetch=1,
            grid=(NA_HEADS // 2, nj, BATCH),
            in_specs=in_specs,
            out_specs=pl.BlockSpec((NA_QT, LANES), lambda p, j, b, code: (b * nj + j, p))),
        out_shape=jax.ShapeDtypeStruct((R_LAT, NA_WIDTH), BF16),
        compiler_params=_cparams(("arbitrary", "arbitrary", "arbitrary")),
        name="na_attention",
    )(jnp.asarray(_NA_CODES), *([z] * (3 + 2 * NA_NKB)), tiles)


def _ctx_attn_kernel(q_ref, k_ref, v_ref, o_ref):
    o_ref[...] = _pair_attention(q_ref[...], [k_ref[...]], [v_ref[...]]).astype(o_ref.dtype)


def _ctx_attention(z):
    qcol, kcol, vcol = 0, NA_WIDTH // LANES, 2 * NA_WIDTH // LANES
    row = lambda b: R_LAT // CTX_LEN + b
    return pl.pallas_call(
        _ctx_attn_kernel,
        grid=(BATCH, NA_HEADS // 2),
        in_specs=[pl.BlockSpec((CTX_LEN, LANES), lambda b, p: (row(b), qcol + p)),
                  pl.BlockSpec((CTX_LEN, LANES), lambda b, p: (row(b), kcol + p)),
                  pl.BlockSpec((CTX_LEN, LANES), lambda b, p: (row(b), vcol + p))],
        out_specs=pl.BlockSpec((CTX_LEN, LANES), lambda b, p: (b, p)),
        out_shape=jax.ShapeDtypeStruct((R_CTX, NA_WIDTH), BF16),
        compiler_params=_cparams(("arbitrary", "arbitrary")),
        name="ctx_attention",
    )(z, z, z)


def _sgu_kernel(u_ref, v_ref, ng_ref, w_ref, bt_ref, o_ref):
    lane = lax.broadcasted_iota(jnp.int32, (1, LANES), 1)
    first = lane < SG_GROUP_DIM
    for ch in range(TM // SG_CHUNK):
        rows = slice(ch * SG_CHUNK, (ch + 1) * SG_CHUNK)
        u = _gelu_tanh(u_ref[rows, :].astype(F32))
        v = _gelu_tanh(v_ref[rows, :].astype(F32))
        vn = v * lax.rsqrt(jnp.mean(v * v, axis=-1, keepdims=True) + EPS) * ng_ref[...]
        vnb = vn.astype(BF16)
        for pr in range(SG_GROUPS // 2):
            cols = slice(pr * LANES, (pr + 1) * LANES)
            slab = vnb[:, cols]
            mixed = jnp.where(first, _dot(w_ref[2 * pr], slab), _dot(w_ref[2 * pr + 1], slab))
            o_ref[rows, cols] = (u[:, cols] * (mixed + bt_ref[:, cols])).astype(o_ref.dtype)


def _sgu(z, norm_g, w_bf16, bias_t):
    ucol = 3 * NA_WIDTH // SG_WIDTH
    return pl.pallas_call(
        _sgu_kernel,
        grid=(R_ALL // TM,),
        in_specs=[pl.BlockSpec((TM, SG_WIDTH), lambda i: (i, ucol)),
                  pl.BlockSpec((TM, SG_WIDTH), lambda i: (i, ucol + 1)),
                  pl.BlockSpec((1, SG_WIDTH), lambda i: (0, 0)),
                  pl.BlockSpec((SG_GROUPS, SG_CHUNK, SG_CHUNK), lambda i: (0, 0, 0)),
                  pl.BlockSpec((SG_CHUNK, SG_WIDTH), lambda i: (0, 0))],
        out_specs=pl.BlockSpec((TM, SG_WIDTH), lambda i: (i, 0)),
        out_shape=jax.ShapeDtypeStruct((R_ALL, SG_WIDTH), BF16),
        compiler_params=_cparams(("arbitrary",)),
        name="sgu",
    )(z, z, norm_g.reshape(1, SG_WIDTH), w_bf16, bias_t)


def _outproj_kernel(*refs, n_lhs):
    lhs = refs[:n_lhs]
    ws = refs[n_lhs:2 * n_lhs]
    x_ref, g1_ref, ng_ref, sh_ref, sc_ref, wr_ref, xo_ref, h2_ref, aff_ref = refs[2 * n_lhs:]
    y = _dot(lhs[0][...], ws[0][...])
    for a, w in zip(lhs[1:], ws[1:]):
        y = y + _dot(a[...], w[...])
    xn = _read_rows(x_ref) + g1_ref[...] * y
    _write_token_tiled(xo_ref, 0, xn)
    hn = xn * lax.rsqrt(jnp.mean(xn * xn, axis=-1, keepdims=True) + EPS) * ng_ref[...]
    h2 = hn * (1.0 + sc_ref[...]) + sh_ref[...]
    _write_token_tiled(h2_ref, 0, h2)
    logits = _dot_nt(wr_ref[...], h2.astype(BF16))
    e = jnp.exp(logits - logits.max(axis=0, keepdims=True))
    aff_ref[...] = e / e.sum(axis=0, keepdims=True)


def _outproj(lhs_list, w_list, xall, gate1, norm_g, shift, scale, w_router_pad, n_rows):
    n_lhs = len(lhs_list)
    grp = lambda i: (_group_of_tile(i), 0, 0)
    row = lambda i: (i, 0)
    const2 = lambda i: (0, 0)
    in_specs = (
        [pl.BlockSpec((TM, a.shape[1]), row) for a in lhs_list]
        + [pl.BlockSpec(w.shape, const2) for w in w_list]
        + [_row_spec(xall, lambda i: i),
           pl.BlockSpec((None, 1, D_MODEL), grp),
           pl.BlockSpec((1, D_MODEL), const2),
           pl.BlockSpec((None, 1, D_MODEL), grp),
           pl.BlockSpec((None, 1, D_MODEL), grp),
           pl.BlockSpec((N_EXPERTS, D_MODEL), const2)])
    tiled = (n_rows * NSUB, LANES)
    tiled_spec = pl.BlockSpec((TM * NSUB, LANES), row)
    return pl.pallas_call(
        functools.partial(_outproj_kernel, n_lhs=n_lhs),
        grid=(n_rows // TM,),
        in_specs=in_specs,
        out_specs=[tiled_spec, tiled_spec, pl.BlockSpec((N_EXPERTS, TM), lambda i: (0, i))],
        out_shape=[jax.ShapeDtypeStruct(tiled, F32),
                   jax.ShapeDtypeStruct(tiled, F32),
                   jax.ShapeDtypeStruct((N_EXPERTS, n_rows), F32)],
        compiler_params=_cparams(("arbitrary",)),
        name="outproj",
    )(*lhs_list, *w_list, xall, gate1, norm_g.reshape(1, D_MODEL), shift, scale, w_router_pad)


def _moe_segments(rows):
    cap = EC_CAPACITY * SEQ // N_EXPERTS
    segs = [(b * cap, (b + 1) * cap, b) for b in range(BATCH)]
    if rows > BATCH * cap:
        segs.append((BATCH * cap, rows, BATCH))
    return segs


class _MoeDmaPlan(NamedTuple):
    write_steps: int
    write_per_tick: int
    fetch_per_tick: int
    fetch_slots: int
    token_per_tick: int
    token_slots: int


def _moe_dma_plan(rows):
    ticks = MOE_NF * MOE_ROW_SPLIT
    write_ticks = MOE_WRITE_STEPS * MOE_ROW_SPLIT
    fetch_ticks = ticks - write_ticks
    token_ticks = ticks - MOE_ROW_SPLIT
    assert rows % write_ticks == 0 and 0 < MOE_WRITE_STEPS < MOE_NF
    fetch_per_tick = -(-rows // fetch_ticks)
    while (fetch_per_tick * fetch_ticks) % MOE_DMA_UNROLL:
        fetch_per_tick += 1
    token_per_tick = -(-rows // token_ticks)
    while (token_per_tick * token_ticks) % MOE_DMA_UNROLL:
        token_per_tick += 1
    return _MoeDmaPlan(MOE_WRITE_STEPS, rows // write_ticks, fetch_per_tick,
                       fetch_per_tick * fetch_ticks, token_per_tick, token_per_tick * token_ticks)


def _moe_kernel(idx_ref, h_hbm, x_hbm, gate_ref, g2_ref, wg_ref, wu_ref, wd_ref, xo_hbm,
                xg, xb, acc, res, wg_s, wu_s, wd_s, sem_h, sem_r, sem_s, *, rows):
    del x_hbm
    e, f = pl.program_id(0), pl.program_id(1)
    nf = pl.num_programs(1)
    plan = _moe_dma_plan(rows)

    def issue(count, first, n_valid, expert, make_copy):
        base = expert * rows
        for u in range(count):
            i = first + u
            n = idx_ref[base + jnp.minimum(i, n_valid - 1)]
            make_copy(pl.ds(pl.multiple_of(i * NSUB, NSUB), NSUB), n).start()

    def issue_all(n_slots, expert, make_copy):
        def group(o, carry):
            issue(MOE_DMA_UNROLL, o * MOE_DMA_UNROLL, rows, expert, make_copy)
            return carry

        lax.fori_loop(0, n_slots // MOE_DMA_UNROLL, group, 0)

    def token_copy(slot, n):
        return pltpu.make_async_copy(h_hbm.at[n], xg.at[slot, :], sem_h)

    def fetch_copy(slot, n):
        return pltpu.make_async_copy(xo_hbm.at[n], res.at[slot, :], sem_r)

    def writeback_copy(slot, n):
        return pltpu.make_async_copy(res.at[slot, :], xo_hbm.at[n], sem_s)

    all_tokens = pltpu.make_async_copy(xg, xg, sem_h)
    all_fetched = pltpu.make_async_copy(res, res, sem_r)
    written_rows = res.at[pl.ds(0, rows * NSUB), :]
    all_written = pltpu.make_async_copy(written_rows, written_rows, sem_s)

    prev = jnp.maximum(e - 1, 0)
    nxt = jnp.where(e + 1 < N_EXPERTS, e + 1, 0)

    @pl.when(f == 0)
    def _():
        @pl.when(e == 0)
        def _():
            issue_all(plan.token_slots, 0, token_copy)
            issue_all(plan.fetch_slots, 0, fetch_copy)

        all_tokens.wait()
        all_fetched.wait()

        @pl.when(e > 0)
        def _():
            for lo, hi, g in _moe_segments(rows):
                for r0 in range(lo, hi, MOE_UPDATE_ROWS):
                    nr = min(MOE_UPDATE_ROWS, hi - r0)
                    upd = acc[r0:r0 + nr, :] * gate_ref[r0:r0 + nr, :] * g2_ref[g]
                    _write_token_tiled(res, r0, _read_token_tiled(res, r0, nr) + upd)

        @pl.when(e < N_EXPERTS)
        def _():
            for r0 in range(0, rows, MOE_UPDATE_ROWS):
                nr = min(MOE_UPDATE_ROWS, rows - r0)
                xb[r0:r0 + nr, :] = _read_token_tiled(xg, r0, nr).astype(BF16)
            acc[...] = jnp.zeros_like(acc)

        @pl.when(e == N_EXPERTS)
        def _():
            issue_all(rows, N_EXPERTS - 1, writeback_copy)
            all_written.wait()

    @pl.when((f == plan.write_steps) & (e < N_EXPERTS))
    def _():
        all_written.wait()

    def step_body(write_back, gather):
        wg_s[...] = wg_ref[...].astype(BF16)
        wu_s[...] = wu_ref[...].astype(BF16)
        wd_s[...] = wd_ref[...].astype(BF16)
        rc = rows // MOE_ROW_SPLIT
        for r in range(MOE_ROW_SPLIT):
            sl = slice(r * rc, (r + 1) * rc)
            x = xb[sl, :]
            a = _dot(x, wg_s[...])
            b = _dot(x, wu_s[...])
            hid = (a * _sigmoid(a) * b).astype(BF16)
            acc[sl, :] += _dot(hid, wd_s[...])
            tick = f * MOE_ROW_SPLIT + r
            if write_back:
                issue(plan.write_per_tick, tick * plan.write_per_tick, rows, prev, writeback_copy)
            else:
                first = (tick - plan.write_steps * MOE_ROW_SPLIT) * plan.fetch_per_tick
                issue(plan.fetch_per_tick, first, rows, e, fetch_copy)
            if gather:
                first = (tick - MOE_ROW_SPLIT) * plan.token_per_tick
                issue(plan.token_per_tick, first, rows, nxt, token_copy)

    @pl.when((f == 0) & (e < N_EXPERTS))
    def _():
        step_body(write_back=True, gather=False)

    @pl.when((f > 0) & (f < plan.write_steps) & (e < N_EXPERTS))
    def _():
        step_body(write_back=True, gather=True)

    @pl.when((f >= plan.write_steps) & (e < N_EXPERTS))
    def _():
        step_body(write_back=False, gather=True)


def _moe_ffn(flat_idx, h2, x1, gates, gate2, w_gate, w_up, w_down, layer):
    rows = flat_idx.shape[1]
    assert rows % MOE_DMA_UNROLL == 0 and rows % MOE_ROW_SPLIT == 0
    plan = _moe_dma_plan(rows)
    nf = MOE_NF
    n_tokens = x1.shape[0] // NSUB
    as_tokens = lambda t: t.reshape(n_tokens, NSUB, LANES)
    ew = lambda e: jnp.minimum(e, N_EXPERTS - 1)
    fw = lambda e, f: jnp.where(e < N_EXPERTS, f, nf - 1)
    out = pl.pallas_call(
        functools.partial(_moe_kernel, rows=rows),
        grid_spec=pltpu.PrefetchScalarGridSpec(
            num_scalar_prefetch=1,
            grid=(N_EXPERTS + 1, nf),
            in_specs=[pl.BlockSpec(memory_space=pl.ANY),
                      pl.BlockSpec(memory_space=pl.ANY),
                      pl.BlockSpec((None, rows, 1), lambda e, f, idx: (jnp.maximum(e - 1, 0), 0, 0)),
                      pl.BlockSpec((N_GROUPS, 1, D_MODEL), lambda e, f, idx: (0, 0, 0)),
                      pl.BlockSpec((None, None, D_MODEL, MOE_TF),
                                   lambda e, f, idx: (layer, ew(e), 0, fw(e, f))),
                      pl.BlockSpec((None, None, D_MODEL, MOE_TF),
                                   lambda e, f, idx: (layer, ew(e), 0, fw(e, f))),
                      pl.BlockSpec((None, None, MOE_TF, D_MODEL),
                                   lambda e, f, idx: (layer, ew(e), fw(e, f), 0))],
            out_specs=pl.BlockSpec(memory_space=pl.ANY),
            scratch_shapes=[pltpu.VMEM((plan.token_slots * NSUB, LANES), F32),
                            pltpu.VMEM((rows, D_MODEL), BF16),
                            pltpu.VMEM((rows, D_MODEL), F32),
                            pltpu.VMEM((plan.fetch_slots * NSUB, LANES), F32),
                            pltpu.VMEM((D_MODEL, MOE_TF), BF16),
                            pltpu.VMEM((D_MODEL, MOE_TF), BF16),
                            pltpu.VMEM((MOE_TF, D_MODEL), BF16),
                            pltpu.SemaphoreType.DMA,
                            pltpu.SemaphoreType.DMA,
                            pltpu.SemaphoreType.DMA]),
        out_shape=jax.ShapeDtypeStruct((n_tokens, NSUB, LANES), F32),
        input_output_aliases={2: 0},
        compiler_params=_cparams(("arbitrary", "arbitrary")),
        name="moe_ffn",
    )(flat_idx.reshape(-1), as_tokens(h2), as_tokens(x1), gates, gate2, w_gate, w_up, w_down)
    return out.reshape(x1.shape)


def _exclusive_prefix(mask, upper):
    out = []
    run = jnp.zeros((mask.shape[0], 1), F32)
    for c in range(mask.shape[1] // ROUTE_PREFIX_BLOCK):
        m = mask[:, c * ROUTE_PREFIX_BLOCK:(c + 1) * ROUTE_PREFIX_BLOCK]
        out.append(_dot(m.astype(BF16), upper) + run)
        run = run + jnp.sum(m, axis=1, keepdims=True)
    return jnp.concatenate(out, axis=1)


def _route_kernel(aff_ref, idx_ref, gate_ref, pos3, sel3, val3, *, cap):
    n = aff_ref.shape[1]
    v = aff_ref[...]
    thr = jnp.zeros((N_EXPERTS, 1), jnp.int32)
    for bit in range(30, -1, -1):
        cand = thr | (1 << bit)
        cnt = jnp.sum((v >= pltpu.bitcast(cand, F32)).astype(jnp.int32), axis=1, keepdims=True)
        thr = jnp.where(cnt >= cap, cand, thr)
    ri = lax.broadcasted_iota(jnp.int32, (ROUTE_PREFIX_BLOCK, ROUTE_PREFIX_BLOCK), 0)
    ci = lax.broadcasted_iota(jnp.int32, (ROUTE_PREFIX_BLOCK, ROUTE_PREFIX_BLOCK), 1)
    upper = (ri < ci).astype(BF16)
    gt = v >= pltpu.bitcast(thr + 1, F32)
    eq = (v >= pltpu.bitcast(thr, F32)) & ~gt
    need = (cap - jnp.sum(gt.astype(jnp.int32), axis=1, keepdims=True)).astype(F32)
    sel = gt | (eq & (_exclusive_prefix(eq.astype(F32), upper) < need))
    sel_f = sel.astype(F32)
    pos = _exclusive_prefix(sel_f, upper)
    for e in range(N_EXPERTS):
        pos3[e] = pos[e:e + 1, :].astype(jnp.int32)
        sel3[e] = sel_f[e:e + 1, :]
        val3[e] = v[e:e + 1, :]

    tok = lax.broadcasted_iota(jnp.int32, (1, n), 1)
    tok_hi = (tok >> 7).astype(F32)
    tok_lo = (tok & (LANES - 1)).astype(F32)
    hi_row = lax.broadcasted_iota(jnp.int32, (ROUTE_SLOT_HI, 1), 0)
    lo_row = lax.broadcasted_iota(jnp.int32, (LANES, 1), 0)

    def per_expert(e, carry):
        p, s, a = pos3[e], sel3[e], val3[e]
        a1 = a.astype(BF16).astype(F32)
        a2 = (a - a1).astype(BF16).astype(F32)
        a3 = a - a1 - a2
        hot_hi = jnp.where((p >> 7) == hi_row, s, 0.0)
        hot_lo = ((p & (LANES - 1)) == lo_row).astype(BF16)
        lhs = jnp.concatenate([hot_hi * tok_hi, hot_hi * tok_lo,
                               hot_hi * a1, hot_hi * a2, hot_hi * a3], axis=0).astype(BF16)
        out = _dot_nt(lhs, hot_lo)
        h = ROUTE_SLOT_HI
        idx_ref[e] = (out[0:h] * float(LANES) + out[h:2 * h]).astype(jnp.int32)
        gate_ref[e] = (out[2 * h:3 * h] + out[3 * h:4 * h]) + out[4 * h:5 * h]
        return carry

    lax.fori_loop(0, N_EXPERTS, per_expert, 0)


def _route(aff_t, n, cap, first_block, n_sets):
    assert cap <= ROUTE_SLOT_HI * LANES and n % ROUTE_PREFIX_BLOCK == 0 and n < (1 << 14)
    shape = (n_sets, N_EXPERTS, ROUTE_SLOT_HI, LANES)
    spec = pl.BlockSpec((None, N_EXPERTS, ROUTE_SLOT_HI, LANES), lambda s: (s, 0, 0, 0))
    return pl.pallas_call(
        functools.partial(_route_kernel, cap=cap),
        grid=(n_sets,),
        in_specs=[pl.BlockSpec((N_EXPERTS, n), lambda s: (0, first_block + s))],
        out_specs=[spec, spec],
        out_shape=[jax.ShapeDtypeStruct(shape, jnp.int32), jax.ShapeDtypeStruct(shape, F32)],
        scratch_shapes=[pltpu.VMEM((N_EXPERTS, 1, n), jnp.int32),
                        pltpu.VMEM((N_EXPERTS, 1, n), F32),
                        pltpu.VMEM((N_EXPERTS, 1, n), F32)],
        compiler_params=_cparams(("arbitrary",)),
        name="route",
    )(aff_t)


def _expert_choice(h2, x1, aff_t, gate2, w_gate, w_up, w_down, layer, with_ctx):
    def slots(idx, gate, cap, row0, set_len):
        idx = idx.reshape(BATCH, N_EXPERTS, -1)[:, :, :cap]
        gate = gate.reshape(BATCH, N_EXPERTS, -1)[:, :, :cap]
        idx = idx + row0 + (jnp.arange(BATCH, dtype=jnp.int32) * set_len)[:, None, None]
        return (jnp.swapaxes(idx, 0, 1).reshape(N_EXPERTS, BATCH * cap),
                jnp.swapaxes(gate, 0, 1).reshape(N_EXPERTS, BATCH * cap))

    cap = EC_CAPACITY * SEQ // N_EXPERTS
    flat, gate = slots(*_route(aff_t, SEQ, cap, 0, BATCH), cap, 0, SEQ)
    if with_ctx:
        cap_c = EC_CAPACITY * CTX_LEN // N_EXPERTS
        flat_c, gate_c = slots(*_route(aff_t, CTX_LEN, cap_c, R_LAT // CTX_LEN, BATCH),
                               cap_c, R_LAT, CTX_LEN)
        flat = jnp.concatenate([flat, flat_c], axis=1)
        gate = jnp.concatenate([gate, gate_c], axis=1)
    return _moe_ffn(flat, h2, x1, gate[..., None], gate2, w_gate, w_up, w_down, layer)


def _final_norm_kernel(x_ref, ng_ref, o_ref):
    x = _read_rows(x_ref)
    o_ref[...] = x * lax.rsqrt(jnp.mean(x * x, axis=-1, keepdims=True) + EPS) * ng_ref[...]


def _final_norm(x, norm_g):
    n_rows = x.size // D_MODEL
    return pl.pallas_call(
        _final_norm_kernel,
        grid=(n_rows // TM,),
        in_specs=[_row_spec(x, lambda i: i),
                  pl.BlockSpec((1, D_MODEL), lambda i: (0, 0))],
        out_specs=pl.BlockSpec((TM, D_MODEL), lambda i: (i, 0)),
        out_shape=jax.ShapeDtypeStruct((n_rows, D_MODEL), F32),
        compiler_params=_cparams(("arbitrary",)),
        name="final_norm",
    )(x, norm_g.reshape(1, D_MODEL))


def _rope(t, cos, sin):
    half = RET_QK_DIM // 2
    t1, t2 = t[:, :half], t[:, half:]
    return jnp.concatenate([t1 * cos - t2 * sin, t1 * sin + t2 * cos], axis=-1)


def _retention_kernel(lg_ref, q_ref, k_ref, v_ref, cos_ref, sin_ref, kc_ref, vc_ref, *rest,
                      backward):
    if backward:
        o_ref, s_ref, dmat_ref, qd_ref, kd_ref = rest
    else:
        ob_ref, g_ref, o_ref, s_ref, dmat_ref, qd_ref, kd_ref = rest
    c = RET_CHUNK
    h = pl.program_id(1)
    lg = lg_ref[1 if backward else 0, h]
    kscale = np.float32(RET_QK_DIM ** -0.5)

    @pl.when(pl.program_id(2) == 0)
    def _():
        ia = lax.broadcasted_iota(jnp.int32, (c, c), 0)
        ib = lax.broadcasted_iota(jnp.int32, (c, c), 1)
        diff = ((ib - ia) if backward else (ia - ib)).astype(F32)
        dmat_ref[...] = jnp.where(diff >= 0, jnp.exp(jnp.maximum(diff, 0.0) * lg), 0.0)
        pos = lax.broadcasted_iota(jnp.int32, (c, RET_QK_DIM), 0).astype(F32)
        if backward:
            qd_ref[...] = jnp.exp((c - pos) * lg)
            kd_ref[...] = jnp.exp(pos * lg) * kscale
        else:
            qd_ref[...] = jnp.exp((pos + 1.0) * lg)
            kd_ref[...] = jnp.exp((c - 1.0 - pos) * lg) * kscale
        cpos = lax.broadcasted_iota(jnp.int32, (CTX_LEN, RET_QK_DIM), 0).astype(F32)
        cw = jnp.exp(cpos * lg) if backward else jnp.exp((CTX_LEN - 1.0 - cpos) * lg)
        kcw = (kc_ref[...].astype(F32) * (cw * kscale)).astype(BF16)
        s_ref[...] = _dot_tn(kcw, vc_ref[...])

    cos = cos_ref[...]
    sin = sin_ref[...]
    q = _rope(q_ref[...].astype(F32), cos, sin)
    k = _rope(k_ref[...].astype(F32), cos, sin)
    v = v_ref[...]
    qb = q.astype(BF16)
    att = _dot_nt(qb, (k * kscale).astype(BF16)) * dmat_ref[...]
    s_old = s_ref[...]
    o = _dot(att.astype(BF16), v) + _dot((q * qd_ref[...]).astype(BF16), s_old.astype(BF16))
    chunk_decay = jnp.exp(jnp.zeros((1, RET_V_DIM), F32) + c * lg)
    s_ref[...] = s_old * chunk_decay + _dot_tn((k * kd_ref[...]).astype(BF16), v)
    if backward:
        o_ref[...] = o.astype(o_ref.dtype)
    else:
        of = o + ob_ref[...].astype(F32)
        y = of * lax.rsqrt(jnp.mean(of * of, axis=-1, keepdims=True) + EPS)
        g = g_ref[...].astype(F32)
        o_ref[...] = (g * _sigmoid(g) * y).astype(o_ref.dtype)


def _retention_pass(log_gamma, z, cos, sin, o_back):
    backward = o_back is None
    c = RET_CHUNK
    n = SEQ // c
    half = RET_QK_DIM // 2
    qk_cols = RET_QK_WIDTH // RET_QK_DIM
    v_base = 2 * RET_QK_WIDTH // RET_V_DIM
    g_base = v_base + RET_V_WIDTH // RET_V_DIM
    chunk = (lambda t: n - 1 - t) if backward else (lambda t: t)
    row = lambda b, t: b * n + chunk(t)
    ctx_row = lambda b: R_LAT // CTX_LEN + b
    in_specs = [
        pl.BlockSpec((c, RET_QK_DIM), lambda b, h, t, lg: (row(b, t), h)),
        pl.BlockSpec((c, RET_QK_DIM), lambda b, h, t, lg: (row(b, t), qk_cols + h)),
        pl.BlockSpec((c, RET_V_DIM), lambda b, h, t, lg: (row(b, t), v_base + h)),
        pl.BlockSpec((c, half), lambda b, h, t, lg: (chunk(t), 0)),
        pl.BlockSpec((c, half), lambda b, h, t, lg: (chunk(t), 0)),
        pl.BlockSpec((CTX_LEN, RET_QK_DIM), lambda b, h, t, lg: (ctx_row(b), qk_cols + h)),
        pl.BlockSpec((CTX_LEN, RET_V_DIM), lambda b, h, t, lg: (ctx_row(b), v_base + h)),
    ]
    args = [z, z, z, cos, sin, z, z]
    if not backward:
        in_specs += [
            pl.BlockSpec((c, RET_V_DIM), lambda b, h, t, lg: (row(b, t), h)),
            pl.BlockSpec((c, RET_V_DIM), lambda b, h, t, lg: (row(b, t), g_base + h)),
        ]
        args += [o_back, z]
    return pl.pallas_call(
        functools.partial(_retention_kernel, backward=backward),
        grid_spec=pltpu.PrefetchScalarGridSpec(
            num_scalar_prefetch=1,
            grid=(BATCH, RET_HEADS, n),
            in_specs=in_specs,
            out_specs=pl.BlockSpec((c, RET_V_DIM), lambda b, h, t, lg: (row(b, t), h)),
            scratch_shapes=[pltpu.VMEM((RET_QK_DIM, RET_V_DIM), F32),
                            pltpu.VMEM((c, c), F32),
                            pltpu.VMEM((c, RET_QK_DIM), F32),
                            pltpu.VMEM((c, RET_QK_DIM), F32)]),
        out_shape=jax.ShapeDtypeStruct((R_LAT, RET_V_WIDTH), BF16),
        compiler_params=_cparams(("arbitrary", "arbitrary", "arbitrary")),
        name="retention_bwd" if backward else "retention_fwd",
    )(log_gamma, *args)


def _rope_tables():
    axis_dim = RET_QK_DIM // 4
    inv = 1.0 / (ROPE_BASE ** (jnp.arange(0, 2 * axis_dim, 2, dtype=F32) / (2 * axis_dim)))
    t = jnp.arange(SEQ)
    r = (t // GRID_W).astype(F32)
    col = (t % GRID_W).astype(F32)
    ang = jnp.concatenate([r[:, None] * inv, col[:, None] * inv], axis=-1)
    return jnp.cos(ang), jnp.sin(ang)


def _split_mod(mod_layer):
    m = mod_layer[:N_GROUPS].reshape(N_GROUPS, 6, 1, D_MODEL)
    return [m[:, k] for k in range(6)]


def _router_t(w_router):
    return w_router.T.astype(BF16)


def kernel(x, c, ctx, c_ctx, ada_w, ada_b, norm1_g, norm2_g, ab_w_in, ab_w_out, na_rpb, sgu_norm_g, sgu_w, sgu_b, ret_w_in, ret_w_out, ret_decay_logit, moe_router, moe_w_gate, moe_w_up, moe_w_down, final_norm_g):
    assert DEPTH == 2 and x.shape == (BATCH, SEQ, D_MODEL) and ctx.shape == (BATCH, CTX_LEN, D_MODEL)

    cvec = jnp.zeros((8, D_MODEL), F32).at[:BATCH].set(c).at[BATCH].set(c_ctx)
    mod = _adaln(cvec, ada_w, ada_b)
    xall = jnp.concatenate([x.reshape(R_LAT, D_MODEL), ctx.reshape(R_CTX, D_MODEL)], axis=0)

    sh1, sc1, g1, sh2, sc2, g2 = _split_mod(mod[0])
    z = _inproj(xall, norm1_g[0], sh1, sc1, ab_w_in[0].astype(BF16), tn=AB_IN)
    a_all = jnp.concatenate([_na_attention(z, _na_bias_tiles(na_rpb[0])), _ctx_attention(z)], axis=0)
    bias_t = jnp.repeat(sgu_b[0].T, SG_GROUP_DIM, axis=1)
    bsg = _sgu(z, sgu_norm_g[0], sgu_w[0].astype(BF16), bias_t)
    w_out = ab_w_out[0].astype(BF16)
    x1, h2, aff = _outproj([a_all, bsg], [w_out[:NA_WIDTH], w_out[NA_WIDTH:]], xall, g1,
                           norm2_g[0], sh2, sc2, _router_t(moe_router[0]), R_ALL)
    xall = _expert_choice(h2, x1, aff, g2, moe_w_gate, moe_w_up, moe_w_down, 0, with_ctx=True)

    sh1, sc1, g1, sh2, sc2, g2 = _split_mod(mod[1])
    z = _inproj(xall, norm1_g[1], sh1, sc1, ret_w_in[0].astype(BF16), tn=2048)
    log_gamma = jax.nn.log_sigmoid(ret_decay_logit[0].astype(F32))
    cos, sin = _rope_tables()
    o_back = _retention_pass(log_gamma, z, cos, sin, None)
    ypre = _retention_pass(log_gamma, z, cos, sin, o_back)
    x1, h2, aff = _outproj([ypre], [ret_w_out[0].astype(BF16)], xall, g1,
                           norm2_g[1], sh2, sc2, _router_t(moe_router[1]), R_LAT)
    x2 = _expert_choice(h2, x1, aff, g2, moe_w_gate, moe_w_up, moe_w_down, 1, with_ctx=False)
    return _final_norm(x2, final_norm_g).reshape(BATCH, SEQ, D_MODEL)
```

```python
import functools
from typing import NamedTuple

import numpy as np
import jax
import jax.numpy as jnp
from jax import lax
from jax.experimental import pallas as pl
from jax.experimental.pallas import tpu as pltpu

F32 = jnp.float32
BF16 = jnp.bfloat16

D_MODEL = 1024
BATCH = 2
SEQ = 8192
DEPTH = 2
GRID_W = 64
CTX_LEN = 256
EPS = 1e-6
NEG_INF = -1e30

NA_HEADS = 8
NA_HEAD_DIM = 64
NA_WIN_H = 8
NA_WIN_W = 16
NA_WIDTH = NA_HEADS * NA_HEAD_DIM
SG_GROUPS = 8
SG_GROUP_DIM = 64
SG_CHUNK = 128
SG_WIDTH = SG_GROUPS * SG_GROUP_DIM
AB_IN = 3 * NA_WIDTH + 2 * SG_WIDTH

RET_HEADS = 4
RET_QK_DIM = 256
RET_V_DIM = 512
RET_QK_WIDTH = RET_HEADS * RET_QK_DIM
RET_V_WIDTH = RET_HEADS * RET_V_DIM
RET_IN = 2 * RET_QK_WIDTH + 2 * RET_V_WIDTH
ROPE_BASE = 10000.0

N_EXPERTS = 16
EC_CAPACITY = 2
D_FF_EXPERT = 2816

ROWS = SEQ // GRID_W
R_LAT = BATCH * SEQ
R_CTX = BATCH * CTX_LEN
R_ALL = R_LAT + R_CTX
N_GROUPS = BATCH + 1

LANES = 128
TM = 512
VMEM_LIMIT = 56 * 1024 * 1024

NA_QROWS = 8
NA_KROWS = 16
NA_QT = NA_QROWS * GRID_W
NA_KT = 256
NA_NKB = NA_KROWS * GRID_W // NA_KT
RET_CHUNK = 256
RET_HEADS_PER_STEP = 2
MOE_TF = 256
MOE_NF = D_FF_EXPERT // MOE_TF
MOE_ROW_SPLIT = 4
MOE_WRITE_STEPS = 4
MOE_DMA_UNROLL = 8
MOE_UPDATE_ROWS = 256
ROUTE_PREFIX_BLOCK = 256
ROUTE_SLOT_HI = 8


def _group_of_tile(i):
    return jnp.minimum(i // (SEQ // TM), BATCH)


def _cparams(sem, vmem=VMEM_LIMIT):
    return pltpu.CompilerParams(dimension_semantics=sem, vmem_limit_bytes=vmem)


def _dot(a, b):
    return jnp.dot(a, b, preferred_element_type=F32)


def _dot_nt(a, b):
    return lax.dot_general(a, b, (((1,), (1,)), ((), ())), preferred_element_type=F32)


def _dot_tn(a, b):
    return lax.dot_general(a, b, (((0,), (0,)), ((), ())), preferred_element_type=F32)


def _sigmoid(x):
    return 1.0 / (1.0 + jnp.exp(-x))


def _gelu_tanh(x):
    c = np.float32(np.sqrt(2.0 / np.pi))
    return 0.5 * x * (1.0 + jnp.tanh(c * (x + np.float32(0.044715) * (x * x * x))))


def _adaln_kernel(c_ref, w_ref, b_ref, o_ref):
    cv = c_ref[...]
    s = (cv * _sigmoid(cv)).astype(BF16)
    o_ref[...] = _dot(s, w_ref[...].astype(BF16)) + b_ref[...]


def _adaln(cvec, ada_w, ada_b):
    tn = 1024
    n = 6 * D_MODEL
    return pl.pallas_call(
        _adaln_kernel,
        grid=(DEPTH, n // tn),
        in_specs=[
            pl.BlockSpec((8, D_MODEL), lambda l, j: (0, 0)),
            pl.BlockSpec((None, D_MODEL, tn), lambda l, j: (l, 0, j)),
            pl.BlockSpec((None, 1, tn), lambda l, j: (l, 0, j)),
        ],
        out_specs=pl.BlockSpec((None, 8, tn), lambda l, j: (l, 0, j)),
        out_shape=jax.ShapeDtypeStruct((DEPTH, 8, n), F32),
        compiler_params=_cparams(("arbitrary", "arbitrary")),
        name="adaln",
    )(cvec, ada_w, ada_b.reshape(DEPTH, 1, n))


NSUB = D_MODEL // LANES


def _read_token_tiled(ref, start, rows):
    return jnp.concatenate(
        [ref[pl.ds(start * NSUB + s, rows, stride=NSUB), :] for s in range(NSUB)], axis=-1)


def _write_token_tiled(ref, start, val):
    for s in range(NSUB):
        ref[pl.ds(start * NSUB + s, val.shape[0], stride=NSUB), :] = val[:, s * LANES:(s + 1) * LANES]


def _read_rows(x_ref):
    if x_ref.shape[-1] == D_MODEL:
        return x_ref[...]
    return _read_token_tiled(x_ref, 0, x_ref.shape[0] // NSUB)


def _row_spec(x, index_map):
    if x.shape[-1] == D_MODEL:
        return pl.BlockSpec((TM, D_MODEL), lambda *a: (index_map(*a), 0))
    return pl.BlockSpec((TM * NSUB, LANES), lambda *a: (index_map(*a), 0))


def _inproj_kernel(x_ref, g_ref, sh_ref, sc_ref, w_ref, z_ref, hb_ref):
    @pl.when(pl.program_id(1) == 0)
    def _():
        x = _read_rows(x_ref)
        y = x * lax.rsqrt(jnp.mean(x * x, axis=-1, keepdims=True) + EPS) * g_ref[...]
        hb_ref[...] = (y * (1.0 + sc_ref[...]) + sh_ref[...]).astype(BF16)

    z_ref[...] = _dot(hb_ref[...], w_ref[...]).astype(z_ref.dtype)


def _inproj(xall, norm_g, shift, scale, w_bf16, tn):
    n = w_bf16.shape[1]
    grp = lambda i, j: (_group_of_tile(i), 0, 0)
    return pl.pallas_call(
        _inproj_kernel,
        grid=(R_ALL // TM, n // tn),
        in_specs=[
            _row_spec(xall, lambda i, j: i),
            pl.BlockSpec((1, D_MODEL), lambda i, j: (0, 0)),
            pl.BlockSpec((None, 1, D_MODEL), grp),
            pl.BlockSpec((None, 1, D_MODEL), grp),
            pl.BlockSpec((D_MODEL, tn), lambda i, j: (0, j)),
        ],
        out_specs=pl.BlockSpec((TM, tn), lambda i, j: (i, j)),
        out_shape=jax.ShapeDtypeStruct((R_ALL, n), BF16),
        scratch_shapes=[pltpu.VMEM((TM, D_MODEL), BF16)],
        compiler_params=_cparams(("arbitrary", "arbitrary")),
        name="inproj",
    )(xall, norm_g.reshape(1, D_MODEL), shift, scale, w_bf16)


def _rope(t, cos, sin):
    half = RET_QK_DIM // 2
    t1, t2 = t[:, :half], t[:, half:]
    return jnp.concatenate([t1 * cos - t2 * sin, t1 * sin + t2 * cos], axis=-1)


def _inproj_ret_kernel(x_ref, g_ref, sh_ref, sc_ref, cos_ref, sin_ref, w_ref, z_ref):
    x = _read_rows(x_ref)
    y = x * lax.rsqrt(jnp.mean(x * x, axis=-1, keepdims=True) + EPS) * g_ref[...]
    hb = (y * (1.0 + sc_ref[...]) + sh_ref[...]).astype(BF16)
    cos, sin = cos_ref[...], sin_ref[...]
    kscale = np.float32(RET_QK_DIM ** -0.5)
    for h in range(2 * RET_HEADS):
        cols = slice(h * RET_QK_DIM, (h + 1) * RET_QK_DIM)
        t = _rope(_dot(hb, w_ref[:, cols]), cos, sin)
        z_ref[:, cols] = (t if h < RET_HEADS else t * kscale).astype(z_ref.dtype)
    rest = slice(2 * RET_QK_WIDTH, RET_IN)
    z_ref[:, rest] = _dot(hb, w_ref[:, rest]).astype(z_ref.dtype)


def _inproj_ret(xall, norm_g, shift, scale, w_bf16, cos, sin):
    grp = lambda i: (_group_of_tile(i), 0, 0)
    lat_tiles = SEQ // TM
    pos = lambda i: (jnp.where(i < BATCH * lat_tiles, i % lat_tiles, lat_tiles), 0)
    half = RET_QK_DIM // 2
    return pl.pallas_call(
        _inproj_ret_kernel,
        grid=(R_ALL // TM,),
        in_specs=[
            _row_spec(xall, lambda i: i),
            pl.BlockSpec((1, D_MODEL), lambda i: (0, 0)),
            pl.BlockSpec((None, 1, D_MODEL), grp),
            pl.BlockSpec((None, 1, D_MODEL), grp),
            pl.BlockSpec((TM, half), pos),
            pl.BlockSpec((TM, half), pos),
            pl.BlockSpec((D_MODEL, RET_IN), lambda i: (0, 0)),
        ],
        out_specs=pl.BlockSpec((TM, RET_IN), lambda i: (i, 0)),
        out_shape=jax.ShapeDtypeStruct((R_ALL, RET_IN), BF16),
        compiler_params=_cparams(("arbitrary",)),
        name="inproj_ret",
    )(xall, norm_g.reshape(1, D_MODEL), shift, scale, cos, sin, w_bf16)


def _na_tile_codes():
    masked = 2 * NA_WIN_H - 1
    kinds = (0, 1, ROWS // NA_QROWS - 1)
    d = np.full((3, NA_QROWS, NA_KROWS), masked, np.int64)
    for v, j in enumerate(kinds):
        kb = int(np.clip(NA_QROWS * j - NA_WIN_H // 2, 0, ROWS - NA_KROWS))
        for a in range(NA_QROWS):
            i = NA_QROWS * j + a
            r0 = int(np.clip(i - NA_WIN_H // 2, 0, ROWS - NA_WIN_H))
            for r in range(NA_KROWS):
                if r0 <= kb + r < r0 + NA_WIN_H:
                    d[v, a, r] = kb + r - i + NA_WIN_H - 1
    pairs = d.reshape(-1, 2)
    uniq = sorted(set(map(tuple, pairs)))
    code = np.array([uniq.index(tuple(p)) for p in pairs], np.int32)
    return code, np.array(uniq, np.int64)


_NA_CODES, _NA_CODE_ROWS = _na_tile_codes()


def _na_bias_tiles(rpb):
    qc = np.arange(GRID_W)[:, None]
    kc = np.arange(GRID_W)[None, :]
    cstart = np.clip(qc - NA_WIN_W // 2, 0, GRID_W - NA_WIN_W)
    col_ok = (kc >= cstart) & (kc < cstart + NA_WIN_W)
    dcol = np.clip(kc - qc, 1 - NA_WIN_W, NA_WIN_W - 1) + NA_WIN_W - 1
    onehot = (dcol.reshape(1, -1) == np.arange(2 * NA_WIN_W - 1)[:, None]).astype(np.float32)
    toe = jnp.einsum('hdm,mq->hdq', rpb.astype(F32), jnp.asarray(onehot),
                     precision=lax.Precision.HIGHEST)
    toe = toe.reshape(NA_HEADS, 2 * NA_WIN_H - 1, GRID_W, GRID_W)
    toe = jnp.where(jnp.asarray(col_ok), toe, NEG_INF)
    toe = jnp.concatenate([toe, jnp.full((NA_HEADS, 1, GRID_W, GRID_W), NEG_INF, F32)], axis=1)
    return jnp.concatenate([toe[:, _NA_CODE_ROWS[:, 0]], toe[:, _NA_CODE_ROWS[:, 1]]], axis=-1)


def _pair_attention(q, keys, vals, add_bias=None):
    lane = lax.broadcasted_iota(jnp.int32, (1, LANES), 1)
    scale = NA_HEAD_DIM ** -0.5
    out = jnp.zeros((q.shape[0], LANES), F32)
    for hh in range(2):
        in_head = (lane < NA_HEAD_DIM) if hh == 0 else (lane >= NA_HEAD_DIM)
        qm = jnp.where(in_head, q, jnp.zeros_like(q)) * jnp.asarray(scale, q.dtype)
        s = []
        for t, k in enumerate(keys):
            st = _dot_nt(qm, k)
            if add_bias is not None:
                st = add_bias(hh, t, st)
            s.append(st)
        m = s[0].max(axis=-1, keepdims=True)
        for st in s[1:]:
            m = jnp.maximum(m, st.max(axis=-1, keepdims=True))
        den = jnp.zeros_like(m)
        acc = jnp.zeros((q.shape[0], LANES), F32)
        for st, v in zip(s, vals):
            p = jnp.exp(st - m)
            den = den + p.sum(axis=-1, keepdims=True)
            acc = acc + _dot(p.astype(BF16), v)
        out = jnp.where(in_head, acc / den, out)
    return out


def _na_kernel(code_ref, q_ref, k0, k1, k2, k3, v0, v1, v2, v3, kc_ref, vc_ref, tile_ref, o_ref):
    keys = [k0[...], k1[...], k2[...], k3[...], kc_ref[...]]
    vals = [v0[...], v1[...], v2[...], v3[...], vc_ref[...]]
    nj = pl.num_programs(1)
    j = pl.program_id(1)
    kind = jnp.minimum(j, 1) + jnp.maximum(j - (nj - 2), 0)
    tiles_per_row = NA_KROWS // 2
    code_base = kind * (NA_QROWS * tiles_per_row)

    def add_bias(hh, t, st):
        if t >= NA_NKB:
            return st
        rows = []
        for a in range(NA_QROWS):
            cols = []
            for u in range(NA_KT // LANES):
                code = code_ref[code_base + a * tiles_per_row + t * (NA_KT // LANES) + u]
                piece = st[a * GRID_W:(a + 1) * GRID_W, u * LANES:(u + 1) * LANES]
                cols.append(piece + tile_ref[hh, code])
            rows.append(jnp.concatenate(cols, axis=1))
        return jnp.concatenate(rows, axis=0)

    o_ref[...] = _pair_attention(q_ref[...], keys, vals, add_bias).astype(o_ref.dtype)


def _na_attention(z, tiles):
    nj = ROWS // NA_QROWS
    kblocks = SEQ // NA_KT
    qcol, kcol, vcol = 0, NA_WIDTH // LANES, 2 * NA_WIDTH // LANES
    n_codes = tiles.shape[1]

    def kbase(j):
        return jnp.clip(2 * j - 1, 0, kblocks - NA_NKB)

    def kspec(t, col):
        return pl.BlockSpec((NA_KT, LANES),
                            lambda p, j, b, code: (b * kblocks + kbase(j) + t, col + p))

    ctx_row = lambda b: R_LAT // CTX_LEN + b
    in_specs = (
        [pl.BlockSpec((NA_QT, LANES), lambda p, j, b, code: (b * nj + j, qcol + p))]
        + [kspec(t, kcol) for t in range(NA_NKB)]
        + [kspec(t, vcol) for t in range(NA_NKB)]
        + [pl.BlockSpec((CTX_LEN, LANES), lambda p, j, b, code: (ctx_row(b), kcol + p)),
           pl.BlockSpec((CTX_LEN, LANES), lambda p, j, b, code: (ctx_row(b), vcol + p)),
           pl.BlockSpec((2, n_codes, GRID_W, LANES), lambda p, j, b, code: (p, 0, 0, 0))])
    return pl.pallas_call(
        _na_kernel,
        grid_spec=pltpu.PrefetchScalarGridSpec(
            num_scalar_prefetch=1,
            grid=(NA_HEADS // 2, nj, BATCH),
            in_specs=in_specs,
            out_specs=pl.BlockSpec((NA_QT, LANES), lambda p, j, b, code: (b * nj + j, p))),
        out_shape=jax.ShapeDtypeStruct((R_LAT, NA_WIDTH), BF16),
        compiler_params=_cparams(("arbitrary", "arbitrary", "arbitrary")),
        name="na_attention",
    )(jnp.asarray(_NA_CODES), *([z] * (3 + 2 * NA_NKB)), tiles)


def _ctx_attn_kernel(q_ref, k_ref, v_ref, o_ref):
    o_ref[...] = _pair_attention(q_ref[...], [k_ref[...]], [v_ref[...]]).astype(o_ref.dtype)


def _ctx_attention(z):
    qcol, kcol, vcol = 0, NA_WIDTH // LANES, 2 * NA_WIDTH // LANES
    row = lambda b: R_LAT // CTX_LEN + b
    return pl.pallas_call(
        _ctx_attn_kernel,
        grid=(BATCH, NA_HEADS // 2),
        in_specs=[pl.BlockSpec((CTX_LEN, LANES), lambda b, p: (row(b), qcol + p)),
                  pl.BlockSpec((CTX_LEN, LANES), lambda b, p: (row(b), kcol + p)),
                  pl.BlockSpec((CTX_LEN, LANES), lambda b, p: (row(b), vcol + p))],
        out_specs=pl.BlockSpec((CTX_LEN, LANES), lambda b, p: (b, p)),
        out_shape=jax.ShapeDtypeStruct((R_CTX, NA_WIDTH), BF16),
        compiler_params=_cparams(("arbitrary", "arbitrary")),
        name="ctx_attention",
    )(z, z, z)


def _sgu_kernel(u_ref, v_ref, ng_ref, w_ref, bt_ref, o_ref):
    lane = lax.broadcasted_iota(jnp.int32, (1, LANES), 1)
    first = lane < SG_GROUP_DIM
    for ch in range(TM // SG_CHUNK):
        rows = slice(ch * SG_CHUNK, (ch + 1) * SG_CHUNK)
        u = _gelu_tanh(u_ref[rows, :].astype(F32))
        v = _gelu_tanh(v_ref[rows, :].astype(F32))
        vn = v * lax.rsqrt(jnp.mean(v * v, axis=-1, keepdims=True) + EPS) * ng_ref[...]
        vnb = vn.astype(BF16)
        for pr in range(SG_GROUPS // 2):
            cols = slice(pr * LANES, (pr + 1) * LANES)
            slab = vnb[:, cols]
            mixed = jnp.where(first, _dot(w_ref[2 * pr], slab), _dot(w_ref[2 * pr + 1], slab))
            o_ref[rows, cols] = (u[:, cols] * (mixed + bt_ref[:, cols])).astype(o_ref.dtype)


def _sgu(z, norm_g, w_bf16, bias_t):
    ucol = 3 * NA_WIDTH // SG_WIDTH
    return pl.pallas_call(
        _sgu_kernel,
        grid=(R_ALL // TM,),
        in_specs=[pl.BlockSpec((TM, SG_WIDTH), lambda i: (i, ucol)),
                  pl.BlockSpec((TM, SG_WIDTH), lambda i: (i, ucol + 1)),
                  pl.BlockSpec((1, SG_WIDTH), lambda i: (0, 0)),
                  pl.BlockSpec((SG_GROUPS, SG_CHUNK, SG_CHUNK), lambda i: (0, 0, 0)),
                  pl.BlockSpec((SG_CHUNK, SG_WIDTH), lambda i: (0, 0))],
        out_specs=pl.BlockSpec((TM, SG_WIDTH), lambda i: (i, 0)),
        out_shape=jax.ShapeDtypeStruct((R_ALL, SG_WIDTH), BF16),
        compiler_params=_cparams(("arbitrary",)),
        name="sgu",
    )(z, z, norm_g.reshape(1, SG_WIDTH), w_bf16, bias_t)


def _outproj_kernel(*refs, n_lhs):
    lhs = refs[:n_lhs]
    ws = refs[n_lhs:2 * n_lhs]
    x_ref, g1_ref, ng_ref, sh_ref, sc_ref, wr_ref, xo_ref, h2_ref, aff_ref = refs[2 * n_lhs:]
    y = _dot(lhs[0][...], ws[0][...])
    for a, w in zip(lhs[1:], ws[1:]):
        y = y + _dot(a[...], w[...])
    xn = _read_rows(x_ref) + g1_ref[...] * y
    _write_token_tiled(xo_ref, 0, xn)
    hn = xn * lax.rsqrt(jnp.mean(xn * xn, axis=-1, keepdims=True) + EPS) * ng_ref[...]
    h2 = hn * (1.0 + sc_ref[...]) + sh_ref[...]
    _write_token_tiled(h2_ref, 0, h2)
    logits = _dot_nt(wr_ref[...], h2.astype(BF16))
    e = jnp.exp(logits - logits.max(axis=0, keepdims=True))
    aff_ref[...] = e / e.sum(axis=0, keepdims=True)


def _outproj(lhs_list, w_list, xall, gate1, norm_g, shift, scale, w_router_pad, n_rows):
    n_lhs = len(lhs_list)
    grp = lambda i: (_group_of_tile(i), 0, 0)
    row = lambda i: (i, 0)
    const2 = lambda i: (0, 0)
    in_specs = (
        [pl.BlockSpec((TM, a.shape[1]), row) for a in lhs_list]
        + [pl.BlockSpec(w.shape, const2) for w in w_list]
        + [_row_spec(xall, lambda i: i),
           pl.BlockSpec((None, 1, D_MODEL), grp),
           pl.BlockSpec((1, D_MODEL), const2),
           pl.BlockSpec((None, 1, D_MODEL), grp),
           pl.BlockSpec((None, 1, D_MODEL), grp),
           pl.BlockSpec((N_EXPERTS, D_MODEL), const2)])
    tiled = (n_rows * NSUB, LANES)
    tiled_spec = pl.BlockSpec((TM * NSUB, LANES), row)
    return pl.pallas_call(
        functools.partial(_outproj_kernel, n_lhs=n_lhs),
        grid=(n_rows // TM,),
        in_specs=in_specs,
        out_specs=[tiled_spec, tiled_spec, pl.BlockSpec((N_EXPERTS, TM), lambda i: (0, i))],
        out_shape=[jax.ShapeDtypeStruct(tiled, F32),
                   jax.ShapeDtypeStruct(tiled, F32),
                   jax.ShapeDtypeStruct((N_EXPERTS, n_rows), F32)],
        compiler_params=_cparams(("arbitrary",)),
        name="outproj",
    )(*lhs_list, *w_list, xall, gate1, norm_g.reshape(1, D_MODEL), shift, scale, w_router_pad)


def _moe_segments(rows):
    cap = EC_CAPACITY * SEQ // N_EXPERTS
    segs = [(b * cap, (b + 1) * cap, b) for b in range(BATCH)]
    if rows > BATCH * cap:
        segs.append((BATCH * cap, rows, BATCH))
    return segs


class _MoeDmaPlan(NamedTuple):
    write_steps: int
    write_per_tick: int
    fetch_per_tick: int
    fetch_slots: int
    token_per_tick: int
    token_slots: int


def _moe_dma_plan(rows):
    ticks = MOE_NF * MOE_ROW_SPLIT
    write_ticks = MOE_WRITE_STEPS * MOE_ROW_SPLIT
    fetch_ticks = ticks - write_ticks
    token_ticks = ticks - MOE_ROW_SPLIT
    assert rows % write_ticks == 0 and 0 < MOE_WRITE_STEPS < MOE_NF
    fetch_per_tick = -(-rows // fetch_ticks)
    while (fetch_per_tick * fetch_ticks) % MOE_DMA_UNROLL:
        fetch_per_tick += 1
    token_per_tick = -(-rows // token_ticks)
    while (token_per_tick * token_ticks) % MOE_DMA_UNROLL:
        token_per_tick += 1
    return _MoeDmaPlan(MOE_WRITE_STEPS, rows // write_ticks, fetch_per_tick,
                       fetch_per_tick * fetch_ticks, token_per_tick, token_per_tick * token_ticks)


def _moe_kernel(idx_ref, h_hbm, x_hbm, gate_ref, g2_ref, wg_ref, wu_ref, wd_ref, xo_hbm,
                xg, xb, acc, res, wg_s, wu_s, wd_s, sem_h, sem_r, sem_s, *, rows):
    del x_hbm
    e, f = pl.program_id(0), pl.program_id(1)
    nf = pl.num_programs(1)
    plan = _moe_dma_plan(rows)

    def issue(count, first, n_valid, expert, make_copy):
        base = expert * rows
        for u in range(count):
            i = first + u
            n = idx_ref[base + jnp.minimum(i, n_valid - 1)]
            make_copy(pl.ds(pl.multiple_of(i * NSUB, NSUB), NSUB), n).start(priority=1)

    def issue_all(n_slots, expert, make_copy):
        def group(o, carry):
            issue(MOE_DMA_UNROLL, o * MOE_DMA_UNROLL, rows, expert, make_copy)
            return carry

        lax.fori_loop(0, n_slots // MOE_DMA_UNROLL, group, 0)

    def token_copy(slot, n):
        return pltpu.make_async_copy(h_hbm.at[n], xg.at[slot, :], sem_h)

    def fetch_copy(slot, n):
        return pltpu.make_async_copy(xo_hbm.at[n], res.at[slot, :], sem_r)

    def writeback_copy(slot, n):
        return pltpu.make_async_copy(res.at[slot, :], xo_hbm.at[n], sem_s)

    all_tokens = pltpu.make_async_copy(xg, xg, sem_h)
    all_fetched = pltpu.make_async_copy(res, res, sem_r)
    written_rows = res.at[pl.ds(0, rows * NSUB), :]
    all_written = pltpu.make_async_copy(written_rows, written_rows, sem_s)

    prev = jnp.maximum(e - 1, 0)
    nxt = jnp.where(e + 1 < N_EXPERTS, e + 1, 0)

    @pl.when(f == 0)
    def _():
        @pl.when(e == 0)
        def _():
            issue_all(plan.token_slots, 0, token_copy)
            issue_all(plan.fetch_slots, 0, fetch_copy)

        all_tokens.wait()
        all_fetched.wait()

        @pl.when(e > 0)
        def _():
            for lo, hi, g in _moe_segments(rows):
                for r0 in range(lo, hi, MOE_UPDATE_ROWS):
                    nr = min(MOE_UPDATE_ROWS, hi - r0)
                    upd = acc[r0:r0 + nr, :] * gate_ref[r0:r0 + nr, :] * g2_ref[g]
                    _write_token_tiled(res, r0, _read_token_tiled(res, r0, nr) + upd)

        @pl.when(e < N_EXPERTS)
        def _():
            for r0 in range(0, rows, MOE_UPDATE_ROWS):
                nr = min(MOE_UPDATE_ROWS, rows - r0)
                xb[r0:r0 + nr, :] = _read_token_tiled(xg, r0, nr).astype(BF16)
            acc[...] = jnp.zeros_like(acc)

        @pl.when(e == N_EXPERTS)
        def _():
            issue_all(rows, N_EXPERTS - 1, writeback_copy)
            all_written.wait()

    @pl.when((f == plan.write_steps) & (e < N_EXPERTS))
    def _():
        all_written.wait()

    def step_body(write_back, gather):
        wg_s[...] = wg_ref[...].astype(BF16)
        wu_s[...] = wu_ref[...].astype(BF16)
        wd_s[...] = wd_ref[...].astype(BF16)
        rc = rows // MOE_ROW_SPLIT
        for r in range(MOE_ROW_SPLIT):
            sl = slice(r * rc, (r + 1) * rc)
            x = xb[sl, :]
            a = _dot(x, wg_s[...])
            b = _dot(x, wu_s[...])
            hid = (a * _sigmoid(a) * b).astype(BF16)
            acc[sl, :] += _dot(hid, wd_s[...])
            tick = f * MOE_ROW_SPLIT + r
            if write_back:
                issue(plan.write_per_tick, tick * plan.write_per_tick, rows, prev, writeback_copy)
            else:
                first = (tick - plan.write_steps * MOE_ROW_SPLIT) * plan.fetch_per_tick
                issue(plan.fetch_per_tick, first, rows, e, fetch_copy)
            if gather:
                first = (tick - MOE_ROW_SPLIT) * plan.token_per_tick
                issue(plan.token_per_tick, first, rows, nxt, token_copy)

    @pl.when((f == 0) & (e < N_EXPERTS))
    def _():
        step_body(write_back=True, gather=False)

    @pl.when((f > 0) & (f < plan.write_steps) & (e < N_EXPERTS))
    def _():
        step_body(write_back=True, gather=True)

    @pl.when((f >= plan.write_steps) & (e < N_EXPERTS))
    def _():
        step_body(write_back=False, gather=True)


def _moe_ffn(flat_idx, h2, x1, gates, gate2, w_gate, w_up, w_down, layer):
    rows = flat_idx.shape[1]
    assert rows % MOE_DMA_UNROLL == 0 and rows % MOE_ROW_SPLIT == 0
    plan = _moe_dma_plan(rows)
    nf = MOE_NF
    n_tokens = x1.shape[0] // NSUB
    as_tokens = lambda t: t.reshape(n_tokens, NSUB, LANES)
    ew = lambda e: jnp.minimum(e, N_EXPERTS - 1)
    fw = lambda e, f: jnp.where(e < N_EXPERTS, f, nf - 1)
    out = pl.pallas_call(
        functools.partial(_moe_kernel, rows=rows),
        grid_spec=pltpu.PrefetchScalarGridSpec(
            num_scalar_prefetch=1,
            grid=(N_EXPERTS + 1, nf),
            in_specs=[pl.BlockSpec(memory_space=pl.ANY),
                      pl.BlockSpec(memory_space=pl.ANY),
                      pl.BlockSpec((None, rows, 1), lambda e, f, idx: (jnp.maximum(e - 1, 0), 0, 0)),
                      pl.BlockSpec((N_GROUPS, 1, D_MODEL), lambda e, f, idx: (0, 0, 0)),
                      pl.BlockSpec((None, None, D_MODEL, MOE_TF),
                                   lambda e, f, idx: (layer, ew(e), 0, fw(e, f))),
                      pl.BlockSpec((None, None, D_MODEL, MOE_TF),
                                   lambda e, f, idx: (layer, ew(e), 0, fw(e, f))),
                      pl.BlockSpec((None, None, MOE_TF, D_MODEL),
                                   lambda e, f, idx: (layer, ew(e), fw(e, f), 0))],
            out_specs=pl.BlockSpec(memory_space=pl.ANY),
            scratch_shapes=[pltpu.VMEM((plan.token_slots * NSUB, LANES), F32),
                            pltpu.VMEM((rows, D_MODEL), BF16),
                            pltpu.VMEM((rows, D_MODEL), F32),
                            pltpu.VMEM((plan.fetch_slots * NSUB, LANES), F32),
                            pltpu.VMEM((D_MODEL, MOE_TF), BF16),
                            pltpu.VMEM((D_MODEL, MOE_TF), BF16),
                            pltpu.VMEM((MOE_TF, D_MODEL), BF16),
                            pltpu.SemaphoreType.DMA,
                            pltpu.SemaphoreType.DMA,
                            pltpu.SemaphoreType.DMA]),
        out_shape=jax.ShapeDtypeStruct((n_tokens, NSUB, LANES), F32),
        input_output_aliases={2: 0},
        compiler_params=_cparams(("arbitrary", "arbitrary")),
        name="moe_ffn",
    )(flat_idx.reshape(-1), as_tokens(h2), as_tokens(x1), gates, gate2, w_gate, w_up, w_down)
    return out.reshape(x1.shape)


def _exclusive_prefix(mask, upper):
    out = []
    run = jnp.zeros((mask.shape[0], 1), F32)
    for c in range(mask.shape[1] // ROUTE_PREFIX_BLOCK):
        m = mask[:, c * ROUTE_PREFIX_BLOCK:(c + 1) * ROUTE_PREFIX_BLOCK]
        out.append(_dot(m.astype(BF16), upper) + run)
        run = run + jnp.sum(m, axis=1, keepdims=True)
    return jnp.concatenate(out, axis=1)


def _route_kernel(aff_ref, idx_ref, gate_ref, pos3, sel3, val3, *, cap):
    n = aff_ref.shape[1]
    v = aff_ref[...]
    thr = jnp.zeros((N_EXPERTS, 1), jnp.int32)
    for bit in range(30, -1, -1):
        cand = thr | (1 << bit)
        cnt = jnp.sum((v >= pltpu.bitcast(cand, F32)).astype(jnp.int32), axis=1, keepdims=True)
        thr = jnp.where(cnt >= cap, cand, thr)
    ri = lax.broadcasted_iota(jnp.int32, (ROUTE_PREFIX_BLOCK, ROUTE_PREFIX_BLOCK), 0)
    ci = lax.broadcasted_iota(jnp.int32, (ROUTE_PREFIX_BLOCK, ROUTE_PREFIX_BLOCK), 1)
    upper = (ri < ci).astype(BF16)
    gt = v >= pltpu.bitcast(thr + 1, F32)
    eq = (v >= pltpu.bitcast(thr, F32)) & ~gt
    need = (cap - jnp.sum(gt.astype(jnp.int32), axis=1, keepdims=True)).astype(F32)
    sel = gt | (eq & (_exclusive_prefix(eq.astype(F32), upper) < need))
    sel_f = sel.astype(F32)
    pos = _exclusive_prefix(sel_f, upper)
    for e in range(N_EXPERTS):
        pos3[e] = pos[e:e + 1, :].astype(jnp.int32)
        sel3[e] = sel_f[e:e + 1, :]
        val3[e] = v[e:e + 1, :]

    tok = lax.broadcasted_iota(jnp.int32, (1, n), 1)
    tok_hi = (tok >> 7).astype(F32)
    tok_lo = (tok & (LANES - 1)).astype(F32)
    hi_row = lax.broadcasted_iota(jnp.int32, (ROUTE_SLOT_HI, 1), 0)
    lo_row = lax.broadcasted_iota(jnp.int32, (LANES, 1), 0)

    def per_expert(e, carry):
        p, s, a = pos3[e], sel3[e], val3[e]
        a1 = a.astype(BF16).astype(F32)
        a2 = (a - a1).astype(BF16).astype(F32)
        a3 = a - a1 - a2
        hot_hi = jnp.where((p >> 7) == hi_row, s, 0.0)
        hot_lo = ((p & (LANES - 1)) == lo_row).astype(BF16)
        lhs = jnp.concatenate([hot_hi * tok_hi, hot_hi * tok_lo,
                               hot_hi * a1, hot_hi * a2, hot_hi * a3], axis=0).astype(BF16)
        out = _dot_nt(lhs, hot_lo)
        h = ROUTE_SLOT_HI
        idx_ref[e] = (out[0:h] * float(LANES) + out[h:2 * h]).astype(jnp.int32)
        gate_ref[e] = (out[2 * h:3 * h] + out[3 * h:4 * h]) + out[4 * h:5 * h]
        return carry

    lax.fori_loop(0, N_EXPERTS, per_expert, 0)


def _route(aff_t, n, cap, first_block, n_sets):
    assert cap <= ROUTE_SLOT_HI * LANES and n % ROUTE_PREFIX_BLOCK == 0 and n < (1 << 14)
    shape = (n_sets, N_EXPERTS, ROUTE_SLOT_HI, LANES)
    spec = pl.BlockSpec((None, N_EXPERTS, ROUTE_SLOT_HI, LANES), lambda s: (s, 0, 0, 0))
    return pl.pallas_call(
        functools.partial(_route_kernel, cap=cap),
        grid=(n_sets,),
        in_specs=[pl.BlockSpec((N_EXPERTS, n), lambda s: (0, first_block + s))],
        out_specs=[spec, spec],
        out_shape=[jax.ShapeDtypeStruct(shape, jnp.int32), jax.ShapeDtypeStruct(shape, F32)],
        scratch_shapes=[pltpu.VMEM((N_EXPERTS, 1, n), jnp.int32),
                        pltpu.VMEM((N_EXPERTS, 1, n), F32),
                        pltpu.VMEM((N_EXPERTS, 1, n), F32)],
        compiler_params=_cparams(("arbitrary",)),
        name="route",
    )(aff_t)


def _expert_choice(h2, x1, aff_t, gate2, w_gate, w_up, w_down, layer, with_ctx):
    def slots(idx, gate, cap, row0, set_len):
        idx = idx.reshape(BATCH, N_EXPERTS, -1)[:, :, :cap]
        gate = gate.reshape(BATCH, N_EXPERTS, -1)[:, :, :cap]
        idx = idx + row0 + (jnp.arange(BATCH, dtype=jnp.int32) * set_len)[:, None, None]
        return (jnp.swapaxes(idx, 0, 1).reshape(N_EXPERTS, BATCH * cap),
                jnp.swapaxes(gate, 0, 1).reshape(N_EXPERTS, BATCH * cap))

    cap = EC_CAPACITY * SEQ // N_EXPERTS
    flat, gate = slots(*_route(aff_t, SEQ, cap, 0, BATCH), cap, 0, SEQ)
    if with_ctx:
        cap_c = EC_CAPACITY * CTX_LEN // N_EXPERTS
        flat_c, gate_c = slots(*_route(aff_t, CTX_LEN, cap_c, R_LAT // CTX_LEN, BATCH),
                               cap_c, R_LAT, CTX_LEN)
        flat = jnp.concatenate([flat, flat_c], axis=1)
        gate = jnp.concatenate([gate, gate_c], axis=1)
    return _moe_ffn(flat, h2, x1, gate[..., None], gate2, w_gate, w_up, w_down, layer)


def _final_norm_kernel(x_ref, ng_ref, o_ref):
    x = _read_rows(x_ref)
    o_ref[...] = x * lax.rsqrt(jnp.mean(x * x, axis=-1, keepdims=True) + EPS) * ng_ref[...]


def _final_norm(x, norm_g):
    n_rows = x.size // D_MODEL
    return pl.pallas_call(
        _final_norm_kernel,
        grid=(n_rows // TM,),
        in_specs=[_row_spec(x, lambda i: i),
                  pl.BlockSpec((1, D_MODEL), lambda i: (0, 0))],
        out_specs=pl.BlockSpec((TM, D_MODEL), lambda i: (i, 0)),
        out_shape=jax.ShapeDtypeStruct((n_rows, D_MODEL), F32),
        compiler_params=_cparams(("arbitrary",)),
        name="final_norm",
    )(x, norm_g.reshape(1, D_MODEL))


def _retention_kernel(lg_ref, q_ref, k_ref, v_ref, kc_ref, vc_ref, *rest, backward):
    if backward:
        o_ref, s_ref, dmat_ref, qd_ref, kd_ref = rest
    else:
        ob_ref, g_ref, o_ref, s_ref, dmat_ref, qd_ref, kd_ref = rest
    c = RET_CHUNK
    for hh in range(RET_HEADS_PER_STEP):
        lg = lg_ref[1 if backward else 0, pl.program_id(1) * RET_HEADS_PER_STEP + hh]
        qk = slice(hh * RET_QK_DIM, (hh + 1) * RET_QK_DIM)
        vv = slice(hh * RET_V_DIM, (hh + 1) * RET_V_DIM)

        @pl.when(pl.program_id(2) == 0)
        def _():
            ia = lax.broadcasted_iota(jnp.int32, (c, c), 0)
            ib = lax.broadcasted_iota(jnp.int32, (c, c), 1)
            diff = ((ib - ia) if backward else (ia - ib)).astype(F32)
            dmat_ref[hh] = jnp.where(diff >= 0, jnp.exp(jnp.maximum(diff, 0.0) * lg), 0.0)
            pos = lax.broadcasted_iota(jnp.int32, (c, RET_QK_DIM), 0).astype(F32)
            if backward:
                qd_ref[hh] = jnp.exp((c - pos) * lg)
                kd_ref[hh] = jnp.exp(pos * lg)
            else:
                qd_ref[hh] = jnp.exp((pos + 1.0) * lg)
                kd_ref[hh] = jnp.exp((c - 1.0 - pos) * lg)
            cpos = lax.broadcasted_iota(jnp.int32, (CTX_LEN, RET_QK_DIM), 0).astype(F32)
            cw = jnp.exp(cpos * lg) if backward else jnp.exp((CTX_LEN - 1.0 - cpos) * lg)
            kcw = (kc_ref[:, qk].astype(F32) * cw).astype(BF16)
            s_ref[hh] = _dot_tn(kcw, vc_ref[:, vv])

        qb = q_ref[:, qk]
        kb = k_ref[:, qk]
        v = v_ref[:, vv]
        att = _dot_nt(qb, kb) * dmat_ref[hh]
        s_old = s_ref[hh]
        o = (_dot(att.astype(BF16), v)
             + _dot((qb.astype(F32) * qd_ref[hh]).astype(BF16), s_old.astype(BF16)))
        chunk_decay = jnp.exp(jnp.zeros((1, RET_V_DIM), F32) + c * lg)
        s_ref[hh] = s_old * chunk_decay + _dot_tn((kb.astype(F32) * kd_ref[hh]).astype(BF16), v)
        if backward:
            o_ref[:, vv] = o.astype(o_ref.dtype)
        else:
            of = o + ob_ref[:, vv].astype(F32)
            y = of * lax.rsqrt(jnp.mean(of * of, axis=-1, keepdims=True) + EPS)
            g = g_ref[:, vv].astype(F32)
            o_ref[:, vv] = (g * _sigmoid(g) * y).astype(o_ref.dtype)


def _retention_pass(log_gamma, z, o_back):
    backward = o_back is None
    c = RET_CHUNK
    n = SEQ // c
    hp = RET_HEADS_PER_STEP
    qk_w, v_w = hp * RET_QK_DIM, hp * RET_V_DIM
    qk_cols = RET_QK_WIDTH // qk_w
    v_base = 2 * RET_QK_WIDTH // v_w
    g_base = v_base + RET_V_WIDTH // v_w
    chunk = (lambda t: n - 1 - t) if backward else (lambda t: t)
    row = lambda b, t: b * n + chunk(t)
    ctx_row = lambda b: R_LAT // CTX_LEN + b
    in_specs = [
        pl.BlockSpec((c, qk_w), lambda b, h, t, lg: (row(b, t), h)),
        pl.BlockSpec((c, qk_w), lambda b, h, t, lg: (row(b, t), qk_cols + h)),
        pl.BlockSpec((c, v_w), lambda b, h, t, lg: (row(b, t), v_base + h)),
        pl.BlockSpec((CTX_LEN, qk_w), lambda b, h, t, lg: (ctx_row(b), qk_cols + h)),
        pl.BlockSpec((CTX_LEN, v_w), lambda b, h, t, lg: (ctx_row(b), v_base + h)),
    ]
    args = [z, z, z, z, z]
    if not backward:
        in_specs += [
            pl.BlockSpec((c, v_w), lambda b, h, t, lg: (row(b, t), h)),
            pl.BlockSpec((c, v_w), lambda b, h, t, lg: (row(b, t), g_base + h)),
        ]
        args += [o_back, z]
    return pl.pallas_call(
        functools.partial(_retention_kernel, backward=backward),
        grid_spec=pltpu.PrefetchScalarGridSpec(
            num_scalar_prefetch=1,
            grid=(BATCH, RET_HEADS // hp, n),
            in_specs=in_specs,
            out_specs=pl.BlockSpec((c, v_w), lambda b, h, t, lg: (row(b, t), h)),
            scratch_shapes=[pltpu.VMEM((hp, RET_QK_DIM, RET_V_DIM), F32),
                            pltpu.VMEM((hp, c, c), F32),
                            pltpu.VMEM((hp, c, RET_QK_DIM), F32),
                            pltpu.VMEM((hp, c, RET_QK_DIM), F32)]),
        out_shape=jax.ShapeDtypeStruct((R_LAT, RET_V_WIDTH), BF16),
        compiler_params=_cparams(("arbitrary", "arbitrary", "arbitrary")),
        name="retention_bwd" if backward else "retention_fwd",
    )(log_gamma, *args)


def _rope_tables():
    axis_dim = RET_QK_DIM // 4
    inv = 1.0 / (ROPE_BASE ** (jnp.arange(0, 2 * axis_dim, 2, dtype=F32) / (2 * axis_dim)))
    t = jnp.arange(SEQ)
    r = (t // GRID_W).astype(F32)
    col = (t % GRID_W).astype(F32)
    ang = jnp.concatenate([r[:, None] * inv, col[:, None] * inv], axis=-1)
    ang = jnp.concatenate([ang, jnp.zeros((TM, ang.shape[1]), F32)], axis=0)
    return jnp.cos(ang), jnp.sin(ang)


def _split_mod(mod_layer):
    m = mod_layer[:N_GROUPS].reshape(N_GROUPS, 6, 1, D_MODEL)
    return [m[:, k] for k in range(6)]


def _router_t(w_router):
    return w_router.T.astype(BF16)


def kernel(x, c, ctx, c_ctx, ada_w, ada_b, norm1_g, norm2_g, ab_w_in, ab_w_out, na_rpb, sgu_norm_g, sgu_w, sgu_b, ret_w_in, ret_w_out, ret_decay_logit, moe_router, moe_w_gate, moe_w_up, moe_w_down, final_norm_g):
    assert DEPTH == 2 and x.shape == (BATCH, SEQ, D_MODEL) and ctx.shape == (BATCH, CTX_LEN, D_MODEL)

    cvec = jnp.zeros((8, D_MODEL), F32).at[:BATCH].set(c).at[BATCH].set(c_ctx)
    mod = _adaln(cvec, ada_w, ada_b)
    xall = jnp.concatenate([x.reshape(R_LAT, D_MODEL), ctx.reshape(R_CTX, D_MODEL)], axis=0)

    sh1, sc1, g1, sh2, sc2, g2 = _split_mod(mod[0])
    z = _inproj(xall, norm1_g[0], sh1, sc1, ab_w_in[0].astype(BF16), tn=AB_IN)
    a_all = jnp.concatenate([_na_attention(z, _na_bias_tiles(na_rpb[0])), _ctx_attention(z)], axis=0)
    bias_t = jnp.repeat(sgu_b[0].T, SG_GROUP_DIM, axis=1)
    bsg = _sgu(z, sgu_norm_g[0], sgu_w[0].astype(BF16), bias_t)
    w_out = ab_w_out[0].astype(BF16)
    x1, h2, aff = _outproj([a_all, bsg], [w_out[:NA_WIDTH], w_out[NA_WIDTH:]], xall, g1,
                           norm2_g[0], sh2, sc2, _router_t(moe_router[0]), R_ALL)
    xall = _expert_choice(h2, x1, aff, g2, moe_w_gate, moe_w_up, moe_w_down, 0, with_ctx=True)

    sh1, sc1, g1, sh2, sc2, g2 = _split_mod(mod[1])
    z = _inproj_ret(xall, norm1_g[1], sh1, sc1, ret_w_in[0].astype(BF16), *_rope_tables())
    log_gamma = jax.nn.log_sigmoid(ret_decay_logit[0].astype(F32))
    o_back = _retention_pass(log_gamma, z, None)
    ypre = _retention_pass(log_gamma, z, o_back)
    x1, h2, aff = _outproj([ypre], [ret_w_out[0].astype(BF16)], xall, g1,
                           norm2_g[1], sh2, sc2, _router_t(moe_router[1]), R_LAT)
    x2 = _expert_choice(h2, x1, aff, g2, moe_w_gate, moe_w_up, moe_w_down, 1, with_ctx=False)
    return _final_norm(x2, final_norm_g).reshape(BATCH, SEQ, D_MODEL)
```

```python
import functools
from typing import NamedTuple

import numpy as np
import jax
import jax.numpy as jnp
from jax import lax
from jax.experimental import pallas as pl
from jax.experimental.pallas import tpu as pltpu

F32 = jnp.float32
BF16 = jnp.bfloat16

D_MODEL = 1024
BATCH = 2
SEQ = 8192
DEPTH = 2
GRID_W = 64
CTX_LEN = 256
EPS = 1e-6
NEG_INF = -1e30

NA_HEADS = 8
NA_HEAD_DIM = 64
NA_WIN_H = 8
NA_WIN_W = 16
NA_WIDTH = NA_HEADS * NA_HEAD_DIM
SG_GROUPS = 8
SG_GROUP_DIM = 64
SG_CHUNK = 128
SG_WIDTH = SG_GROUPS * SG_GROUP_DIM
AB_IN = 3 * NA_WIDTH + 2 * SG_WIDTH

RET_HEADS = 4
RET_QK_DIM = 256
RET_V_DIM = 512
RET_QK_WIDTH = RET_HEADS * RET_QK_DIM
RET_V_WIDTH = RET_HEADS * RET_V_DIM
RET_IN = 2 * RET_QK_WIDTH + 2 * RET_V_WIDTH
ROPE_BASE = 10000.0

N_EXPERTS = 16
EC_CAPACITY = 2
D_FF_EXPERT = 2816

ROWS = SEQ // GRID_W
R_LAT = BATCH * SEQ
R_CTX = BATCH * CTX_LEN
R_ALL = R_LAT + R_CTX
N_GROUPS = BATCH + 1

LANES = 128
TM = 512
VMEM_LIMIT = 56 * 1024 * 1024

NA_QROWS = 8
NA_KROWS = 16
NA_QT = NA_QROWS * GRID_W
NA_KT = 256
NA_NKB = NA_KROWS * GRID_W // NA_KT
NA_PAIRS_PER_STEP = 2
RET_CHUNK = 256
RET_HEADS_PER_STEP = 4
MOE_TF = 256
MOE_NF = D_FF_EXPERT // MOE_TF
MOE_ROW_SPLIT = 4
MOE_WRITE_STEPS = 4
MOE_DMA_UNROLL = 8
MOE_UPDATE_ROWS = 256
ROUTE_PREFIX_BLOCK = 256
ROUTE_SLOT_HI = 8


def _group_of_tile(i):
    return jnp.minimum(i // (SEQ // TM), BATCH)


def _cparams(sem, vmem=VMEM_LIMIT):
    return pltpu.CompilerParams(dimension_semantics=sem, vmem_limit_bytes=vmem)


def _dot(a, b):
    return jnp.dot(a, b, preferred_element_type=F32)


def _dot_nt(a, b):
    return lax.dot_general(a, b, (((1,), (1,)), ((), ())), preferred_element_type=F32)


def _dot_tn(a, b):
    return lax.dot_general(a, b, (((0,), (0,)), ((), ())), preferred_element_type=F32)


def _sigmoid(x):
    return 1.0 / (1.0 + jnp.exp(-x))


def _gelu_tanh(x):
    c = np.float32(np.sqrt(2.0 / np.pi))
    return 0.5 * x * (1.0 + jnp.tanh(c * (x + np.float32(0.044715) * (x * x * x))))


def _adaln_kernel(c_ref, w_ref, b_ref, o_ref):
    cv = c_ref[...]
    s = (cv * _sigmoid(cv)).astype(BF16)
    o_ref[...] = _dot(s, w_ref[...].astype(BF16)) + b_ref[...]


def _adaln(cvec, ada_w, ada_b):
    tn = 1024
    n = 6 * D_MODEL
    return pl.pallas_call(
        _adaln_kernel,
        grid=(DEPTH, n // tn),
        in_specs=[
            pl.BlockSpec((8, D_MODEL), lambda l, j: (0, 0)),
            pl.BlockSpec((None, D_MODEL, tn), lambda l, j: (l, 0, j)),
            pl.BlockSpec((None, 1, tn), lambda l, j: (l, 0, j)),
        ],
        out_specs=pl.BlockSpec((None, 8, tn), lambda l, j: (l, 0, j)),
        out_shape=jax.ShapeDtypeStruct((DEPTH, 8, n), F32),
        compiler_params=_cparams(("arbitrary", "arbitrary")),
        name="adaln",
    )(cvec, ada_w, ada_b.reshape(DEPTH, 1, n))


NSUB = D_MODEL // LANES


def _read_token_tiled(ref, start, rows):
    return jnp.concatenate(
        [ref[pl.ds(start * NSUB + s, rows, stride=NSUB), :] for s in range(NSUB)], axis=-1)


def _write_token_tiled(ref, start, val):
    for s in range(NSUB):
        ref[pl.ds(start * NSUB + s, val.shape[0], stride=NSUB), :] = val[:, s * LANES:(s + 1) * LANES]


def _read_rows(x_ref):
    if x_ref.shape[-1] != LANES:
        return x_ref[...]
    return _read_token_tiled(x_ref, 0, x_ref.shape[0] // NSUB)


def _row_spec(x, index_map):
    if x.shape[-1] != LANES:
        return pl.BlockSpec((TM, x.shape[-1]), lambda *a: (index_map(*a), 0))
    return pl.BlockSpec((TM * NSUB, LANES), lambda *a: (index_map(*a), 0))


def _row_specs(x, index_map):
    if not isinstance(x, tuple):
        return [_row_spec(x, index_map)]
    lat, ctx = x
    lat_tiles = lat.shape[0] // TM
    assert lat.shape[0] == R_LAT and ctx.shape[0] == TM and lat.shape[1] == ctx.shape[1]
    width = lat.shape[1]
    return [pl.BlockSpec((TM, width), lambda *a: (jnp.minimum(index_map(*a), lat_tiles - 1), 0)),
            pl.BlockSpec((TM, width), lambda *a: (0, 0))]


def _read_row_operand(refs, tile):
    if len(refs) == 1:
        return _read_rows(refs[0])
    lat_ref, ctx_ref = refs
    return jnp.where(tile < R_LAT // TM, lat_ref[...], ctx_ref[...])


def _inproj_kernel(*refs, n_x):
    x_refs = refs[:n_x]
    g_ref, sh_ref, sc_ref, w_ref, z_ref, hb_ref = refs[n_x:]

    @pl.when(pl.program_id(1) == 0)
    def _():
        x = _read_row_operand(x_refs, pl.program_id(0))
        y = x * lax.rsqrt(jnp.mean(x * x, axis=-1, keepdims=True) + EPS) * g_ref[...]
        hb_ref[...] = (y * (1.0 + sc_ref[...]) + sh_ref[...]).astype(BF16)

    z_ref[...] = _dot(hb_ref[...], w_ref[...]).astype(z_ref.dtype)


def _inproj(xall, norm_g, shift, scale, w_bf16, tn):
    n = w_bf16.shape[1]
    grp = lambda i, j: (_group_of_tile(i), 0, 0)
    x_specs = _row_specs(xall, lambda i, j: i)
    x_args = xall if isinstance(xall, tuple) else (xall,)
    return pl.pallas_call(
        functools.partial(_inproj_kernel, n_x=len(x_specs)),
        grid=(R_ALL // TM, n // tn),
        in_specs=x_specs + [
            pl.BlockSpec((1, D_MODEL), lambda i, j: (0, 0)),
            pl.BlockSpec((None, 1, D_MODEL), grp),
            pl.BlockSpec((None, 1, D_MODEL), grp),
            pl.BlockSpec((D_MODEL, tn), lambda i, j: (0, j)),
        ],
        out_specs=pl.BlockSpec((TM, tn), lambda i, j: (i, j)),
        out_shape=jax.ShapeDtypeStruct((R_ALL, n), BF16),
        scratch_shapes=[pltpu.VMEM((TM, D_MODEL), BF16)],
        compiler_params=_cparams(("arbitrary", "arbitrary")),
        name="inproj",
    )(*x_args, norm_g.reshape(1, D_MODEL), shift, scale, w_bf16)


def _rope(t, cos, sin):
    half = RET_QK_DIM // 2
    t1, t2 = t[:, :half], t[:, half:]
    return jnp.concatenate([t1 * cos - t2 * sin, t1 * sin + t2 * cos], axis=-1)


def _inproj_ret_kernel(x_ref, g_ref, sh_ref, sc_ref, cos_ref, sin_ref, w_ref, z_ref):
    x = _read_rows(x_ref)
    y = x * lax.rsqrt(jnp.mean(x * x, axis=-1, keepdims=True) + EPS) * g_ref[...]
    hb = (y * (1.0 + sc_ref[...]) + sh_ref[...]).astype(BF16)
    cos, sin = cos_ref[...], sin_ref[...]
    kscale = np.float32(RET_QK_DIM ** -0.5)
    for h in range(2 * RET_HEADS):
        cols = slice(h * RET_QK_DIM, (h + 1) * RET_QK_DIM)
        t = _rope(_dot(hb, w_ref[:, cols]), cos, sin)
        z_ref[:, cols] = (t if h < RET_HEADS else t * kscale).astype(z_ref.dtype)
    rest = slice(2 * RET_QK_WIDTH, RET_IN)
    z_ref[:, rest] = _dot(hb, w_ref[:, rest]).astype(z_ref.dtype)


def _inproj_ret(xall, norm_g, shift, scale, w_bf16, cos, sin):
    grp = lambda i: (_group_of_tile(i), 0, 0)
    lat_tiles = SEQ // TM
    pos = lambda i: (jnp.where(i < BATCH * lat_tiles, i % lat_tiles, lat_tiles), 0)
    half = RET_QK_DIM // 2
    return pl.pallas_call(
        _inproj_ret_kernel,
        grid=(R_ALL // TM,),
        in_specs=[
            _row_spec(xall, lambda i: i),
            pl.BlockSpec((1, D_MODEL), lambda i: (0, 0)),
            pl.BlockSpec((None, 1, D_MODEL), grp),
            pl.BlockSpec((None, 1, D_MODEL), grp),
            pl.BlockSpec((TM, half), pos),
            pl.BlockSpec((TM, half), pos),
            pl.BlockSpec((D_MODEL, RET_IN), lambda i: (0, 0)),
        ],
        out_specs=pl.BlockSpec((TM, RET_IN), lambda i: (i, 0)),
        out_shape=jax.ShapeDtypeStruct((R_ALL, RET_IN), BF16),
        compiler_params=_cparams(("arbitrary",)),
        name="inproj_ret",
    )(xall, norm_g.reshape(1, D_MODEL), shift, scale, cos, sin, w_bf16)


def _na_tile_codes():
    masked = 2 * NA_WIN_H - 1
    kinds = (0, 1, ROWS // NA_QROWS - 1)
    d = np.full((3, NA_QROWS, NA_KROWS), masked, np.int64)
    for v, j in enumerate(kinds):
        kb = int(np.clip(NA_QROWS * j - NA_WIN_H // 2, 0, ROWS - NA_KROWS))
        for a in range(NA_QROWS):
            i = NA_QROWS * j + a
            r0 = int(np.clip(i - NA_WIN_H // 2, 0, ROWS - NA_WIN_H))
            for r in range(NA_KROWS):
                if r0 <= kb + r < r0 + NA_WIN_H:
                    d[v, a, r] = kb + r - i + NA_WIN_H - 1
    pairs = d.reshape(-1, 2)
    uniq = sorted(set(map(tuple, pairs)))
    code = np.array([uniq.index(tuple(p)) for p in pairs], np.int32)
    return code, np.array(uniq, np.int64)


_NA_CODES, _NA_CODE_ROWS = _na_tile_codes()


def _na_bias_tiles(rpb):
    qc = np.arange(GRID_W)[:, None]
    kc = np.arange(GRID_W)[None, :]
    cstart = np.clip(qc - NA_WIN_W // 2, 0, GRID_W - NA_WIN_W)
    col_ok = (kc >= cstart) & (kc < cstart + NA_WIN_W)
    dcol = np.clip(kc - qc, 1 - NA_WIN_W, NA_WIN_W - 1) + NA_WIN_W - 1
    onehot = (dcol.reshape(1, -1) == np.arange(2 * NA_WIN_W - 1)[:, None]).astype(np.float32)
    toe = jnp.einsum('hdm,mq->hdq', rpb.astype(F32), jnp.asarray(onehot),
                     precision=lax.Precision.HIGHEST)
    toe = toe.reshape(NA_HEADS, 2 * NA_WIN_H - 1, GRID_W, GRID_W)
    toe = jnp.where(jnp.asarray(col_ok), toe, NEG_INF)
    toe = jnp.concatenate([toe, jnp.full((NA_HEADS, 1, GRID_W, GRID_W), NEG_INF, F32)], axis=1)
    return jnp.concatenate([toe[:, _NA_CODE_ROWS[:, 0]], toe[:, _NA_CODE_ROWS[:, 1]]], axis=-1)


def _pair_attention(q, keys, vals, add_bias=None):
    lane = lax.broadcasted_iota(jnp.int32, (1, LANES), 1)
    scale = NA_HEAD_DIM ** -0.5
    out = jnp.zeros((q.shape[0], LANES), F32)
    for hh in range(2):
        in_head = (lane < NA_HEAD_DIM) if hh == 0 else (lane >= NA_HEAD_DIM)
        qm = jnp.where(in_head, q, jnp.zeros_like(q)) * jnp.asarray(scale, q.dtype)
        s = []
        for t, k in enumerate(keys):
            st = _dot_nt(qm, k)
            if add_bias is not None:
                st = add_bias(hh, t, st)
            s.append(st)
        m = s[0].max(axis=-1, keepdims=True)
        for st in s[1:]:
            m = jnp.maximum(m, st.max(axis=-1, keepdims=True))
        den = jnp.zeros_like(m)
        acc = jnp.zeros((q.shape[0], LANES), F32)
        for st, v in zip(s, vals):
            p = jnp.exp(st - m)
            den = den + p.sum(axis=-1, keepdims=True)
            acc = acc + _dot(p.astype(BF16), v)
        out = jnp.where(in_head, acc / den, out)
    return out


def _na_kernel(code_ref, q_ref, k0, k1, k2, k3, v0, v1, v2, v3, kc_ref, vc_ref, tile_ref, o_ref):
    nj = pl.num_programs(1)
    j = pl.program_id(1)
    kind = jnp.minimum(j, 1) + jnp.maximum(j - (nj - 2), 0)
    tiles_per_row = NA_KROWS // 2
    code_base = kind * (NA_QROWS * tiles_per_row)
    for pp in range(NA_PAIRS_PER_STEP):
        lanes = slice(pp * LANES, (pp + 1) * LANES)
        keys = [r[:, lanes] for r in (k0, k1, k2, k3, kc_ref)]
        vals = [r[:, lanes] for r in (v0, v1, v2, v3, vc_ref)]

        def add_bias(hh, t, st, pp=pp):
            if t >= NA_NKB:
                return st
            rows = []
            for a in range(NA_QROWS):
                cols = []
                for u in range(NA_KT // LANES):
                    code = code_ref[code_base + a * tiles_per_row + t * (NA_KT // LANES) + u]
                    piece = st[a * GRID_W:(a + 1) * GRID_W, u * LANES:(u + 1) * LANES]
                    cols.append(piece + tile_ref[2 * pp + hh, code])
                rows.append(jnp.concatenate(cols, axis=1))
            return jnp.concatenate(rows, axis=0)

        o_ref[:, lanes] = _pair_attention(q_ref[:, lanes], keys, vals, add_bias).astype(o_ref.dtype)


def _na_attention(z, tiles):
    nj = ROWS // NA_QROWS
    kblocks = SEQ // NA_KT
    width = NA_PAIRS_PER_STEP * LANES
    qcol, kcol, vcol = 0, NA_WIDTH // width, 2 * NA_WIDTH // width
    n_codes = tiles.shape[1]

    def kbase(j):
        return jnp.clip(2 * j - 1, 0, kblocks - NA_NKB)

    def kspec(t, col):
        return pl.BlockSpec((NA_KT, width),
                            lambda p, j, b, code: (b * kblocks + kbase(j) + t, col + p))

    ctx_row = lambda b: R_LAT // CTX_LEN + b
    in_specs = (
        [pl.BlockSpec((NA_QT, width), lambda p, j, b, code: (b * nj + j, qcol + p))]
        + [kspec(t, kcol) for t in range(NA_NKB)]
        + [kspec(t, vcol) for t in range(NA_NKB)]
        + [pl.BlockSpec((CTX_LEN, width), lambda p, j, b, code: (ctx_row(b), kcol + p)),
           pl.BlockSpec((CTX_LEN, width), lambda p, j, b, code: (ctx_row(b), vcol + p)),
           pl.BlockSpec((2 * NA_PAIRS_PER_STEP, n_codes, GRID_W, LANES),
                        lambda p, j, b, code: (p, 0, 0, 0))])
    return pl.pallas_call(
        _na_kernel,
        grid_spec=pltpu.PrefetchScalarGridSpec(
            num_scalar_prefetch=1,
            grid=(NA_HEADS // (2 * NA_PAIRS_PER_STEP), nj, BATCH),
            in_specs=in_specs,
            out_specs=pl.BlockSpec((NA_QT, width), lambda p, j, b, code: (b * nj + j, p))),
        out_shape=jax.ShapeDtypeStruct((R_LAT, NA_WIDTH), BF16),
        compiler_params=_cparams(("arbitrary", "arbitrary", "arbitrary")),
        name="na_attention",
    )(jnp.asarray(_NA_CODES), *([z] * (3 + 2 * NA_NKB)), tiles)


def _ctx_attn_kernel(q_ref, k_ref, v_ref, o_ref):
    o_ref[...] = _pair_attention(q_ref[...], [k_ref[...]], [v_ref[...]]).astype(o_ref.dtype)


def _ctx_attention(z):
    qcol, kcol, vcol = 0, NA_WIDTH // LANES, 2 * NA_WIDTH // LANES
    row = lambda b: R_LAT // CTX_LEN + b
    return pl.pallas_call(
        _ctx_attn_kernel,
        grid=(BATCH, NA_HEADS // 2),
        in_specs=[pl.BlockSpec((CTX_LEN, LANES), lambda b, p: (row(b), qcol + p)),
                  pl.BlockSpec((CTX_LEN, LANES), lambda b, p: (row(b), kcol + p)),
                  pl.BlockSpec((CTX_LEN, LANES), lambda b, p: (row(b), vcol + p))],
        out_specs=pl.BlockSpec((CTX_LEN, LANES), lambda b, p: (b, p)),
        out_shape=jax.ShapeDtypeStruct((R_CTX, NA_WIDTH), BF16),
        compiler_params=_cparams(("arbitrary", "arbitrary")),
        name="ctx_attention",
    )(z, z, z)


def _sgu_kernel(u_ref, v_ref, ng_ref, w_ref, bt_ref, o_ref):
    lane = lax.broadcasted_iota(jnp.int32, (1, LANES), 1)
    first = lane < SG_GROUP_DIM
    for ch in range(TM // SG_CHUNK):
        rows = slice(ch * SG_CHUNK, (ch + 1) * SG_CHUNK)
        u = _gelu_tanh(u_ref[rows, :].astype(F32))
        v = _gelu_tanh(v_ref[rows, :].astype(F32))
        vn = v * lax.rsqrt(jnp.mean(v * v, axis=-1, keepdims=True) + EPS) * ng_ref[...]
        vnb = vn.astype(BF16)
        for pr in range(SG_GROUPS // 2):
            cols = slice(pr * LANES, (pr + 1) * LANES)
            slab = vnb[:, cols]
            mixed = jnp.where(first, _dot(w_ref[2 * pr], slab), _dot(w_ref[2 * pr + 1], slab))
            o_ref[rows, cols] = (u[:, cols] * (mixed + bt_ref[:, cols])).astype(o_ref.dtype)


def _sgu(z, norm_g, w_bf16, bias_t):
    ucol = 3 * NA_WIDTH // SG_WIDTH
    return pl.pallas_call(
        _sgu_kernel,
        grid=(R_ALL // TM,),
        in_specs=[pl.BlockSpec((TM, SG_WIDTH), lambda i: (i, ucol)),
                  pl.BlockSpec((TM, SG_WIDTH), lambda i: (i, ucol + 1)),
                  pl.BlockSpec((1, SG_WIDTH), lambda i: (0, 0)),
                  pl.BlockSpec((SG_GROUPS, SG_CHUNK, SG_CHUNK), lambda i: (0, 0, 0)),
                  pl.BlockSpec((SG_CHUNK, SG_WIDTH), lambda i: (0, 0))],
        out_specs=pl.BlockSpec((TM, SG_WIDTH), lambda i: (i, 0)),
        out_shape=jax.ShapeDtypeStruct((R_ALL, SG_WIDTH), BF16),
        compiler_params=_cparams(("arbitrary",)),
        name="sgu",
    )(z, z, norm_g.reshape(1, SG_WIDTH), w_bf16, bias_t)


def _outproj_kernel(*refs, operand_refs):
    tile = pl.program_id(0)
    operands, pos = [], 0
    for n in operand_refs:
        operands.append(_read_row_operand(refs[pos:pos + n], tile))
        pos += n
    lhs, x = operands[:-1], operands[-1]
    ws = refs[pos:pos + len(lhs)]
    g1_ref, ng_ref, sh_ref, sc_ref, wr_ref, xo_ref, h2_ref, aff_ref = refs[pos + len(lhs):]
    y = _dot(lhs[0], ws[0][...])
    for a, w in zip(lhs[1:], ws[1:]):
        y = y + _dot(a, w[...])
    xn = x + g1_ref[...] * y
    _write_token_tiled(xo_ref, 0, xn)
    hn = xn * lax.rsqrt(jnp.mean(xn * xn, axis=-1, keepdims=True) + EPS) * ng_ref[...]
    h2 = hn * (1.0 + sc_ref[...]) + sh_ref[...]
    _write_token_tiled(h2_ref, 0, h2)
    logits = _dot_nt(wr_ref[...], h2.astype(BF16))
    e = jnp.exp(logits - logits.max(axis=0, keepdims=True))
    aff_ref[...] = e / e.sum(axis=0, keepdims=True)


def _outproj(lhs_list, w_list, xall, gate1, norm_g, shift, scale, w_router_pad, n_rows):
    grp = lambda i: (_group_of_tile(i), 0, 0)
    row = lambda i: (i, 0)
    const2 = lambda i: (0, 0)
    row_operands = list(lhs_list) + [xall]
    operand_specs = [_row_specs(a, lambda i: i) for a in row_operands]
    row_args = [t for a in row_operands for t in (a if isinstance(a, tuple) else (a,))]
    in_specs = (
        [s for specs in operand_specs for s in specs]
        + [pl.BlockSpec(w.shape, const2) for w in w_list]
        + [pl.BlockSpec((None, 1, D_MODEL), grp),
           pl.BlockSpec((1, D_MODEL), const2),
           pl.BlockSpec((None, 1, D_MODEL), grp),
           pl.BlockSpec((None, 1, D_MODEL), grp),
           pl.BlockSpec((N_EXPERTS, D_MODEL), const2)])
    tiled = (n_rows * NSUB, LANES)
    tiled_spec = pl.BlockSpec((TM * NSUB, LANES), row)
    return pl.pallas_call(
        functools.partial(_outproj_kernel, operand_refs=tuple(len(s) for s in operand_specs)),
        grid=(n_rows // TM,),
        in_specs=in_specs,
        out_specs=[tiled_spec, tiled_spec, pl.BlockSpec((N_EXPERTS, TM), lambda i: (0, i))],
        out_shape=[jax.ShapeDtypeStruct(tiled, F32),
                   jax.ShapeDtypeStruct(tiled, F32),
                   jax.ShapeDtypeStruct((N_EXPERTS, n_rows), F32)],
        compiler_params=_cparams(("arbitrary",)),
        name="outproj",
    )(*row_args, *w_list, gate1, norm_g.reshape(1, D_MODEL), shift, scale, w_router_pad)


def _moe_segments(rows):
    cap = EC_CAPACITY * SEQ // N_EXPERTS
    segs = [(b * cap, (b + 1) * cap, b) for b in range(BATCH)]
    if rows > BATCH * cap:
        segs.append((BATCH * cap, rows, BATCH))
    return segs


class _MoeDmaPlan(NamedTuple):
    write_steps: int
    write_per_tick: int
    fetch_per_tick: int
    fetch_slots: int
    token_per_tick: int
    token_slots: int


def _moe_dma_plan(rows):
    ticks = MOE_NF * MOE_ROW_SPLIT
    write_ticks = MOE_WRITE_STEPS * MOE_ROW_SPLIT
    fetch_ticks = ticks - write_ticks
    token_ticks = ticks - MOE_ROW_SPLIT
    assert rows % write_ticks == 0 and 0 < MOE_WRITE_STEPS < MOE_NF
    fetch_per_tick = -(-rows // fetch_ticks)
    while (fetch_per_tick * fetch_ticks) % MOE_DMA_UNROLL:
        fetch_per_tick += 1
    token_per_tick = -(-rows // token_ticks)
    while (token_per_tick * token_ticks) % MOE_DMA_UNROLL:
        token_per_tick += 1
    return _MoeDmaPlan(MOE_WRITE_STEPS, rows // write_ticks, fetch_per_tick,
                       fetch_per_tick * fetch_ticks, token_per_tick, token_per_tick * token_ticks)


def _moe_kernel(idx_ref, h_hbm, x_hbm, gate_ref, g2_ref, wg_ref, wu_ref, wd_ref, xo_hbm,
                xg, xb, acc, res, wg_s, wu_s, wd_s, sem_h, sem_r, sem_s, *, rows):
    del x_hbm
    e, f = pl.program_id(0), pl.program_id(1)
    nf = pl.num_programs(1)
    plan = _moe_dma_plan(rows)

    def issue(count, first, n_valid, expert, make_copy):
        base = expert * rows
        for u in range(count):
            i = first + u
            n = idx_ref[base + jnp.minimum(i, n_valid - 1)]
            make_copy(pl.ds(pl.multiple_of(i * NSUB, NSUB), NSUB), n).start(priority=u % 2)

    def issue_all(n_slots, expert, make_copy):
        def group(o, carry):
            issue(MOE_DMA_UNROLL, o * MOE_DMA_UNROLL, rows, expert, make_copy)
            return carry

        lax.fori_loop(0, n_slots // MOE_DMA_UNROLL, group, 0)

    def token_copy(slot, n):
        return pltpu.make_async_copy(h_hbm.at[n], xg.at[slot, :], sem_h)

    def fetch_copy(slot, n):
        return pltpu.make_async_copy(xo_hbm.at[n], res.at[slot, :], sem_r)

    def writeback_copy(slot, n):
        return pltpu.make_async_copy(res.at[slot, :], xo_hbm.at[n], sem_s)

    all_tokens = pltpu.make_async_copy(xg, xg, sem_h)
    all_fetched = pltpu.make_async_copy(res, res, sem_r)
    written_rows = res.at[pl.ds(0, rows * NSUB), :]
    all_written = pltpu.make_async_copy(written_rows, written_rows, sem_s)

    prev = jnp.maximum(e - 1, 0)
    nxt = jnp.where(e + 1 < N_EXPERTS, e + 1, 0)

    @pl.when(f == 0)
    def _():
        @pl.when(e == 0)
        def _():
            issue_all(plan.token_slots, 0, token_copy)
            issue_all(plan.fetch_slots, 0, fetch_copy)

        all_tokens.wait()
        all_fetched.wait()

        @pl.when(e > 0)
        def _():
            for lo, hi, g in _moe_segments(rows):
                for r0 in range(lo, hi, MOE_UPDATE_ROWS):
                    nr = min(MOE_UPDATE_ROWS, hi - r0)
                    gate = jnp.concatenate([gate_ref[r0:r0 + nr, :]] * NSUB, axis=-1)
                    upd = acc[r0:r0 + nr, :] * gate * g2_ref[g]
                    _write_token_tiled(res, r0, _read_token_tiled(res, r0, nr) + upd)

        @pl.when(e < N_EXPERTS)
        def _():
            for r0 in range(0, rows, MOE_UPDATE_ROWS):
                nr = min(MOE_UPDATE_ROWS, rows - r0)
                xb[r0:r0 + nr, :] = _read_token_tiled(xg, r0, nr).astype(BF16)
            acc[...] = jnp.zeros_like(acc)

        @pl.when(e == N_EXPERTS)
        def _():
            issue_all(rows, N_EXPERTS - 1, writeback_copy)
            all_written.wait()

    @pl.when((f == plan.write_steps) & (e < N_EXPERTS))
    def _():
        all_written.wait()

    def step_body(write_back, gather):
        wg_s[...] = wg_ref[...].astype(BF16)
        wu_s[...] = wu_ref[...].astype(BF16)
        wd_s[...] = wd_ref[...].astype(BF16)
        rc = rows // MOE_ROW_SPLIT
        for r in range(MOE_ROW_SPLIT):
            sl = slice(r * rc, (r + 1) * rc)
            x = xb[sl, :]
            a = _dot(x, wg_s[...])
            b = _dot(x, wu_s[...])
            hid = (a * _sigmoid(a) * b).astype(BF16)
            acc[sl, :] += _dot(hid, wd_s[...])
            tick = f * MOE_ROW_SPLIT + r
            if write_back:
                issue(plan.write_per_tick, tick * plan.write_per_tick, rows, prev, writeback_copy)
            else:
                first = (tick - plan.write_steps * MOE_ROW_SPLIT) * plan.fetch_per_tick
                issue(plan.fetch_per_tick, first, rows, e, fetch_copy)
            if gather:
                first = (tick - MOE_ROW_SPLIT) * plan.token_per_tick
                issue(plan.token_per_tick, first, rows, nxt, token_copy)

    @pl.when((f == 0) & (e < N_EXPERTS))
    def _():
        step_body(write_back=True, gather=False)

    @pl.when((f > 0) & (f < plan.write_steps) & (e < N_EXPERTS))
    def _():
        step_body(write_back=True, gather=True)

    @pl.when((f >= plan.write_steps) & (e < N_EXPERTS))
    def _():
        step_body(write_back=False, gather=True)


def _moe_ffn(flat_idx, h2, x1, gates, gate2, w_gate, w_up, w_down, layer):
    rows = flat_idx.shape[1]
    assert rows % MOE_DMA_UNROLL == 0 and rows % MOE_ROW_SPLIT == 0
    plan = _moe_dma_plan(rows)
    nf = MOE_NF
    n_tokens = x1.shape[0] // NSUB
    as_tokens = lambda t: t.reshape(n_tokens, NSUB, LANES)
    ew = lambda e: jnp.minimum(e, N_EXPERTS - 1)
    fw = lambda e, f: jnp.where(e < N_EXPERTS, f, nf - 1)
    out = pl.pallas_call(
        functools.partial(_moe_kernel, rows=rows),
        grid_spec=pltpu.PrefetchScalarGridSpec(
            num_scalar_prefetch=1,
            grid=(N_EXPERTS + 1, nf),
            in_specs=[pl.BlockSpec(memory_space=pl.ANY),
                      pl.BlockSpec(memory_space=pl.ANY),
                      pl.BlockSpec((None, rows, LANES), lambda e, f, idx: (jnp.maximum(e - 1, 0), 0, 0)),
                      pl.BlockSpec((N_GROUPS, 1, D_MODEL), lambda e, f, idx: (0, 0, 0)),
                      pl.BlockSpec((None, None, D_MODEL, MOE_TF),
                                   lambda e, f, idx: (layer, ew(e), 0, fw(e, f))),
                      pl.BlockSpec((None, None, D_MODEL, MOE_TF),
                                   lambda e, f, idx: (layer, ew(e), 0, fw(e, f))),
                      pl.BlockSpec((None, None, MOE_TF, D_MODEL),
                                   lambda e, f, idx: (layer, ew(e), fw(e, f), 0))],
            out_specs=pl.BlockSpec(memory_space=pl.ANY),
            scratch_shapes=[pltpu.VMEM((plan.token_slots * NSUB, LANES), F32),
                            pltpu.VMEM((rows, D_MODEL), BF16),
                            pltpu.VMEM((rows, D_MODEL), F32),
                            pltpu.VMEM((plan.fetch_slots * NSUB, LANES), F32),
                            pltpu.VMEM((D_MODEL, MOE_TF), BF16),
                            pltpu.VMEM((D_MODEL, MOE_TF), BF16),
                            pltpu.VMEM((MOE_TF, D_MODEL), BF16),
                            pltpu.SemaphoreType.DMA,
                            pltpu.SemaphoreType.DMA,
                            pltpu.SemaphoreType.DMA]),
        out_shape=jax.ShapeDtypeStruct((n_tokens, NSUB, LANES), F32),
        input_output_aliases={2: 0},
        compiler_params=_cparams(("arbitrary", "arbitrary")),
        name="moe_ffn",
    )(flat_idx.reshape(-1), as_tokens(h2), as_tokens(x1), gates, gate2, w_gate, w_up, w_down)
    return out.reshape(x1.shape)


def _exclusive_prefix(mask, upper):
    out = []
    run = jnp.zeros((mask.shape[0], 1), F32)
    for c in range(mask.shape[1] // ROUTE_PREFIX_BLOCK):
        m = mask[:, c * ROUTE_PREFIX_BLOCK:(c + 1) * ROUTE_PREFIX_BLOCK]
        out.append(_dot(m.astype(BF16), upper) + run)
        run = run + jnp.sum(m, axis=1, keepdims=True)
    return jnp.concatenate(out, axis=1)


def _route_kernel(aff_ref, idx_ref, gate_ref, pos3, sel3, val3, *, cap):
    n = aff_ref.shape[1]
    v = aff_ref[...]
    thr = jnp.zeros((N_EXPERTS, 1), jnp.int32)
    for bit in range(30, -1, -1):
        cand = thr | (1 << bit)
        cnt = jnp.sum((v >= pltpu.bitcast(cand, F32)).astype(jnp.int32), axis=1, keepdims=True)
        thr = jnp.where(cnt >= cap, cand, thr)
    ri = lax.broadcasted_iota(jnp.int32, (ROUTE_PREFIX_BLOCK, ROUTE_PREFIX_BLOCK), 0)
    ci = lax.broadcasted_iota(jnp.int32, (ROUTE_PREFIX_BLOCK, ROUTE_PREFIX_BLOCK), 1)
    upper = (ri < ci).astype(BF16)
    gt = v >= pltpu.bitcast(thr + 1, F32)
    eq = (v >= pltpu.bitcast(thr, F32)) & ~gt
    need = (cap - jnp.sum(gt.astype(jnp.int32), axis=1, keepdims=True)).astype(F32)
    sel = gt | (eq & (_exclusive_prefix(eq.astype(F32), upper) < need))
    sel_f = sel.astype(F32)
    pos = _exclusive_prefix(sel_f, upper)
    for e in range(N_EXPERTS):
        pos3[e] = pos[e:e + 1, :].astype(jnp.int32)
        sel3[e] = sel_f[e:e + 1, :]
        val3[e] = v[e:e + 1, :]

    tok = lax.broadcasted_iota(jnp.int32, (1, n), 1)
    tok_hi = (tok >> 7).astype(F32)
    tok_lo = (tok & (LANES - 1)).astype(F32)
    hi_row = lax.broadcasted_iota(jnp.int32, (ROUTE_SLOT_HI, 1), 0)
    lo_row = lax.broadcasted_iota(jnp.int32, (LANES, 1), 0)

    def per_expert(e, carry):
        p, s, a = pos3[e], sel3[e], val3[e]
        a1 = a.astype(BF16).astype(F32)
        a2 = (a - a1).astype(BF16).astype(F32)
        a3 = a - a1 - a2
        hot_hi = jnp.where((p >> 7) == hi_row, s, 0.0)
        hot_lo = ((p & (LANES - 1)) == lo_row).astype(BF16)
        lhs = jnp.concatenate([hot_hi * tok_hi, hot_hi * tok_lo,
                               hot_hi * a1, hot_hi * a2, hot_hi * a3], axis=0).astype(BF16)
        out = _dot_nt(lhs, hot_lo)
        h = ROUTE_SLOT_HI
        idx_ref[e] = (out[0:h] * float(LANES) + out[h:2 * h]).astype(jnp.int32)
        gate_ref[e] = (out[2 * h:3 * h] + out[3 * h:4 * h]) + out[4 * h:5 * h]
        return carry

    lax.fori_loop(0, N_EXPERTS, per_expert, 0)


def _route(aff_t, n, cap, first_block, n_sets):
    assert cap <= ROUTE_SLOT_HI * LANES and n % ROUTE_PREFIX_BLOCK == 0 and n < (1 << 14)
    shape = (n_sets, N_EXPERTS, ROUTE_SLOT_HI, LANES)
    spec = pl.BlockSpec((None, N_EXPERTS, ROUTE_SLOT_HI, LANES), lambda s: (s, 0, 0, 0))
    return pl.pallas_call(
        functools.partial(_route_kernel, cap=cap),
        grid=(n_sets,),
        in_specs=[pl.BlockSpec((N_EXPERTS, n), lambda s: (0, first_block + s))],
        out_specs=[spec, spec],
        out_shape=[jax.ShapeDtypeStruct(shape, jnp.int32), jax.ShapeDtypeStruct(shape, F32)],
        scratch_shapes=[pltpu.VMEM((N_EXPERTS, 1, n), jnp.int32),
                        pltpu.VMEM((N_EXPERTS, 1, n), F32),
                        pltpu.VMEM((N_EXPERTS, 1, n), F32)],
        compiler_params=_cparams(("arbitrary",)),
        name="route",
    )(aff_t)


def _expert_choice(h2, x1, aff_t, gate2, w_gate, w_up, w_down, layer, with_ctx):
    def slots(idx, gate, cap, row0, set_len):
        idx = idx.reshape(BATCH, N_EXPERTS, -1)[:, :, :cap]
        gate = gate.reshape(BATCH, N_EXPERTS, -1)[:, :, :cap]
        idx = idx + row0 + (jnp.arange(BATCH, dtype=jnp.int32) * set_len)[:, None, None]
        return (jnp.swapaxes(idx, 0, 1).reshape(N_EXPERTS, BATCH * cap),
                jnp.swapaxes(gate, 0, 1).reshape(N_EXPERTS, BATCH * cap))

    cap = EC_CAPACITY * SEQ // N_EXPERTS
    flat, gate = slots(*_route(aff_t, SEQ, cap, 0, BATCH), cap, 0, SEQ)
    if with_ctx:
        cap_c = EC_CAPACITY * CTX_LEN // N_EXPERTS
        flat_c, gate_c = slots(*_route(aff_t, CTX_LEN, cap_c, R_LAT // CTX_LEN, BATCH),
                               cap_c, R_LAT, CTX_LEN)
        flat = jnp.concatenate([flat, flat_c], axis=1)
        gate = jnp.concatenate([gate, gate_c], axis=1)
    gate = jnp.broadcast_to(gate[..., None], gate.shape + (LANES,))
    return _moe_ffn(flat, h2, x1, gate, gate2, w_gate, w_up, w_down, layer)


def _final_norm_kernel(x_ref, ng_ref, o_ref):
    x = _read_rows(x_ref)
    o_ref[...] = x * lax.rsqrt(jnp.mean(x * x, axis=-1, keepdims=True) + EPS) * ng_ref[...]


def _final_norm(x, norm_g):
    n_rows = x.size // D_MODEL
    return pl.pallas_call(
        _final_norm_kernel,
        grid=(n_rows // TM,),
        in_specs=[_row_spec(x, lambda i: i),
                  pl.BlockSpec((1, D_MODEL), lambda i: (0, 0))],
        out_specs=pl.BlockSpec((TM, D_MODEL), lambda i: (i, 0)),
        out_shape=jax.ShapeDtypeStruct((n_rows, D_MODEL), F32),
        compiler_params=_cparams(("arbitrary",)),
        name="final_norm",
    )(x, norm_g.reshape(1, D_MODEL))


def _retention_kernel(lg_ref, q_ref, k_ref, v_ref, kc_ref, vc_ref, *rest, backward):
    if backward:
        o_ref, s_ref, dmat_ref, qd_ref, kd_ref = rest
    else:
        ob_ref, g_ref, o_ref, s_ref, dmat_ref, qd_ref, kd_ref = rest
    c = RET_CHUNK
    heads = range(RET_HEADS_PER_STEP)
    lgs = [lg_ref[1 if backward else 0, pl.program_id(1) * RET_HEADS_PER_STEP + hh] for hh in heads]
    qks = [slice(hh * RET_QK_DIM, (hh + 1) * RET_QK_DIM) for hh in heads]
    vvs = [slice(hh * RET_V_DIM, (hh + 1) * RET_V_DIM) for hh in heads]

    @pl.when(pl.program_id(2) == 0)
    def _():
        ia = lax.broadcasted_iota(jnp.int32, (c, c), 0)
        ib = lax.broadcasted_iota(jnp.int32, (c, c), 1)
        diff = ((ib - ia) if backward else (ia - ib)).astype(F32)
        pos = lax.broadcasted_iota(jnp.int32, (c, RET_QK_DIM), 0).astype(F32)
        cpos = lax.broadcasted_iota(jnp.int32, (CTX_LEN, RET_QK_DIM), 0).astype(F32)
        for hh, lg in zip(heads, lgs):
            dmat_ref[hh] = jnp.where(diff >= 0, jnp.exp(jnp.maximum(diff, 0.0) * lg), 0.0)
            if backward:
                qd_ref[hh] = jnp.exp((c - pos) * lg)
                kd_ref[hh] = jnp.exp(pos * lg)
            else:
                qd_ref[hh] = jnp.exp((pos + 1.0) * lg)
                kd_ref[hh] = jnp.exp((c - 1.0 - pos) * lg)
            cw = jnp.exp(cpos * lg) if backward else jnp.exp((CTX_LEN - 1.0 - cpos) * lg)
            kcw = (kc_ref[:, qks[hh]].astype(F32) * cw).astype(BF16)
            s_ref[hh] = _dot_tn(kcw, vc_ref[:, vvs[hh]])

    for hh, lg, qk, vv in zip(heads, lgs, qks, vvs):
        qb = q_ref[:, qk]
        kb = k_ref[:, qk]
        v = v_ref[:, vv]
        att = _dot_nt(qb, kb) * dmat_ref[hh]
        s_old = s_ref[hh]
        o = (_dot(att.astype(BF16), v)
             + _dot((qb.astype(F32) * qd_ref[hh]).astype(BF16), s_old.astype(BF16)))
        chunk_decay = jnp.exp(jnp.zeros((1, RET_V_DIM), F32) + c * lg)
        s_ref[hh] = s_old * chunk_decay + _dot_tn((kb.astype(F32) * kd_ref[hh]).astype(BF16), v)
        if backward:
            o_ref[:, vv] = o.astype(o_ref.dtype)
        else:
            of = o + ob_ref[:, vv].astype(F32)
            y = of * lax.rsqrt(jnp.mean(of * of, axis=-1, keepdims=True) + EPS)
            g = g_ref[:, vv].astype(F32)
            o_ref[:, vv] = (g * _sigmoid(g) * y).astype(o_ref.dtype)


def _retention_pass(log_gamma, z, o_back):
    backward = o_back is None
    c = RET_CHUNK
    n = SEQ // c
    hp = RET_HEADS_PER_STEP
    qk_w, v_w = hp * RET_QK_DIM, hp * RET_V_DIM
    qk_cols = RET_QK_WIDTH // qk_w
    v_base = 2 * RET_QK_WIDTH // v_w
    g_base = v_base + RET_V_WIDTH // v_w
    chunk = (lambda t: n - 1 - t) if backward else (lambda t: t)
    row = lambda b, t: b * n + chunk(t)
    ctx_row = lambda b: R_LAT // CTX_LEN + b
    in_specs = [
        pl.BlockSpec((c, qk_w), lambda b, h, t, lg: (row(b, t), h)),
        pl.BlockSpec((c, qk_w), lambda b, h, t, lg: (row(b, t), qk_cols + h)),
        pl.BlockSpec((c, v_w), lambda b, h, t, lg: (row(b, t), v_base + h)),
        pl.BlockSpec((CTX_LEN, qk_w), lambda b, h, t, lg: (ctx_row(b), qk_cols + h)),
        pl.BlockSpec((CTX_LEN, v_w), lambda b, h, t, lg: (ctx_row(b), v_base + h)),
    ]
    args = [z, z, z, z, z]
    if not backward:
        in_specs += [
            pl.BlockSpec((c, v_w), lambda b, h, t, lg: (row(b, t), h)),
            pl.BlockSpec((c, v_w), lambda b, h, t, lg: (row(b, t), g_base + h)),
        ]
        args += [o_back, z]
    return pl.pallas_call(
        functools.partial(_retention_kernel, backward=backward),
        grid_spec=pltpu.PrefetchScalarGridSpec(
            num_scalar_prefetch=1,
            grid=(BATCH, RET_HEADS // hp, n),
            in_specs=in_specs,
            out_specs=pl.BlockSpec((c, v_w), lambda b, h, t, lg: (row(b, t), h)),
            scratch_shapes=[pltpu.VMEM((hp, RET_QK_DIM, RET_V_DIM), F32),
                            pltpu.VMEM((hp, c, c), F32),
                            pltpu.VMEM((hp, c, RET_QK_DIM), F32),
                            pltpu.VMEM((hp, c, RET_QK_DIM), F32)]),
        out_shape=jax.ShapeDtypeStruct((R_LAT, RET_V_WIDTH), BF16),
        compiler_params=_cparams(("arbitrary", "arbitrary", "arbitrary")),
        name="retention_bwd" if backward else "retention_fwd",
    )(log_gamma, *args)


def _rope_tables():
    axis_dim = RET_QK_DIM // 2
    inv = (1.0 / (np.float32(ROPE_BASE) ** (np.arange(0, axis_dim, 2, dtype=np.float32)
                                             / np.float32(axis_dim)))).astype(np.float32)
    t = np.arange(SEQ)
    r = (t // GRID_W).astype(np.float32)
    col = (t % GRID_W).astype(np.float32)
    ang = np.concatenate([r[:, None] * inv, col[:, None] * inv], axis=-1).astype(np.float32)
    ang = np.concatenate([ang, np.zeros((TM, ang.shape[1]), np.float32)], axis=0)
    return (jnp.asarray(np.cos(ang).astype(np.float32)),
            jnp.asarray(np.sin(ang).astype(np.float32)))


def _split_mod(mod_layer):
    m = mod_layer[:N_GROUPS].reshape(N_GROUPS, 6, 1, D_MODEL)
    return [m[:, k] for k in range(6)]


def _router_t(w_router):
    return w_router.T.astype(BF16)


def kernel(x, c, ctx, c_ctx, ada_w, ada_b, norm1_g, norm2_g, ab_w_in, ab_w_out, na_rpb, sgu_norm_g, sgu_w, sgu_b, ret_w_in, ret_w_out, ret_decay_logit, moe_router, moe_w_gate, moe_w_up, moe_w_down, final_norm_g):
    assert DEPTH == 2 and x.shape == (BATCH, SEQ, D_MODEL) and ctx.shape == (BATCH, CTX_LEN, D_MODEL)

    cvec = jnp.zeros((8, D_MODEL), F32).at[:BATCH].set(c).at[BATCH].set(c_ctx)
    mod = _adaln(cvec, ada_w, ada_b)
    xall = (x.reshape(R_LAT, D_MODEL), ctx.reshape(R_CTX, D_MODEL))

    sh1, sc1, g1, sh2, sc2, g2 = _split_mod(mod[0])
    z = _inproj(xall, norm1_g[0], sh1, sc1, ab_w_in[0].astype(BF16), tn=AB_IN)
    a_all = (_na_attention(z, _na_bias_tiles(na_rpb[0])), _ctx_attention(z))
    bias_t = jnp.repeat(sgu_b[0].T, SG_GROUP_DIM, axis=1)
    bsg = _sgu(z, sgu_norm_g[0], sgu_w[0].astype(BF16), bias_t)
    w_out = ab_w_out[0].astype(BF16)
    x1, h2, aff = _outproj([a_all, bsg], [w_out[:NA_WIDTH], w_out[NA_WIDTH:]], xall, g1,
                           norm2_g[0], sh2, sc2, _router_t(moe_router[0]), R_ALL)
    xall = _expert_choice(h2, x1, aff, g2, moe_w_gate, moe_w_up, moe_w_down, 0, with_ctx=True)

    sh1, sc1, g1, sh2, sc2, g2 = _split_mod(mod[1])
    z = _inproj_ret(xall, norm1_g[1], sh1, sc1, ret_w_in[0].astype(BF16), *_rope_tables())
    log_gamma = jax.nn.log_sigmoid(ret_decay_logit[0].astype(F32))
    o_back = _retention_pass(log_gamma, z, None)
    ypre = _retention_pass(log_gamma, z, o_back)
    x1, h2, aff = _outproj([ypre], [ret_w_out[0].astype(BF16)], xall, g1,
                           norm2_g[1], sh2, sc2, _router_t(moe_router[1]), R_LAT)
    x2 = _expert_choice(h2, x1, aff, g2, moe_w_gate, moe_w_up, moe_w_down, 1, with_ctx=False)
    return _final_norm(x2, final_norm_g).reshape(BATCH, SEQ, D_MODEL)
```

```python
import functools
from typing import NamedTuple

import numpy as np
import jax
import jax.numpy as jnp
from jax import lax
from jax.experimental import pallas as pl
from jax.experimental.pallas import tpu as pltpu

F32 = jnp.float32
BF16 = jnp.bfloat16

D_MODEL = 1024
BATCH = 2
SEQ = 8192
DEPTH = 2
GRID_W = 64
CTX_LEN = 256
EPS = 1e-6
NEG_INF = -1e30

NA_HEADS = 8
NA_HEAD_DIM = 64
NA_WIN_H = 8
NA_WIN_W = 16
NA_WIDTH = NA_HEADS * NA_HEAD_DIM
SG_GROUPS = 8
SG_GROUP_DIM = 64
SG_CHUNK = 128
SG_WIDTH = SG_GROUPS * SG_GROUP_DIM
AB_IN = 3 * NA_WIDTH + 2 * SG_WIDTH

RET_HEADS = 4
RET_QK_DIM = 256
RET_V_DIM = 512
RET_QK_WIDTH = RET_HEADS * RET_QK_DIM
RET_V_WIDTH = RET_HEADS * RET_V_DIM
RET_IN = 2 * RET_QK_WIDTH + 2 * RET_V_WIDTH
ROPE_BASE = 10000.0

N_EXPERTS = 16
EC_CAPACITY = 2
D_FF_EXPERT = 2816

ROWS = SEQ // GRID_W
R_LAT = BATCH * SEQ
R_CTX = BATCH * CTX_LEN
R_ALL = R_LAT + R_CTX
N_GROUPS = BATCH + 1

LANES = 128
TM = 512
VMEM_LIMIT = 56 * 1024 * 1024

NA_QROWS = 8
NA_KROWS = 16
NA_QT = NA_QROWS * GRID_W
NA_KT = 256
NA_NKB = NA_KROWS * GRID_W // NA_KT
NA_PAIRS_PER_STEP = 2
RET_CHUNK = 256
RET_HEADS_PER_STEP = 4
MOE_TF = 256
MOE_NF = D_FF_EXPERT // MOE_TF
MOE_ROW_SPLIT = 4
MOE_WRITE_STEPS = 4
MOE_DMA_UNROLL = 8
MOE_UPDATE_ROWS = 64
ROUTE_PREFIX_BLOCK = 256
ROUTE_SLOT_HI = 8


def _group_of_tile(i):
    return jnp.minimum(i // (SEQ // TM), BATCH)


def _cparams(sem, vmem=VMEM_LIMIT):
    return pltpu.CompilerParams(dimension_semantics=sem, vmem_limit_bytes=vmem)


def _dot(a, b):
    return jnp.dot(a, b, preferred_element_type=F32)


def _dot_nt(a, b):
    return lax.dot_general(a, b, (((1,), (1,)), ((), ())), preferred_element_type=F32)


def _dot_tn(a, b):
    return lax.dot_general(a, b, (((0,), (0,)), ((), ())), preferred_element_type=F32)


def _sigmoid(x):
    return 1.0 / (1.0 + jnp.exp(-x))


def _gelu_tanh(x):
    c = np.float32(np.sqrt(2.0 / np.pi))
    return 0.5 * x * (1.0 + jnp.tanh(c * (x + np.float32(0.044715) * (x * x * x))))


def _adaln_kernel(c_ref, w_ref, b_ref, o_ref):
    cv = c_ref[...]
    s = (cv * _sigmoid(cv)).astype(BF16)
    o_ref[...] = _dot(s, w_ref[...].astype(BF16)) + b_ref[...]


def _adaln(cvec, ada_w, ada_b):
    tn = 1024
    n = 6 * D_MODEL
    return pl.pallas_call(
        _adaln_kernel,
        grid=(DEPTH, n // tn),
        in_specs=[
            pl.BlockSpec((8, D_MODEL), lambda l, j: (0, 0)),
            pl.BlockSpec((None, D_MODEL, tn), lambda l, j: (l, 0, j)),
            pl.BlockSpec((None, 1, tn), lambda l, j: (l, 0, j)),
        ],
        out_specs=pl.BlockSpec((None, 8, tn), lambda l, j: (l, 0, j)),
        out_shape=jax.ShapeDtypeStruct((DEPTH, 8, n), F32),
        compiler_params=_cparams(("arbitrary", "arbitrary")),
        name="adaln",
    )(cvec, ada_w, ada_b.reshape(DEPTH, 1, n))


NSUB = D_MODEL // LANES


def _read_token_tiled(ref, start, rows):
    return jnp.concatenate(
        [ref[pl.ds(start * NSUB + s, rows, stride=NSUB), :] for s in range(NSUB)], axis=-1)


def _write_token_tiled(ref, start, val):
    for s in range(NSUB):
        ref[pl.ds(start * NSUB + s, val.shape[0], stride=NSUB), :] = val[:, s * LANES:(s + 1) * LANES]


def _read_rows(x_ref):
    if x_ref.shape[-1] != LANES:
        return x_ref[...]
    return _read_token_tiled(x_ref, 0, x_ref.shape[0] // NSUB)


def _row_spec(x, index_map):
    if x.shape[-1] != LANES:
        return pl.BlockSpec((TM, x.shape[-1]), lambda *a: (index_map(*a), 0))
    return pl.BlockSpec((TM * NSUB, LANES), lambda *a: (index_map(*a), 0))


def _row_specs(x, index_map):
    if not isinstance(x, tuple):
        return [_row_spec(x, index_map)]
    lat, ctx = x
    lat_tiles = lat.shape[0] // TM
    assert lat.shape[0] == R_LAT and ctx.shape[0] == TM and lat.shape[1] == ctx.shape[1]
    width = lat.shape[1]
    return [pl.BlockSpec((TM, width), lambda *a: (jnp.minimum(index_map(*a), lat_tiles - 1), 0)),
            pl.BlockSpec((TM, width), lambda *a: (0, 0))]


def _read_row_operand(refs, tile):
    if len(refs) == 1:
        return _read_rows(refs[0])
    lat_ref, ctx_ref = refs
    return jnp.where(tile < R_LAT // TM, lat_ref[...], ctx_ref[...])


def _inproj_kernel(*refs, n_x):
    x_refs = refs[:n_x]
    g_ref, sh_ref, sc_ref, w_ref, z_ref, hb_ref = refs[n_x:]

    @pl.when(pl.program_id(1) == 0)
    def _():
        x = _read_row_operand(x_refs, pl.program_id(0))
        y = x * lax.rsqrt(jnp.mean(x * x, axis=-1, keepdims=True) + EPS) * g_ref[...]
        hb_ref[...] = (y * (1.0 + sc_ref[...]) + sh_ref[...]).astype(BF16)

    z_ref[...] = _dot(hb_ref[...], w_ref[...]).astype(z_ref.dtype)


def _inproj(xall, norm_g, shift, scale, w_bf16, tn):
    n = w_bf16.shape[1]
    grp = lambda i, j: (_group_of_tile(i), 0, 0)
    x_specs = _row_specs(xall, lambda i, j: i)
    x_args = xall if isinstance(xall, tuple) else (xall,)
    return pl.pallas_call(
        functools.partial(_inproj_kernel, n_x=len(x_specs)),
        grid=(R_ALL // TM, n // tn),
        in_specs=x_specs + [
            pl.BlockSpec((1, D_MODEL), lambda i, j: (0, 0)),
            pl.BlockSpec((None, 1, D_MODEL), grp),
            pl.BlockSpec((None, 1, D_MODEL), grp),
            pl.BlockSpec((D_MODEL, tn), lambda i, j: (0, j)),
        ],
        out_specs=pl.BlockSpec((TM, tn), lambda i, j: (i, j)),
        out_shape=jax.ShapeDtypeStruct((R_ALL, n), BF16),
        scratch_shapes=[pltpu.VMEM((TM, D_MODEL), BF16)],
        compiler_params=_cparams(("arbitrary", "arbitrary")),
        name="inproj",
    )(*x_args, norm_g.reshape(1, D_MODEL), shift, scale, w_bf16)


def _rope(t, cos, sin):
    half = RET_QK_DIM // 2
    t1, t2 = t[:, :half], t[:, half:]
    return jnp.concatenate([t1 * cos - t2 * sin, t1 * sin + t2 * cos], axis=-1)


def _inproj_ret_kernel(x_ref, g_ref, sh_ref, sc_ref, cos_ref, sin_ref, w_ref, z_ref):
    x = _read_rows(x_ref)
    y = x * lax.rsqrt(jnp.mean(x * x, axis=-1, keepdims=True) + EPS) * g_ref[...]
    hb = (y * (1.0 + sc_ref[...]) + sh_ref[...]).astype(BF16)
    cos, sin = cos_ref[...], sin_ref[...]
    kscale = np.float32(RET_QK_DIM ** -0.5)
    for h in range(2 * RET_HEADS):
        cols = slice(h * RET_QK_DIM, (h + 1) * RET_QK_DIM)
        t = _rope(_dot(hb, w_ref[:, cols]), cos, sin)
        z_ref[:, cols] = (t if h < RET_HEADS else t * kscale).astype(z_ref.dtype)
    rest = slice(2 * RET_QK_WIDTH, RET_IN)
    z_ref[:, rest] = _dot(hb, w_ref[:, rest]).astype(z_ref.dtype)


def _inproj_ret(xall, norm_g, shift, scale, w_bf16, cos, sin):
    grp = lambda i: (_group_of_tile(i), 0, 0)
    lat_tiles = SEQ // TM
    pos = lambda i: (jnp.where(i < BATCH * lat_tiles, i % lat_tiles, lat_tiles), 0)
    half = RET_QK_DIM // 2
    return pl.pallas_call(
        _inproj_ret_kernel,
        grid=(R_ALL // TM,),
        in_specs=[
            _row_spec(xall, lambda i: i),
            pl.BlockSpec((1, D_MODEL), lambda i: (0, 0)),
            pl.BlockSpec((None, 1, D_MODEL), grp),
            pl.BlockSpec((None, 1, D_MODEL), grp),
            pl.BlockSpec((TM, half), pos),
            pl.BlockSpec((TM, half), pos),
            pl.BlockSpec((D_MODEL, RET_IN), lambda i: (0, 0)),
        ],
        out_specs=pl.BlockSpec((TM, RET_IN), lambda i: (i, 0)),
        out_shape=jax.ShapeDtypeStruct((R_ALL, RET_IN), BF16),
        compiler_params=_cparams(("arbitrary",)),
        name="inproj_ret",
    )(xall, norm_g.reshape(1, D_MODEL), shift, scale, cos, sin, w_bf16)


def _na_tile_codes():
    masked = 2 * NA_WIN_H - 1
    kinds = (0, 1, ROWS // NA_QROWS - 1)
    d = np.full((3, NA_QROWS, NA_KROWS), masked, np.int64)
    for v, j in enumerate(kinds):
        kb = int(np.clip(NA_QROWS * j - NA_WIN_H // 2, 0, ROWS - NA_KROWS))
        for a in range(NA_QROWS):
            i = NA_QROWS * j + a
            r0 = int(np.clip(i - NA_WIN_H // 2, 0, ROWS - NA_WIN_H))
            for r in range(NA_KROWS):
                if r0 <= kb + r < r0 + NA_WIN_H:
                    d[v, a, r] = kb + r - i + NA_WIN_H - 1
    pairs = d.reshape(-1, 2)
    uniq = sorted(set(map(tuple, pairs)))
    code = np.array([uniq.index(tuple(p)) for p in pairs], np.int32)
    return code, np.array(uniq, np.int64)


_NA_CODES, _NA_CODE_ROWS = _na_tile_codes()


def _na_bias_tiles(rpb):
    qc = np.arange(GRID_W)[:, None]
    kc = np.arange(GRID_W)[None, :]
    cstart = np.clip(qc - NA_WIN_W // 2, 0, GRID_W - NA_WIN_W)
    col_ok = (kc >= cstart) & (kc < cstart + NA_WIN_W)
    dcol = np.clip(kc - qc, 1 - NA_WIN_W, NA_WIN_W - 1) + NA_WIN_W - 1
    onehot = (dcol.reshape(1, -1) == np.arange(2 * NA_WIN_W - 1)[:, None]).astype(np.float32)
    toe = jnp.einsum('hdm,mq->hdq', rpb.astype(F32), jnp.asarray(onehot),
                     precision=lax.Precision.HIGHEST)
    toe = toe.reshape(NA_HEADS, 2 * NA_WIN_H - 1, GRID_W, GRID_W)
    toe = jnp.where(jnp.asarray(col_ok), toe, NEG_INF)
    toe = jnp.concatenate([toe, jnp.full((NA_HEADS, 1, GRID_W, GRID_W), NEG_INF, F32)], axis=1)
    return jnp.concatenate([toe[:, _NA_CODE_ROWS[:, 0]], toe[:, _NA_CODE_ROWS[:, 1]]], axis=-1)


def _pair_attention(q, keys, vals, add_bias=None):
    lane = lax.broadcasted_iota(jnp.int32, (1, LANES), 1)
    scale = NA_HEAD_DIM ** -0.5
    out = jnp.zeros((q.shape[0], LANES), F32)
    for hh in range(2):
        in_head = (lane < NA_HEAD_DIM) if hh == 0 else (lane >= NA_HEAD_DIM)
        qm = jnp.where(in_head, q, jnp.zeros_like(q)) * jnp.asarray(scale, q.dtype)
        s = []
        for t, k in enumerate(keys):
            st = _dot_nt(qm, k)
            if add_bias is not None:
                st = add_bias(hh, t, st)
            s.append(st)
        m = s[0].max(axis=-1, keepdims=True)
        for st in s[1:]:
            m = jnp.maximum(m, st.max(axis=-1, keepdims=True))
        den = jnp.zeros_like(m)
        acc = jnp.zeros((q.shape[0], LANES), F32)
        for st, v in zip(s, vals):
            p = jnp.exp(st - m)
            den = den + p.sum(axis=-1, keepdims=True)
            acc = acc + _dot(p.astype(BF16), v)
        out = jnp.where(in_head, acc / den, out)
    return out


def _na_kernel(code_ref, q_ref, k0, k1, k2, k3, v0, v1, v2, v3, kc_ref, vc_ref, tile_ref, o_ref):
    nj = pl.num_programs(1)
    j = pl.program_id(1)
    kind = jnp.minimum(j, 1) + jnp.maximum(j - (nj - 2), 0)
    tiles_per_row = NA_KROWS // 2
    code_base = kind * (NA_QROWS * tiles_per_row)
    for pp in range(NA_PAIRS_PER_STEP):
        lanes = slice(pp * LANES, (pp + 1) * LANES)
        keys = [r[:, lanes] for r in (k0, k1, k2, k3, kc_ref)]
        vals = [r[:, lanes] for r in (v0, v1, v2, v3, vc_ref)]

        def add_bias(hh, t, st, pp=pp):
            if t >= NA_NKB:
                return st
            rows = []
            for a in range(NA_QROWS):
                cols = []
                for u in range(NA_KT // LANES):
                    code = code_ref[code_base + a * tiles_per_row + t * (NA_KT // LANES) + u]
                    piece = st[a * GRID_W:(a + 1) * GRID_W, u * LANES:(u + 1) * LANES]
                    cols.append(piece + tile_ref[2 * pp + hh, code])
                rows.append(jnp.concatenate(cols, axis=1))
            return jnp.concatenate(rows, axis=0)

        o_ref[:, lanes] = _pair_attention(q_ref[:, lanes], keys, vals, add_bias).astype(o_ref.dtype)


def _na_attention(z, tiles):
    nj = ROWS // NA_QROWS
    kblocks = SEQ // NA_KT
    width = NA_PAIRS_PER_STEP * LANES
    qcol, kcol, vcol = 0, NA_WIDTH // width, 2 * NA_WIDTH // width
    n_codes = tiles.shape[1]

    def kbase(j):
        return jnp.clip(2 * j - 1, 0, kblocks - NA_NKB)

    def kspec(t, col):
        return pl.BlockSpec((NA_KT, width),
                            lambda p, j, b, code: (b * kblocks + kbase(j) + t, col + p))

    ctx_row = lambda b: R_LAT // CTX_LEN + b
    in_specs = (
        [pl.BlockSpec((NA_QT, width), lambda p, j, b, code: (b * nj + j, qcol + p))]
        + [kspec(t, kcol) for t in range(NA_NKB)]
        + [kspec(t, vcol) for t in range(NA_NKB)]
        + [pl.BlockSpec((CTX_LEN, width), lambda p, j, b, code: (ctx_row(b), kcol + p)),
           pl.BlockSpec((CTX_LEN, width), lambda p, j, b, code: (ctx_row(b), vcol + p)),
           pl.BlockSpec((2 * NA_PAIRS_PER_STEP, n_codes, GRID_W, LANES),
                        lambda p, j, b, code: (p, 0, 0, 0))])
    return pl.pallas_call(
        _na_kernel,
        grid_spec=pltpu.PrefetchScalarGridSpec(
            num_scalar_prefetch=1,
            grid=(NA_HEADS // (2 * NA_PAIRS_PER_STEP), nj, BATCH),
            in_specs=in_specs,
            out_specs=pl.BlockSpec((NA_QT, width), lambda p, j, b, code: (b * nj + j, p))),
        out_shape=jax.ShapeDtypeStruct((R_LAT, NA_WIDTH), BF16),
        compiler_params=_cparams(("arbitrary", "arbitrary", "arbitrary")),
        name="na_attention",
    )(jnp.asarray(_NA_CODES), *([z] * (3 + 2 * NA_NKB)), tiles)


def _ctx_attn_kernel(q_ref, k_ref, v_ref, o_ref):
    o_ref[...] = _pair_attention(q_ref[...], [k_ref[...]], [v_ref[...]]).astype(o_ref.dtype)


def _ctx_attention(z):
    qcol, kcol, vcol = 0, NA_WIDTH // LANES, 2 * NA_WIDTH // LANES
    row = lambda b: R_LAT // CTX_LEN + b
    return pl.pallas_call(
        _ctx_attn_kernel,
        grid=(BATCH, NA_HEADS // 2),
        in_specs=[pl.BlockSpec((CTX_LEN, LANES), lambda b, p: (row(b), qcol + p)),
                  pl.BlockSpec((CTX_LEN, LANES), lambda b, p: (row(b), kcol + p)),
                  pl.BlockSpec((CTX_LEN, LANES), lambda b, p: (row(b), vcol + p))],
        out_specs=pl.BlockSpec((CTX_LEN, LANES), lambda b, p: (b, p)),
        out_shape=jax.ShapeDtypeStruct((R_CTX, NA_WIDTH), BF16),
        compiler_params=_cparams(("arbitrary", "arbitrary")),
        name="ctx_attention",
    )(z, z, z)


def _sgu_kernel(u_ref, v_ref, ng_ref, w_ref, bt_ref, o_ref):
    lane = lax.broadcasted_iota(jnp.int32, (1, LANES), 1)
    first = lane < SG_GROUP_DIM
    for ch in range(TM // SG_CHUNK):
        rows = slice(ch * SG_CHUNK, (ch + 1) * SG_CHUNK)
        u = _gelu_tanh(u_ref[rows, :].astype(F32))
        v = _gelu_tanh(v_ref[rows, :].astype(F32))
        vn = v * lax.rsqrt(jnp.mean(v * v, axis=-1, keepdims=True) + EPS) * ng_ref[...]
        vnb = vn.astype(BF16)
        for pr in range(SG_GROUPS // 2):
            cols = slice(pr * LANES, (pr + 1) * LANES)
            slab = vnb[:, cols]
            mixed = jnp.where(first, _dot(w_ref[2 * pr], slab), _dot(w_ref[2 * pr + 1], slab))
            o_ref[rows, cols] = (u[:, cols] * (mixed + bt_ref[:, cols])).astype(o_ref.dtype)


def _sgu(z, norm_g, w_bf16, bias_t):
    ucol = 3 * NA_WIDTH // SG_WIDTH
    return pl.pallas_call(
        _sgu_kernel,
        grid=(R_ALL // TM,),
        in_specs=[pl.BlockSpec((TM, SG_WIDTH), lambda i: (i, ucol)),
                  pl.BlockSpec((TM, SG_WIDTH), lambda i: (i, ucol + 1)),
                  pl.BlockSpec((1, SG_WIDTH), lambda i: (0, 0)),
                  pl.BlockSpec((SG_GROUPS, SG_CHUNK, SG_CHUNK), lambda i: (0, 0, 0)),
                  pl.BlockSpec((SG_CHUNK, SG_WIDTH), lambda i: (0, 0))],
        out_specs=pl.BlockSpec((TM, SG_WIDTH), lambda i: (i, 0)),
        out_shape=jax.ShapeDtypeStruct((R_ALL, SG_WIDTH), BF16),
        compiler_params=_cparams(("arbitrary",)),
        name="sgu",
    )(z, z, norm_g.reshape(1, SG_WIDTH), w_bf16, bias_t)


def _outproj_kernel(*refs, operand_refs):
    tile = pl.program_id(0)
    operands, pos = [], 0
    for n in operand_refs:
        operands.append(_read_row_operand(refs[pos:pos + n], tile))
        pos += n
    lhs, x = operands[:-1], operands[-1]
    ws = refs[pos:pos + len(lhs)]
    g1_ref, ng_ref, sh_ref, sc_ref, wr_ref, xo_ref, h2_ref, aff_ref = refs[pos + len(lhs):]
    y = _dot(lhs[0], ws[0][...])
    for a, w in zip(lhs[1:], ws[1:]):
        y = y + _dot(a, w[...])
    xn = x + g1_ref[...] * y
    _write_token_tiled(xo_ref, 0, xn)
    hn = xn * lax.rsqrt(jnp.mean(xn * xn, axis=-1, keepdims=True) + EPS) * ng_ref[...]
    h2 = hn * (1.0 + sc_ref[...]) + sh_ref[...]
    _write_token_tiled(h2_ref, 0, h2)
    logits = _dot_nt(wr_ref[...], h2.astype(BF16))
    e = jnp.exp(logits - logits.max(axis=0, keepdims=True))
    aff_ref[...] = e / e.sum(axis=0, keepdims=True)


def _outproj(lhs_list, w_list, xall, gate1, norm_g, shift, scale, w_router_pad, n_rows):
    grp = lambda i: (_group_of_tile(i), 0, 0)
    row = lambda i: (i, 0)
    const2 = lambda i: (0, 0)
    row_operands = list(lhs_list) + [xall]
    operand_specs = [_row_specs(a, lambda i: i) for a in row_operands]
    row_args = [t for a in row_operands for t in (a if isinstance(a, tuple) else (a,))]
    in_specs = (
        [s for specs in operand_specs for s in specs]
        + [pl.BlockSpec(w.shape, const2) for w in w_list]
        + [pl.BlockSpec((None, 1, D_MODEL), grp),
           pl.BlockSpec((1, D_MODEL), const2),
           pl.BlockSpec((None, 1, D_MODEL), grp),
           pl.BlockSpec((None, 1, D_MODEL), grp),
           pl.BlockSpec((N_EXPERTS, D_MODEL), const2)])
    tiled = (n_rows * NSUB, LANES)
    tiled_spec = pl.BlockSpec((TM * NSUB, LANES), row)
    return pl.pallas_call(
        functools.partial(_outproj_kernel, operand_refs=tuple(len(s) for s in operand_specs)),
        grid=(n_rows // TM,),
        in_specs=in_specs,
        out_specs=[tiled_spec, tiled_spec, pl.BlockSpec((N_EXPERTS, TM), lambda i: (0, i))],
        out_shape=[jax.ShapeDtypeStruct(tiled, F32),
                   jax.ShapeDtypeStruct(tiled, F32),
                   jax.ShapeDtypeStruct((N_EXPERTS, n_rows), F32)],
        compiler_params=_cparams(("arbitrary",)),
        name="outproj",
    )(*row_args, *w_list, gate1, norm_g.reshape(1, D_MODEL), shift, scale, w_router_pad)


def _moe_segments(rows):
    cap = EC_CAPACITY * SEQ // N_EXPERTS
    segs = [(b * cap, (b + 1) * cap, b) for b in range(BATCH)]
    if rows > BATCH * cap:
        segs.append((BATCH * cap, rows, BATCH))
    return segs


class _MoeDmaPlan(NamedTuple):
    write_steps: int
    write_per_tick: int
    fetch_per_tick: int
    fetch_slots: int
    token_per_tick: int
    token_slots: int


def _moe_dma_plan(rows):
    ticks = MOE_NF * MOE_ROW_SPLIT
    write_ticks = MOE_WRITE_STEPS * MOE_ROW_SPLIT
    fetch_ticks = ticks - write_ticks
    token_ticks = ticks - MOE_ROW_SPLIT
    assert rows % write_ticks == 0 and 0 < MOE_WRITE_STEPS < MOE_NF
    fetch_per_tick = -(-rows // fetch_ticks)
    while (fetch_per_tick * fetch_ticks) % MOE_DMA_UNROLL:
        fetch_per_tick += 1
    token_per_tick = -(-rows // token_ticks)
    while (token_per_tick * token_ticks) % MOE_DMA_UNROLL:
        token_per_tick += 1
    return _MoeDmaPlan(MOE_WRITE_STEPS, rows // write_ticks, fetch_per_tick,
                       fetch_per_tick * fetch_ticks, token_per_tick, token_per_tick * token_ticks)


def _moe_kernel(idx_ref, h_hbm, x_hbm, gate_ref, g2_ref, wg_ref, wu_ref, wd_ref, xo_hbm,
                xg, xb, acc, res, wg_s, wu_s, wd_s, sem_h, sem_r, sem_s, *, rows):
    del x_hbm
    e, f = pl.program_id(0), pl.program_id(1)
    nf = pl.num_programs(1)
    plan = _moe_dma_plan(rows)

    def issue(count, first, n_valid, expert, make_copy):
        base = expert * rows
        for u in range(count):
            i = first + u
            n = idx_ref[base + jnp.minimum(i, n_valid - 1)]
            make_copy(pl.ds(pl.multiple_of(i * NSUB, NSUB), NSUB), n).start(priority=u % 2)

    def issue_all(n_slots, expert, make_copy):
        def group(o, carry):
            issue(MOE_DMA_UNROLL, o * MOE_DMA_UNROLL, rows, expert, make_copy)
            return carry

        lax.fori_loop(0, n_slots // MOE_DMA_UNROLL, group, 0)

    def token_copy(slot, n):
        return pltpu.make_async_copy(h_hbm.at[n], xg.at[slot, :], sem_h)

    def fetch_copy(slot, n):
        return pltpu.make_async_copy(xo_hbm.at[n], res.at[slot, :], sem_r)

    def writeback_copy(slot, n):
        return pltpu.make_async_copy(res.at[slot, :], xo_hbm.at[n], sem_s)

    all_tokens = pltpu.make_async_copy(xg, xg, sem_h)
    all_fetched = pltpu.make_async_copy(res, res, sem_r)
    written_rows = res.at[pl.ds(0, rows * NSUB), :]
    all_written = pltpu.make_async_copy(written_rows, written_rows, sem_s)

    prev = jnp.maximum(e - 1, 0)
    nxt = jnp.where(e + 1 < N_EXPERTS, e + 1, 0)

    def apply_update():
        for lo, hi, g in _moe_segments(rows):
            for r0 in range(lo, hi, MOE_UPDATE_ROWS):
                nr = min(MOE_UPDATE_ROWS, hi - r0)
                gate = gate_ref[r0:r0 + nr, :]
                sums = []
                for s in range(NSUB):
                    cols = slice(s * LANES, (s + 1) * LANES)
                    tiled = pl.ds(r0 * NSUB + s, nr, stride=NSUB)
                    sums.append(res[tiled, :] + acc[r0:r0 + nr, cols] * gate * g2_ref[g, :, cols])
                for s in range(NSUB):
                    res[pl.ds(r0 * NSUB + s, nr, stride=NSUB), :] = sums[s]

    @pl.when(f == 0)
    def _():
        @pl.when(e == 0)
        def _():
            issue_all(plan.token_slots, 0, token_copy)
            issue_all(plan.fetch_slots, 0, fetch_copy)

        all_tokens.wait()
        all_fetched.wait()

        @pl.when(e > 0)
        def _():
            apply_update()

        @pl.when(e == N_EXPERTS)
        def _():
            issue_all(rows, N_EXPERTS - 1, writeback_copy)
            all_written.wait()

    @pl.when((f == plan.write_steps) & (e < N_EXPERTS))
    def _():
        all_written.wait()

    def step_body(write_back, gather, first=False):
        wg_s[...] = wg_ref[...].astype(BF16)
        wu_s[...] = wu_ref[...].astype(BF16)
        wd_s[...] = wd_ref[...].astype(BF16)
        rc = rows // MOE_ROW_SPLIT
        for r in range(MOE_ROW_SPLIT):
            sl = slice(r * rc, (r + 1) * rc)
            if first:
                for r0 in range(r * rc, (r + 1) * rc, MOE_UPDATE_ROWS):
                    nr = min(MOE_UPDATE_ROWS, (r + 1) * rc - r0)
                    xb[r0:r0 + nr, :] = _read_token_tiled(xg, r0, nr).astype(BF16)
            x = xb[sl, :]
            a = _dot(x, wg_s[...])
            b = _dot(x, wu_s[...])
            hid = (a * _sigmoid(a) * b).astype(BF16)
            if first:
                acc[sl, :] = _dot(hid, wd_s[...])
            else:
                acc[sl, :] += _dot(hid, wd_s[...])
            tick = f * MOE_ROW_SPLIT + r
            if write_back:
                issue(plan.write_per_tick, tick * plan.write_per_tick, rows, prev, writeback_copy)
            else:
                slot0 = (tick - plan.write_steps * MOE_ROW_SPLIT) * plan.fetch_per_tick
                issue(plan.fetch_per_tick, slot0, rows, e, fetch_copy)
            if gather:
                slot0 = (tick - MOE_ROW_SPLIT) * plan.token_per_tick
                issue(plan.token_per_tick, slot0, rows, nxt, token_copy)

    @pl.when((f == 0) & (e < N_EXPERTS))
    def _():
        step_body(write_back=True, gather=False, first=True)

    @pl.when((f > 0) & (f < plan.write_steps) & (e < N_EXPERTS))
    def _():
        step_body(write_back=True, gather=True)

    @pl.when((f >= plan.write_steps) & (e < N_EXPERTS))
    def _():
        step_body(write_back=False, gather=True)


def _moe_ffn(flat_idx, h2, x1, gates, gate2, w_gate, w_up, w_down, layer):
    rows = flat_idx.shape[1]
    assert rows % MOE_DMA_UNROLL == 0 and rows % MOE_ROW_SPLIT == 0
    plan = _moe_dma_plan(rows)
    nf = MOE_NF
    n_tokens = x1.shape[0] // NSUB
    as_tokens = lambda t: t.reshape(n_tokens, NSUB, LANES)
    ew = lambda e: jnp.minimum(e, N_EXPERTS - 1)
    fw = lambda e, f: jnp.where(e < N_EXPERTS, f, nf - 1)
    out = pl.pallas_call(
        functools.partial(_moe_kernel, rows=rows),
        grid_spec=pltpu.PrefetchScalarGridSpec(
            num_scalar_prefetch=1,
            grid=(N_EXPERTS + 1, nf),
            in_specs=[pl.BlockSpec(memory_space=pl.ANY),
                      pl.BlockSpec(memory_space=pl.ANY),
                      pl.BlockSpec((None, rows, LANES), lambda e, f, idx: (jnp.maximum(e - 1, 0), 0, 0)),
                      pl.BlockSpec((N_GROUPS, 1, D_MODEL), lambda e, f, idx: (0, 0, 0)),
                      pl.BlockSpec((None, None, D_MODEL, MOE_TF),
                                   lambda e, f, idx: (layer, ew(e), 0, fw(e, f))),
                      pl.BlockSpec((None, None, D_MODEL, MOE_TF),
                                   lambda e, f, idx: (layer, ew(e), 0, fw(e, f))),
                      pl.BlockSpec((None, None, MOE_TF, D_MODEL),
                                   lambda e, f, idx: (layer, ew(e), fw(e, f), 0))],
            out_specs=pl.BlockSpec(memory_space=pl.ANY),
            scratch_shapes=[pltpu.VMEM((plan.token_slots * NSUB, LANES), F32),
                            pltpu.VMEM((rows, D_MODEL), BF16),
                            pltpu.VMEM((rows, D_MODEL), F32),
                            pltpu.VMEM((plan.fetch_slots * NSUB, LANES), F32),
                            pltpu.VMEM((D_MODEL, MOE_TF), BF16),
                            pltpu.VMEM((D_MODEL, MOE_TF), BF16),
                            pltpu.VMEM((MOE_TF, D_MODEL), BF16),
                            pltpu.SemaphoreType.DMA,
                            pltpu.SemaphoreType.DMA,
                            pltpu.SemaphoreType.DMA]),
        out_shape=jax.ShapeDtypeStruct((n_tokens, NSUB, LANES), F32),
        input_output_aliases={2: 0},
        compiler_params=_cparams(("arbitrary", "arbitrary")),
        name="moe_ffn",
    )(flat_idx.reshape(-1), as_tokens(h2), as_tokens(x1), gates, gate2, w_gate, w_up, w_down)
    return out.reshape(x1.shape)


def _exclusive_prefix(mask, upper):
    out = []
    run = jnp.zeros((mask.shape[0], 1), F32)
    for c in range(mask.shape[1] // ROUTE_PREFIX_BLOCK):
        m = mask[:, c * ROUTE_PREFIX_BLOCK:(c + 1) * ROUTE_PREFIX_BLOCK]
        out.append(_dot(m.astype(BF16), upper) + run)
        run = run + jnp.sum(m, axis=1, keepdims=True)
    return jnp.concatenate(out, axis=1)


def _route_kernel(aff_ref, idx_ref, gate_ref, pos3, sel3, val3, *, cap):
    n = aff_ref.shape[1]
    v = aff_ref[...]
    thr = jnp.zeros((N_EXPERTS, 1), jnp.int32)
    for bit in range(30, -1, -1):
        cand = thr | (1 << bit)
        cnt = jnp.sum((v >= pltpu.bitcast(cand, F32)).astype(jnp.int32), axis=1, keepdims=True)
        thr = jnp.where(cnt >= cap, cand, thr)
    ri = lax.broadcasted_iota(jnp.int32, (ROUTE_PREFIX_BLOCK, ROUTE_PREFIX_BLOCK), 0)
    ci = lax.broadcasted_iota(jnp.int32, (ROUTE_PREFIX_BLOCK, ROUTE_PREFIX_BLOCK), 1)
    upper = (ri < ci).astype(BF16)
    gt = v >= pltpu.bitcast(thr + 1, F32)
    eq = (v >= pltpu.bitcast(thr, F32)) & ~gt
    need = (cap - jnp.sum(gt.astype(jnp.int32), axis=1, keepdims=True)).astype(F32)
    sel = gt | (eq & (_exclusive_prefix(eq.astype(F32), upper) < need))
    sel_f = sel.astype(F32)
    pos = _exclusive_prefix(sel_f, upper)
    for e in range(N_EXPERTS):
        pos3[e] = pos[e:e + 1, :].astype(jnp.int32)
        sel3[e] = sel_f[e:e + 1, :]
        val3[e] = v[e:e + 1, :]

    tok = lax.broadcasted_iota(jnp.int32, (1, n), 1)
    tok_hi = (tok >> 7).astype(F32)
    tok_lo = (tok & (LANES - 1)).astype(F32)
    hi_row = lax.broadcasted_iota(jnp.int32, (ROUTE_SLOT_HI, 1), 0)
    lo_row = lax.broadcasted_iota(jnp.int32, (LANES, 1), 0)

    def per_expert(e, carry):
        p, s, a = pos3[e], sel3[e], val3[e]
        a1 = a.astype(BF16).astype(F32)
        a2 = (a - a1).astype(BF16).astype(F32)
        a3 = a - a1 - a2
        hot_hi = jnp.where((p >> 7) == hi_row, s, 0.0)
        hot_lo = ((p & (LANES - 1)) == lo_row).astype(BF16)
        lhs = jnp.concatenate([hot_hi * tok_hi, hot_hi * tok_lo,
                               hot_hi * a1, hot_hi * a2, hot_hi * a3], axis=0).astype(BF16)
        out = _dot_nt(lhs, hot_lo)
        h = ROUTE_SLOT_HI
        idx_ref[e] = (out[0:h] * float(LANES) + out[h:2 * h]).astype(jnp.int32)
        gate_ref[e] = (out[2 * h:3 * h] + out[3 * h:4 * h]) + out[4 * h:5 * h]
        return carry

    lax.fori_loop(0, N_EXPERTS, per_expert, 0)


def _route(aff_t, n, cap, first_block, n_sets):
    assert cap <= ROUTE_SLOT_HI * LANES and n % ROUTE_PREFIX_BLOCK == 0 and n < (1 << 14)
    shape = (n_sets, N_EXPERTS, ROUTE_SLOT_HI, LANES)
    spec = pl.BlockSpec((None, N_EXPERTS, ROUTE_SLOT_HI, LANES), lambda s: (s, 0, 0, 0))
    return pl.pallas_call(
        functools.partial(_route_kernel, cap=cap),
        grid=(n_sets,),
        in_specs=[pl.BlockSpec((N_EXPERTS, n), lambda s: (0, first_block + s))],
        out_specs=[spec, spec],
        out_shape=[jax.ShapeDtypeStruct(shape, jnp.int32), jax.ShapeDtypeStruct(shape, F32)],
        scratch_shapes=[pltpu.VMEM((N_EXPERTS, 1, n), jnp.int32),
                        pltpu.VMEM((N_EXPERTS, 1, n), F32),
                        pltpu.VMEM((N_EXPERTS, 1, n), F32)],
        compiler_params=_cparams(("arbitrary",)),
        name="route",
    )(aff_t)


def _expert_choice(h2, x1, aff_t, gate2, w_gate, w_up, w_down, layer, with_ctx):
    def slots(idx, gate, cap, row0, set_len):
        idx = idx.reshape(BATCH, N_EXPERTS, -1)[:, :, :cap]
        gate = gate.reshape(BATCH, N_EXPERTS, -1)[:, :, :cap]
        idx = idx + row0 + (jnp.arange(BATCH, dtype=jnp.int32) * set_len)[:, None, None]
        return (jnp.swapaxes(idx, 0, 1).reshape(N_EXPERTS, BATCH * cap),
                jnp.swapaxes(gate, 0, 1).reshape(N_EXPERTS, BATCH * cap))

    cap = EC_CAPACITY * SEQ // N_EXPERTS
    flat, gate = slots(*_route(aff_t, SEQ, cap, 0, BATCH), cap, 0, SEQ)
    if with_ctx:
        cap_c = EC_CAPACITY * CTX_LEN // N_EXPERTS
        flat_c, gate_c = slots(*_route(aff_t, CTX_LEN, cap_c, R_LAT // CTX_LEN, BATCH),
                               cap_c, R_LAT, CTX_LEN)
        flat = jnp.concatenate([flat, flat_c], axis=1)
        gate = jnp.concatenate([gate, gate_c], axis=1)
    gate = jnp.broadcast_to(gate[..., None], gate.shape + (LANES,))
    return _moe_ffn(flat, h2, x1, gate, gate2, w_gate, w_up, w_down, layer)


def _final_norm_kernel(x_ref, ng_ref, o_ref):
    x = _read_rows(x_ref)
    o_ref[...] = x * lax.rsqrt(jnp.mean(x * x, axis=-1, keepdims=True) + EPS) * ng_ref[...]


def _final_norm(x, norm_g):
    n_rows = x.size // D_MODEL
    return pl.pallas_call(
        _final_norm_kernel,
        grid=(n_rows // TM,),
        in_specs=[_row_spec(x, lambda i: i),
                  pl.BlockSpec((1, D_MODEL), lambda i: (0, 0))],
        out_specs=pl.BlockSpec((TM, D_MODEL), lambda i: (i, 0)),
        out_shape=jax.ShapeDtypeStruct((n_rows, D_MODEL), F32),
        compiler_params=_cparams(("arbitrary",)),
        name="final_norm",
    )(x, norm_g.reshape(1, D_MODEL))


def _retention_kernel(lg_ref, q_ref, k_ref, v_ref, kc_ref, vc_ref, *rest, backward):
    if backward:
        o_ref, s_ref, dmat_ref, qd_ref, kd_ref = rest
    else:
        ob_ref, g_ref, o_ref, s_ref, dmat_ref, qd_ref, kd_ref = rest
    c = RET_CHUNK
    heads = range(RET_HEADS_PER_STEP)
    lgs = [lg_ref[1 if backward else 0, pl.program_id(1) * RET_HEADS_PER_STEP + hh] for hh in heads]
    qks = [slice(hh * RET_QK_DIM, (hh + 1) * RET_QK_DIM) for hh in heads]
    vvs = [slice(hh * RET_V_DIM, (hh + 1) * RET_V_DIM) for hh in heads]

    @pl.when(pl.program_id(2) == 0)
    def _():
        ia = lax.broadcasted_iota(jnp.int32, (c, c), 0)
        ib = lax.broadcasted_iota(jnp.int32, (c, c), 1)
        diff = ((ib - ia) if backward else (ia - ib)).astype(F32)
        pos = lax.broadcasted_iota(jnp.int32, (c, RET_QK_DIM), 0).astype(F32)
        cpos = lax.broadcasted_iota(jnp.int32, (CTX_LEN, RET_QK_DIM), 0).astype(F32)
        for hh, lg in zip(heads, lgs):
            dmat_ref[hh] = jnp.where(diff >= 0, jnp.exp(jnp.maximum(diff, 0.0) * lg), 0.0)
            if backward:
                qd_ref[hh] = jnp.exp((c - pos) * lg)
                kd_ref[hh] = jnp.exp(pos * lg)
            else:
                qd_ref[hh] = jnp.exp((pos + 1.0) * lg)
                kd_ref[hh] = jnp.exp((c - 1.0 - pos) * lg)
            cw = jnp.exp(cpos * lg) if backward else jnp.exp((CTX_LEN - 1.0 - cpos) * lg)
            kcw = (kc_ref[:, qks[hh]].astype(F32) * cw).astype(BF16)
            s_ref[hh] = _dot_tn(kcw, vc_ref[:, vvs[hh]])

    for hh, lg, qk, vv in zip(heads, lgs, qks, vvs):
        qb = q_ref[:, qk]
        kb = k_ref[:, qk]
        v = v_ref[:, vv]
        att = _dot_nt(qb, kb) * dmat_ref[hh]
        s_old = s_ref[hh]
        o = (_dot(att.astype(BF16), v)
             + _dot((qb.astype(F32) * qd_ref[hh]).astype(BF16), s_old.astype(BF16)))
        chunk_decay = jnp.exp(jnp.zeros((1, RET_V_DIM), F32) + c * lg)
        s_ref[hh] = s_old * chunk_decay + _dot_tn((kb.astype(F32) * kd_ref[hh]).astype(BF16), v)
        if backward:
            o_ref[:, vv] = o.astype(o_ref.dtype)
        else:
            of = o + ob_ref[:, vv].astype(F32)
            y = of * lax.rsqrt(jnp.mean(of * of, axis=-1, keepdims=True) + EPS)
            g = g_ref[:, vv].astype(F32)
            o_ref[:, vv] = (g * _sigmoid(g) * y).astype(o_ref.dtype)


def _retention_pass(log_gamma, z, o_back):
    backward = o_back is None
    c = RET_CHUNK
    n = SEQ // c
    hp = RET_HEADS_PER_STEP
    qk_w, v_w = hp * RET_QK_DIM, hp * RET_V_DIM
    qk_cols = RET_QK_WIDTH // qk_w
    v_base = 2 * RET_QK_WIDTH // v_w
    g_base = v_base + RET_V_WIDTH // v_w
    chunk = (lambda t: n - 1 - t) if backward else (lambda t: t)
    row = lambda b, t: b * n + chunk(t)
    ctx_row = lambda b: R_LAT // CTX_LEN + b
    in_specs = [
        pl.BlockSpec((c, qk_w), lambda b, h, t, lg: (row(b, t), h)),
        pl.BlockSpec((c, qk_w), lambda b, h, t, lg: (row(b, t), qk_cols + h)),
        pl.BlockSpec((c, v_w), lambda b, h, t, lg: (row(b, t), v_base + h)),
        pl.BlockSpec((CTX_LEN, qk_w), lambda b, h, t, lg: (ctx_row(b), qk_cols + h)),
        pl.BlockSpec((CTX_LEN, v_w), lambda b, h, t, lg: (ctx_row(b), v_base + h)),
    ]
    args = [z, z, z, z, z]
    if not backward:
        in_specs += [
            pl.BlockSpec((c, v_w), lambda b, h, t, lg: (row(b, t), h)),
            pl.BlockSpec((c, v_w), lambda b, h, t, lg: (row(b, t), g_base + h)),
        ]
        args += [o_back, z]
    return pl.pallas_call(
        functools.partial(_retention_kernel, backward=backward),
        grid_spec=pltpu.PrefetchScalarGridSpec(
            num_scalar_prefetch=1,
            grid=(BATCH, RET_HEADS // hp, n),
            in_specs=in_specs,
            out_specs=pl.BlockSpec((c, v_w), lambda b, h, t, lg: (row(b, t), h)),
            scratch_shapes=[pltpu.VMEM((hp, RET_QK_DIM, RET_V_DIM), F32),
                            pltpu.VMEM((hp, c, c), F32),
                            pltpu.VMEM((hp, c, RET_QK_DIM), F32),
                            pltpu.VMEM((hp, c, RET_QK_DIM), F32)]),
        out_shape=jax.ShapeDtypeStruct((R_LAT, RET_V_WIDTH), BF16),
        compiler_params=_cparams(("arbitrary", "arbitrary", "arbitrary")),
        name="retention_bwd" if backward else "retention_fwd",
    )(log_gamma, *args)


def _rope_tables():
    axis_dim = RET_QK_DIM // 2
    inv = (1.0 / (np.float32(ROPE_BASE) ** (np.arange(0, axis_dim, 2, dtype=np.float32)
                                             / np.float32(axis_dim)))).astype(np.float32)
    t = np.arange(SEQ)
    r = (t // GRID_W).astype(np.float32)
    col = (t % GRID_W).astype(np.float32)
    ang = np.concatenate([r[:, None] * inv, col[:, None] * inv], axis=-1).astype(np.float32)
    ang = np.concatenate([ang, np.zeros((TM, ang.shape[1]), np.float32)], axis=0)
    return (jnp.asarray(np.cos(ang).astype(np.float32)),
            jnp.asarray(np.sin(ang).astype(np.float32)))


def _split_mod(mod_layer):
    m = mod_layer[:N_GROUPS].reshape(N_GROUPS, 6, 1, D_MODEL)
    return [m[:, k] for k in range(6)]


def _router_t(w_router):
    return w_router.T.astype(BF16)


def kernel(x, c, ctx, c_ctx, ada_w, ada_b, norm1_g, norm2_g, ab_w_in, ab_w_out, na_rpb, sgu_norm_g, sgu_w, sgu_b, ret_w_in, ret_w_out, ret_decay_logit, moe_router, moe_w_gate, moe_w_up, moe_w_down, final_norm_g):
    assert DEPTH == 2 and x.shape == (BATCH, SEQ, D_MODEL) and ctx.shape == (BATCH, CTX_LEN, D_MODEL)

    cvec = jnp.zeros((8, D_MODEL), F32).at[:BATCH].set(c).at[BATCH].set(c_ctx)
    mod = _adaln(cvec, ada_w, ada_b)
    xall = (x.reshape(R_LAT, D_MODEL), ctx.reshape(R_CTX, D_MODEL))

    sh1, sc1, g1, sh2, sc2, g2 = _split_mod(mod[0])
    z = _inproj(xall, norm1_g[0], sh1, sc1, ab_w_in[0].astype(BF16), tn=AB_IN)
    a_all = (_na_attention(z, _na_bias_tiles(na_rpb[0])), _ctx_attention(z))
    bias_t = jnp.repeat(sgu_b[0].T, SG_GROUP_DIM, axis=1)
    bsg = _sgu(z, sgu_norm_g[0], sgu_w[0].astype(BF16), bias_t)
    w_out = ab_w_out[0].astype(BF16)
    x1, h2, aff = _outproj([a_all, bsg], [w_out[:NA_WIDTH], w_out[NA_WIDTH:]], xall, g1,
                           norm2_g[0], sh2, sc2, _router_t(moe_router[0]), R_ALL)
    xall = _expert_choice(h2, x1, aff, g2, moe_w_gate, moe_w_up, moe_w_down, 0, with_ctx=True)

    sh1, sc1, g1, sh2, sc2, g2 = _split_mod(mod[1])
    z = _inproj_ret(xall, norm1_g[1], sh1, sc1, ret_w_in[0].astype(BF16), *_rope_tables())
    log_gamma = jax.nn.log_sigmoid(ret_decay_logit[0].astype(F32))
    o_back = _retention_pass(log_gamma, z, None)
    ypre = _retention_pass(log_gamma, z, o_back)
    x1, h2, aff = _outproj([ypre], [ret_w_out[0].astype(BF16)], xall, g1,
                           norm2_g[1], sh2, sc2, _router_t(moe_router[1]), R_LAT)
    x2 = _expert_choice(h2, x1, aff, g2, moe_w_gate, moe_w_up, moe_w_down, 1, with_ctx=False)
    return _final_norm(x2, final_norm_g).reshape(BATCH, SEQ, D_MODEL)
```

```python
import functools
from typing import NamedTuple

import numpy as np
import jax
import jax.numpy as jnp
from jax import lax
from jax.experimental import pallas as pl
from jax.experimental.pallas import tpu as pltpu

F32 = jnp.float32
BF16 = jnp.bfloat16

D_MODEL = 1024
BATCH = 2
SEQ = 8192
DEPTH = 2
GRID_W = 64
CTX_LEN = 256
EPS = 1e-6
NEG_INF = -1e30

NA_HEADS = 8
NA_HEAD_DIM = 64
NA_WIN_H = 8
NA_WIN_W = 16
NA_WIDTH = NA_HEADS * NA_HEAD_DIM
SG_GROUPS = 8
SG_GROUP_DIM = 64
SG_CHUNK = 128
SG_WIDTH = SG_GROUPS * SG_GROUP_DIM
AB_IN = 3 * NA_WIDTH + 2 * SG_WIDTH

RET_HEADS = 4
RET_QK_DIM = 256
RET_V_DIM = 512
RET_QK_WIDTH = RET_HEADS * RET_QK_DIM
RET_V_WIDTH = RET_HEADS * RET_V_DIM
RET_IN = 2 * RET_QK_WIDTH + 2 * RET_V_WIDTH
ROPE_BASE = 10000.0

N_EXPERTS = 16
EC_CAPACITY = 2
D_FF_EXPERT = 2816

ROWS = SEQ // GRID_W
R_LAT = BATCH * SEQ
R_CTX = BATCH * CTX_LEN
R_ALL = R_LAT + R_CTX
N_GROUPS = BATCH + 1

LANES = 128
TM = 512
VMEM_LIMIT = 56 * 1024 * 1024

NA_QROWS = 8
NA_KROWS = 16
NA_QT = NA_QROWS * GRID_W
NA_KT = 256
NA_NKB = NA_KROWS * GRID_W // NA_KT
NA_PAIRS_PER_STEP = 2
RET_CHUNK = 256
RET_HEADS_PER_STEP = 4
MOE_TF = 256
MOE_NF = D_FF_EXPERT // MOE_TF
MOE_ROW_SPLIT = 4
MOE_WRITE_STEPS = 4
MOE_DMA_UNROLL = 8
MOE_UPDATE_ROWS = 64
ROUTE_PREFIX_BLOCK = 256
ROUTE_SLOT_HI = 8


def _group_of_tile(i):
    return jnp.minimum(i // (SEQ // TM), BATCH)


def _cparams(sem, vmem=VMEM_LIMIT):
    return pltpu.CompilerParams(dimension_semantics=sem, vmem_limit_bytes=vmem)


def _dot(a, b):
    return jnp.dot(a, b, preferred_element_type=F32)


def _dot_nt(a, b):
    return lax.dot_general(a, b, (((1,), (1,)), ((), ())), preferred_element_type=F32)


def _dot_tn(a, b):
    return lax.dot_general(a, b, (((0,), (0,)), ((), ())), preferred_element_type=F32)


def _sigmoid(x):
    return 1.0 / (1.0 + jnp.exp(-x))


def _gelu_tanh(x):
    c = np.float32(np.sqrt(2.0 / np.pi))
    return 0.5 * x * (1.0 + jnp.tanh(c * (x + np.float32(0.044715) * (x * x * x))))


def _adaln_kernel(c_ref, w_ref, b_ref, o_ref):
    cv = c_ref[...]
    s = (cv * _sigmoid(cv)).astype(BF16)
    o_ref[...] = _dot(s, w_ref[...].astype(BF16)) + b_ref[...]


def _adaln(cvec, ada_w, ada_b):
    tn = 1024
    n = 6 * D_MODEL
    return pl.pallas_call(
        _adaln_kernel,
        grid=(DEPTH, n // tn),
        in_specs=[
            pl.BlockSpec((8, D_MODEL), lambda l, j: (0, 0)),
            pl.BlockSpec((None, D_MODEL, tn), lambda l, j: (l, 0, j)),
            pl.BlockSpec((None, 1, tn), lambda l, j: (l, 0, j)),
        ],
        out_specs=pl.BlockSpec((None, 8, tn), lambda l, j: (l, 0, j)),
        out_shape=jax.ShapeDtypeStruct((DEPTH, 8, n), F32),
        compiler_params=_cparams(("arbitrary", "arbitrary")),
        name="adaln",
    )(cvec, ada_w, ada_b.reshape(DEPTH, 1, n))


NSUB = D_MODEL // LANES


def _read_token_tiled(ref, start, rows):
    return jnp.concatenate(
        [ref[pl.ds(start * NSUB + s, rows, stride=NSUB), :] for s in range(NSUB)], axis=-1)


def _write_token_tiled(ref, start, val):
    for s in range(NSUB):
        ref[pl.ds(start * NSUB + s, val.shape[0], stride=NSUB), :] = val[:, s * LANES:(s + 1) * LANES]


def _read_rows(x_ref):
    if x_ref.shape[-1] != LANES:
        return x_ref[...]
    return _read_token_tiled(x_ref, 0, x_ref.shape[0] // NSUB)


def _row_spec(x, index_map):
    if x.shape[-1] != LANES:
        return pl.BlockSpec((TM, x.shape[-1]), lambda *a: (index_map(*a), 0))
    return pl.BlockSpec((TM * NSUB, LANES), lambda *a: (index_map(*a), 0))


def _row_specs(x, index_map):
    if not isinstance(x, tuple):
        return [_row_spec(x, index_map)]
    lat, ctx = x
    lat_tiles = lat.shape[0] // TM
    assert lat.shape[0] == R_LAT and ctx.shape[0] == TM and lat.shape[1] == ctx.shape[1]
    width = lat.shape[1]
    return [pl.BlockSpec((TM, width), lambda *a: (jnp.minimum(index_map(*a), lat_tiles - 1), 0)),
            pl.BlockSpec((TM, width), lambda *a: (0, 0))]


def _read_row_operand(refs, tile):
    if len(refs) == 1:
        return _read_rows(refs[0])
    lat_ref, ctx_ref = refs
    return jnp.where(tile < R_LAT // TM, lat_ref[...], ctx_ref[...])


def _inproj_kernel(*refs, n_x):
    x_refs = refs[:n_x]
    g_ref, sh_ref, sc_ref, w_ref, z_ref, hb_ref = refs[n_x:]

    @pl.when(pl.program_id(1) == 0)
    def _():
        x = _read_row_operand(x_refs, pl.program_id(0))
        y = x * lax.rsqrt(jnp.mean(x * x, axis=-1, keepdims=True) + EPS) * g_ref[...]
        hb_ref[...] = (y * (1.0 + sc_ref[...]) + sh_ref[...]).astype(BF16)

    z_ref[...] = _dot(hb_ref[...], w_ref[...]).astype(z_ref.dtype)


def _inproj(xall, norm_g, shift, scale, w_bf16, tn):
    n = w_bf16.shape[1]
    grp = lambda i, j: (_group_of_tile(i), 0, 0)
    x_specs = _row_specs(xall, lambda i, j: i)
    x_args = xall if isinstance(xall, tuple) else (xall,)
    return pl.pallas_call(
        functools.partial(_inproj_kernel, n_x=len(x_specs)),
        grid=(R_ALL // TM, n // tn),
        in_specs=x_specs + [
            pl.BlockSpec((1, D_MODEL), lambda i, j: (0, 0)),
            pl.BlockSpec((None, 1, D_MODEL), grp),
            pl.BlockSpec((None, 1, D_MODEL), grp),
            pl.BlockSpec((D_MODEL, tn), lambda i, j: (0, j)),
        ],
        out_specs=pl.BlockSpec((TM, tn), lambda i, j: (i, j)),
        out_shape=jax.ShapeDtypeStruct((R_ALL, n), BF16),
        scratch_shapes=[pltpu.VMEM((TM, D_MODEL), BF16)],
        compiler_params=_cparams(("arbitrary", "arbitrary")),
        name="inproj",
    )(*x_args, norm_g.reshape(1, D_MODEL), shift, scale, w_bf16)


def _rope(t, cos, sin):
    half = RET_QK_DIM // 2
    t1, t2 = t[:, :half], t[:, half:]
    return jnp.concatenate([t1 * cos - t2 * sin, t1 * sin + t2 * cos], axis=-1)


def _inproj_ret_kernel(x_ref, g_ref, sh_ref, sc_ref, cos_ref, sin_ref, w_ref, z_ref):
    x = _read_rows(x_ref)
    y = x * lax.rsqrt(jnp.mean(x * x, axis=-1, keepdims=True) + EPS) * g_ref[...]
    hb = (y * (1.0 + sc_ref[...]) + sh_ref[...]).astype(BF16)
    cos, sin = cos_ref[...], sin_ref[...]
    kscale = np.float32(RET_QK_DIM ** -0.5)
    for h in range(2 * RET_HEADS):
        cols = slice(h * RET_QK_DIM, (h + 1) * RET_QK_DIM)
        t = _rope(_dot(hb, w_ref[:, cols]), cos, sin)
        z_ref[:, cols] = (t if h < RET_HEADS else t * kscale).astype(z_ref.dtype)
    rest = slice(2 * RET_QK_WIDTH, RET_IN)
    z_ref[:, rest] = _dot(hb, w_ref[:, rest]).astype(z_ref.dtype)


def _inproj_ret(xall, norm_g, shift, scale, w_bf16, cos, sin):
    grp = lambda i: (_group_of_tile(i), 0, 0)
    lat_tiles = SEQ // TM
    pos = lambda i: (jnp.where(i < BATCH * lat_tiles, i % lat_tiles, lat_tiles), 0)
    half = RET_QK_DIM // 2
    return pl.pallas_call(
        _inproj_ret_kernel,
        grid=(R_ALL // TM,),
        in_specs=[
            _row_spec(xall, lambda i: i),
            pl.BlockSpec((1, D_MODEL), lambda i: (0, 0)),
            pl.BlockSpec((None, 1, D_MODEL), grp),
            pl.BlockSpec((None, 1, D_MODEL), grp),
            pl.BlockSpec((TM, half), pos),
            pl.BlockSpec((TM, half), pos),
            pl.BlockSpec((D_MODEL, RET_IN), lambda i: (0, 0)),
        ],
        out_specs=pl.BlockSpec((TM, RET_IN), lambda i: (i, 0)),
        out_shape=jax.ShapeDtypeStruct((R_ALL, RET_IN), BF16),
        compiler_params=_cparams(("arbitrary",)),
        name="inproj_ret",
    )(xall, norm_g.reshape(1, D_MODEL), shift, scale, cos, sin, w_bf16)


def _na_tile_codes():
    masked = 2 * NA_WIN_H - 1
    kinds = (0, 1, ROWS // NA_QROWS - 1)
    d = np.full((3, NA_QROWS, NA_KROWS), masked, np.int64)
    for v, j in enumerate(kinds):
        kb = int(np.clip(NA_QROWS * j - NA_WIN_H // 2, 0, ROWS - NA_KROWS))
        for a in range(NA_QROWS):
            i = NA_QROWS * j + a
            r0 = int(np.clip(i - NA_WIN_H // 2, 0, ROWS - NA_WIN_H))
            for r in range(NA_KROWS):
                if r0 <= kb + r < r0 + NA_WIN_H:
                    d[v, a, r] = kb + r - i + NA_WIN_H - 1
    pairs = d.reshape(-1, 2)
    uniq = sorted(set(map(tuple, pairs)))
    code = np.array([uniq.index(tuple(p)) for p in pairs], np.int32)
    return code, np.array(uniq, np.int64)


_NA_CODES, _NA_CODE_ROWS = _na_tile_codes()


def _na_bias_tiles(rpb):
    qc = np.arange(GRID_W)[:, None]
    kc = np.arange(GRID_W)[None, :]
    cstart = np.clip(qc - NA_WIN_W // 2, 0, GRID_W - NA_WIN_W)
    col_ok = (kc >= cstart) & (kc < cstart + NA_WIN_W)
    dcol = np.clip(kc - qc, 1 - NA_WIN_W, NA_WIN_W - 1) + NA_WIN_W - 1
    onehot = (dcol.reshape(1, -1) == np.arange(2 * NA_WIN_W - 1)[:, None]).astype(np.float32)
    toe = jnp.einsum('hdm,mq->hdq', rpb.astype(F32), jnp.asarray(onehot),
                     precision=lax.Precision.HIGHEST)
    toe = toe.reshape(NA_HEADS, 2 * NA_WIN_H - 1, GRID_W, GRID_W)
    toe = jnp.where(jnp.asarray(col_ok), toe, NEG_INF)
    toe = jnp.concatenate([toe, jnp.full((NA_HEADS, 1, GRID_W, GRID_W), NEG_INF, F32)], axis=1)
    return jnp.concatenate([toe[:, _NA_CODE_ROWS[:, 0]], toe[:, _NA_CODE_ROWS[:, 1]]], axis=-1)


def _pair_attention(q, keys, vals, add_bias=None):
    lane = lax.broadcasted_iota(jnp.int32, (1, LANES), 1)
    scale = NA_HEAD_DIM ** -0.5
    out = jnp.zeros((q.shape[0], LANES), F32)
    for hh in range(2):
        in_head = (lane < NA_HEAD_DIM) if hh == 0 else (lane >= NA_HEAD_DIM)
        qm = jnp.where(in_head, q, jnp.zeros_like(q)) * jnp.asarray(scale, q.dtype)
        s = []
        for t, k in enumerate(keys):
            st = _dot_nt(qm, k)
            if add_bias is not None:
                st = add_bias(hh, t, st)
            s.append(st)
        m = s[0].max(axis=-1, keepdims=True)
        for st in s[1:]:
            m = jnp.maximum(m, st.max(axis=-1, keepdims=True))
        den = jnp.zeros_like(m)
        acc = jnp.zeros((q.shape[0], LANES), F32)
        for st, v in zip(s, vals):
            p = jnp.exp(st - m)
            den = den + p.sum(axis=-1, keepdims=True)
            acc = acc + _dot(p.astype(BF16), v)
        out = jnp.where(in_head, acc / den, out)
    return out


def _na_kernel(code_ref, q_ref, k0, k1, k2, k3, v0, v1, v2, v3, kc_ref, vc_ref, tile_ref, o_ref):
    nj = pl.num_programs(1)
    j = pl.program_id(1)
    kind = jnp.minimum(j, 1) + jnp.maximum(j - (nj - 2), 0)
    tiles_per_row = NA_KROWS // 2
    code_base = kind * (NA_QROWS * tiles_per_row)
    for pp in range(NA_PAIRS_PER_STEP):
        lanes = slice(pp * LANES, (pp + 1) * LANES)
        keys = [r[:, lanes] for r in (k0, k1, k2, k3, kc_ref)]
        vals = [r[:, lanes] for r in (v0, v1, v2, v3, vc_ref)]

        def add_bias(hh, t, st, pp=pp):
            if t >= NA_NKB:
                return st
            rows = []
            for a in range(NA_QROWS):
                cols = []
                for u in range(NA_KT // LANES):
                    code = code_ref[code_base + a * tiles_per_row + t * (NA_KT // LANES) + u]
                    piece = st[a * GRID_W:(a + 1) * GRID_W, u * LANES:(u + 1) * LANES]
                    cols.append(piece + tile_ref[2 * pp + hh, code])
                rows.append(jnp.concatenate(cols, axis=1))
            return jnp.concatenate(rows, axis=0)

        o_ref[:, lanes] = _pair_attention(q_ref[:, lanes], keys, vals, add_bias).astype(o_ref.dtype)


def _na_attention(z, tiles):
    nj = ROWS // NA_QROWS
    kblocks = SEQ // NA_KT
    width = NA_PAIRS_PER_STEP * LANES
    qcol, kcol, vcol = 0, NA_WIDTH // width, 2 * NA_WIDTH // width
    n_codes = tiles.shape[1]

    def kbase(j):
        return jnp.clip(2 * j - 1, 0, kblocks - NA_NKB)

    def kspec(t, col):
        return pl.BlockSpec((NA_KT, width),
                            lambda p, j, b, code: (b * kblocks + kbase(j) + t, col + p))

    ctx_row = lambda b: R_LAT // CTX_LEN + b
    in_specs = (
        [pl.BlockSpec((NA_QT, width), lambda p, j, b, code: (b * nj + j, qcol + p))]
        + [kspec(t, kcol) for t in range(NA_NKB)]
        + [kspec(t, vcol) for t in range(NA_NKB)]
        + [pl.BlockSpec((CTX_LEN, width), lambda p, j, b, code: (ctx_row(b), kcol + p)),
           pl.BlockSpec((CTX_LEN, width), lambda p, j, b, code: (ctx_row(b), vcol + p)),
           pl.BlockSpec((2 * NA_PAIRS_PER_STEP, n_codes, GRID_W, LANES),
                        lambda p, j, b, code: (p, 0, 0, 0))])
    return pl.pallas_call(
        _na_kernel,
        grid_spec=pltpu.PrefetchScalarGridSpec(
            num_scalar_prefetch=1,
            grid=(NA_HEADS // (2 * NA_PAIRS_PER_STEP), nj, BATCH),
            in_specs=in_specs,
            out_specs=pl.BlockSpec((NA_QT, width), lambda p, j, b, code: (b * nj + j, p))),
        out_shape=jax.ShapeDtypeStruct((R_LAT, NA_WIDTH), BF16),
        compiler_params=_cparams(("arbitrary", "arbitrary", "arbitrary")),
        name="na_attention",
    )(jnp.asarray(_NA_CODES), *([z] * (3 + 2 * NA_NKB)), tiles)


def _ctx_attn_kernel(q_ref, k_ref, v_ref, o_ref):
    o_ref[...] = _pair_attention(q_ref[...], [k_ref[...]], [v_ref[...]]).astype(o_ref.dtype)


def _ctx_attention(z):
    qcol, kcol, vcol = 0, NA_WIDTH // LANES, 2 * NA_WIDTH // LANES
    row = lambda b: R_LAT // CTX_LEN + b
    return pl.pallas_call(
        _ctx_attn_kernel,
        grid=(BATCH, NA_HEADS // 2),
        in_specs=[pl.BlockSpec((CTX_LEN, LANES), lambda b, p: (row(b), qcol + p)),
                  pl.BlockSpec((CTX_LEN, LANES), lambda b, p: (row(b), kcol + p)),
                  pl.BlockSpec((CTX_LEN, LANES), lambda b, p: (row(b), vcol + p))],
        out_specs=pl.BlockSpec((CTX_LEN, LANES), lambda b, p: (b, p)),
        out_shape=jax.ShapeDtypeStruct((R_CTX, NA_WIDTH), BF16),
        compiler_params=_cparams(("arbitrary", "arbitrary")),
        name="ctx_attention",
    )(z, z, z)


def _sgu_kernel(u_ref, v_ref, ng_ref, w_ref, bt_ref, o_ref):
    lane = lax.broadcasted_iota(jnp.int32, (1, LANES), 1)
    first = lane < SG_GROUP_DIM
    for ch in range(TM // SG_CHUNK):
        rows = slice(ch * SG_CHUNK, (ch + 1) * SG_CHUNK)
        u = _gelu_tanh(u_ref[rows, :].astype(F32))
        v = _gelu_tanh(v_ref[rows, :].astype(F32))
        vn = v * lax.rsqrt(jnp.mean(v * v, axis=-1, keepdims=True) + EPS) * ng_ref[...]
        vnb = vn.astype(BF16)
        for pr in range(SG_GROUPS // 2):
            cols = slice(pr * LANES, (pr + 1) * LANES)
            slab = vnb[:, cols]
            mixed = jnp.where(first, _dot(w_ref[2 * pr], slab), _dot(w_ref[2 * pr + 1], slab))
            o_ref[rows, cols] = (u[:, cols] * (mixed + bt_ref[:, cols])).astype(o_ref.dtype)


def _sgu(z, norm_g, w_bf16, bias_t):
    ucol = 3 * NA_WIDTH // SG_WIDTH
    return pl.pallas_call(
        _sgu_kernel,
        grid=(R_ALL // TM,),
        in_specs=[pl.BlockSpec((TM, SG_WIDTH), lambda i: (i, ucol)),
                  pl.BlockSpec((TM, SG_WIDTH), lambda i: (i, ucol + 1)),
                  pl.BlockSpec((1, SG_WIDTH), lambda i: (0, 0)),
                  pl.BlockSpec((SG_GROUPS, SG_CHUNK, SG_CHUNK), lambda i: (0, 0, 0)),
                  pl.BlockSpec((SG_CHUNK, SG_WIDTH), lambda i: (0, 0))],
        out_specs=pl.BlockSpec((TM, SG_WIDTH), lambda i: (i, 0)),
        out_shape=jax.ShapeDtypeStruct((R_ALL, SG_WIDTH), BF16),
        compiler_params=_cparams(("arbitrary",)),
        name="sgu",
    )(z, z, norm_g.reshape(1, SG_WIDTH), w_bf16, bias_t)


def _outproj_kernel(*refs, operand_refs):
    tile = pl.program_id(0)
    operands, pos = [], 0
    for n in operand_refs:
        operands.append(_read_row_operand(refs[pos:pos + n], tile))
        pos += n
    lhs, x = operands[:-1], operands[-1]
    ws = refs[pos:pos + len(lhs)]
    g1_ref, ng_ref, sh_ref, sc_ref, wr_ref, xo_ref, h2_ref, aff_ref = refs[pos + len(lhs):]
    y = _dot(lhs[0], ws[0][...])
    for a, w in zip(lhs[1:], ws[1:]):
        y = y + _dot(a, w[...])
    xn = x + g1_ref[...] * y
    _write_token_tiled(xo_ref, 0, xn)
    hn = xn * lax.rsqrt(jnp.mean(xn * xn, axis=-1, keepdims=True) + EPS) * ng_ref[...]
    h2 = hn * (1.0 + sc_ref[...]) + sh_ref[...]
    _write_token_tiled(h2_ref, 0, h2)
    logits = _dot_nt(wr_ref[...], h2.astype(BF16))
    e = jnp.exp(logits - logits.max(axis=0, keepdims=True))
    aff_ref[...] = e / e.sum(axis=0, keepdims=True)


def _outproj(lhs_list, w_list, xall, gate1, norm_g, shift, scale, w_router_pad, n_rows):
    grp = lambda i: (_group_of_tile(i), 0, 0)
    row = lambda i: (i, 0)
    const2 = lambda i: (0, 0)
    row_operands = list(lhs_list) + [xall]
    operand_specs = [_row_specs(a, lambda i: i) for a in row_operands]
    row_args = [t for a in row_operands for t in (a if isinstance(a, tuple) else (a,))]
    in_specs = (
        [s for specs in operand_specs for s in specs]
        + [pl.BlockSpec(w.shape, const2) for w in w_list]
        + [pl.BlockSpec((None, 1, D_MODEL), grp),
           pl.BlockSpec((1, D_MODEL), const2),
           pl.BlockSpec((None, 1, D_MODEL), grp),
           pl.BlockSpec((None, 1, D_MODEL), grp),
           pl.BlockSpec((N_EXPERTS, D_MODEL), const2)])
    tiled = (n_rows * NSUB, LANES)
    tiled_spec = pl.BlockSpec((TM * NSUB, LANES), row)
    return pl.pallas_call(
        functools.partial(_outproj_kernel, operand_refs=tuple(len(s) for s in operand_specs)),
        grid=(n_rows // TM,),
        in_specs=in_specs,
        out_specs=[tiled_spec, tiled_spec, pl.BlockSpec((N_EXPERTS, TM), lambda i: (0, i))],
        out_shape=[jax.ShapeDtypeStruct(tiled, F32),
                   jax.ShapeDtypeStruct(tiled, F32),
                   jax.ShapeDtypeStruct((N_EXPERTS, n_rows), F32)],
        compiler_params=_cparams(("arbitrary",)),
        name="outproj",
    )(*row_args, *w_list, gate1, norm_g.reshape(1, D_MODEL), shift, scale, w_router_pad)


def _moe_segments(rows):
    cap = EC_CAPACITY * SEQ // N_EXPERTS
    segs = [(b * cap, (b + 1) * cap, b) for b in range(BATCH)]
    if rows > BATCH * cap:
        segs.append((BATCH * cap, rows, BATCH))
    return segs


class _MoeDmaPlan(NamedTuple):
    write_steps: int
    fetch_first: int
    last_step: int
    write_per_tick: int
    fetch_per_tick: int
    fetch_slots: int
    token_per_tick: int
    token_slots: int


def _moe_dma_plan(rows):
    write_ticks = MOE_WRITE_STEPS * MOE_ROW_SPLIT
    fetch_first = MOE_WRITE_STEPS + 1
    last_step = MOE_NF - 1
    fetch_ticks = (last_step - fetch_first) * MOE_ROW_SPLIT
    token_ticks = (last_step - 1) * MOE_ROW_SPLIT
    assert rows % write_ticks == 0 and fetch_ticks > 0
    fetch_per_tick = -(-rows // fetch_ticks)
    while (fetch_per_tick * fetch_ticks) % MOE_DMA_UNROLL:
        fetch_per_tick += 1
    token_per_tick = -(-rows // token_ticks)
    while (token_per_tick * token_ticks) % MOE_DMA_UNROLL:
        token_per_tick += 1
    return _MoeDmaPlan(MOE_WRITE_STEPS, fetch_first, last_step, rows // write_ticks,
                       fetch_per_tick, fetch_per_tick * fetch_ticks,
                       token_per_tick, token_per_tick * token_ticks)


def _moe_kernel(idx_ref, h_hbm, x_hbm, gate_ref, g2_ref, wg_ref, wu_ref, wd_ref, xo_hbm,
                xg, xb, acc, res, wg_s, wu_s, wd_s, sem_h, sem_r, sem_s, *, rows):
    del x_hbm
    e, f = pl.program_id(0), pl.program_id(1)
    nf = pl.num_programs(1)
    plan = _moe_dma_plan(rows)

    def issue(count, first, n_valid, expert, make_copy):
        base = expert * rows
        for u in range(count):
            i = first + u
            n = idx_ref[base + jnp.minimum(i, n_valid - 1)]
            make_copy(pl.ds(pl.multiple_of(i * NSUB, NSUB), NSUB), n).start(priority=u % 2)

    def issue_all(n_slots, expert, make_copy):
        def group(o, carry):
            issue(MOE_DMA_UNROLL, o * MOE_DMA_UNROLL, rows, expert, make_copy)
            return carry

        lax.fori_loop(0, n_slots // MOE_DMA_UNROLL, group, 0)

    def token_copy(slot, n):
        return pltpu.make_async_copy(h_hbm.at[n], xg.at[slot, :], sem_h)

    def fetch_copy(slot, n):
        return pltpu.make_async_copy(xo_hbm.at[n], res.at[slot, :], sem_r)

    def writeback_copy(slot, n):
        return pltpu.make_async_copy(res.at[slot, :], xo_hbm.at[n], sem_s)

    all_tokens = pltpu.make_async_copy(xg, xg, sem_h)
    all_fetched = pltpu.make_async_copy(res, res, sem_r)
    written_rows = res.at[pl.ds(0, rows * NSUB), :]
    all_written = pltpu.make_async_copy(written_rows, written_rows, sem_s)

    prev = jnp.maximum(e - 1, 0)
    nxt = jnp.where(e + 1 < N_EXPERTS, e + 1, 0)

    def apply_update():
        for lo, hi, g in _moe_segments(rows):
            for r0 in range(lo, hi, MOE_UPDATE_ROWS):
                nr = min(MOE_UPDATE_ROWS, hi - r0)
                gate = gate_ref[r0:r0 + nr, :]
                sums = []
                for s in range(NSUB):
                    cols = slice(s * LANES, (s + 1) * LANES)
                    tiled = pl.ds(r0 * NSUB + s, nr, stride=NSUB)
                    sums.append(res[tiled, :] + acc[r0:r0 + nr, cols] * gate * g2_ref[g, :, cols])
                for s in range(NSUB):
                    res[pl.ds(r0 * NSUB + s, nr, stride=NSUB), :] = sums[s]

    @pl.when(f == 0)
    def _():
        @pl.when(e == 0)
        def _():
            issue_all(plan.token_slots, 0, token_copy)
            issue_all(plan.fetch_slots, 0, fetch_copy)

        all_tokens.wait()
        all_fetched.wait()

        @pl.when(e > 0)
        def _():
            apply_update()

        @pl.when(e == N_EXPERTS)
        def _():
            issue_all(rows, N_EXPERTS - 1, writeback_copy)
            all_written.wait()

    @pl.when((f == plan.fetch_first) & (e < N_EXPERTS))
    def _():
        all_written.wait()

    def step_body(write_back=False, fetch=False, gather=False, first=False):
        wg_s[...] = wg_ref[...].astype(BF16)
        wu_s[...] = wu_ref[...].astype(BF16)
        wd_s[...] = wd_ref[...].astype(BF16)
        rc = rows // MOE_ROW_SPLIT
        for r in range(MOE_ROW_SPLIT):
            sl = slice(r * rc, (r + 1) * rc)
            if first:
                for r0 in range(r * rc, (r + 1) * rc, MOE_UPDATE_ROWS):
                    nr = min(MOE_UPDATE_ROWS, (r + 1) * rc - r0)
                    xb[r0:r0 + nr, :] = _read_token_tiled(xg, r0, nr).astype(BF16)
            x = xb[sl, :]
            a = _dot(x, wg_s[...])
            b = _dot(x, wu_s[...])
            hid = (a * _sigmoid(a) * b).astype(BF16)
            if first:
                acc[sl, :] = _dot(hid, wd_s[...])
            else:
                acc[sl, :] += _dot(hid, wd_s[...])
            tick = f * MOE_ROW_SPLIT + r
            if write_back:
                issue(plan.write_per_tick, tick * plan.write_per_tick, rows, prev, writeback_copy)
            if fetch:
                slot0 = (tick - plan.fetch_first * MOE_ROW_SPLIT) * plan.fetch_per_tick
                issue(plan.fetch_per_tick, slot0, rows, e, fetch_copy)
            if gather:
                slot0 = (tick - MOE_ROW_SPLIT) * plan.token_per_tick
                issue(plan.token_per_tick, slot0, rows, nxt, token_copy)

    real = e < N_EXPERTS

    @pl.when((f == 0) & real)
    def _():
        step_body(write_back=True, first=True)

    @pl.when((f > 0) & (f < plan.write_steps) & real)
    def _():
        step_body(write_back=True, gather=True)

    @pl.when((f >= plan.write_steps) & (f < plan.fetch_first) & real)
    def _():
        step_body(gather=True)

    @pl.when((f >= plan.fetch_first) & (f < plan.last_step) & real)
    def _():
        step_body(fetch=True, gather=True)

    @pl.when((f >= plan.last_step) & real)
    def _():
        step_body()


def _moe_ffn(flat_idx, h2, x1, gates, gate2, w_gate, w_up, w_down, layer):
    rows = flat_idx.shape[1]
    assert rows % MOE_DMA_UNROLL == 0 and rows % MOE_ROW_SPLIT == 0
    plan = _moe_dma_plan(rows)
    nf = MOE_NF
    n_tokens = x1.shape[0] // NSUB
    as_tokens = lambda t: t.reshape(n_tokens, NSUB, LANES)
    ew = lambda e: jnp.minimum(e, N_EXPERTS - 1)
    fw = lambda e, f: jnp.where(e < N_EXPERTS, f, nf - 1)
    out = pl.pallas_call(
        functools.partial(_moe_kernel, rows=rows),
        grid_spec=pltpu.PrefetchScalarGridSpec(
            num_scalar_prefetch=1,
            grid=(N_EXPERTS + 1, nf),
            in_specs=[pl.BlockSpec(memory_space=pl.ANY),
                      pl.BlockSpec(memory_space=pl.ANY),
                      pl.BlockSpec((None, rows, LANES), lambda e, f, idx: (jnp.maximum(e - 1, 0), 0, 0)),
                      pl.BlockSpec((N_GROUPS, 1, D_MODEL), lambda e, f, idx: (0, 0, 0)),
                      pl.BlockSpec((None, None, D_MODEL, MOE_TF),
                                   lambda e, f, idx: (layer, ew(e), 0, fw(e, f))),
                      pl.BlockSpec((None, None, D_MODEL, MOE_TF),
                                   lambda e, f, idx: (layer, ew(e), 0, fw(e, f))),
                      pl.BlockSpec((None, None, MOE_TF, D_MODEL),
                                   lambda e, f, idx: (layer, ew(e), fw(e, f), 0))],
            out_specs=pl.BlockSpec(memory_space=pl.ANY),
            scratch_shapes=[pltpu.VMEM((plan.token_slots * NSUB, LANES), F32),
                            pltpu.VMEM((rows, D_MODEL), BF16),
                            pltpu.VMEM((rows, D_MODEL), F32),
                            pltpu.VMEM((plan.fetch_slots * NSUB, LANES), F32),
                            pltpu.VMEM((D_MODEL, MOE_TF), BF16),
                            pltpu.VMEM((D_MODEL, MOE_TF), BF16),
                            pltpu.VMEM((MOE_TF, D_MODEL), BF16),
                            pltpu.SemaphoreType.DMA,
                            pltpu.SemaphoreType.DMA,
                            pltpu.SemaphoreType.DMA]),
        out_shape=jax.ShapeDtypeStruct((n_tokens, NSUB, LANES), F32),
        input_output_aliases={2: 0},
        compiler_params=_cparams(("arbitrary", "arbitrary")),
        name="moe_ffn",
    )(flat_idx.reshape(-1), as_tokens(h2), as_tokens(x1), gates, gate2, w_gate, w_up, w_down)
    return out.reshape(x1.shape)


def _exclusive_prefix(mask, upper):
    out = []
    run = jnp.zeros((mask.shape[0], 1), F32)
    for c in range(mask.shape[1] // ROUTE_PREFIX_BLOCK):
        m = mask[:, c * ROUTE_PREFIX_BLOCK:(c + 1) * ROUTE_PREFIX_BLOCK]
        out.append(_dot(m.astype(BF16), upper) + run)
        run = run + jnp.sum(m, axis=1, keepdims=True)
    return jnp.concatenate(out, axis=1)


def _route_kernel(aff_ref, idx_ref, gate_ref, pos3, sel3, val3, *, cap):
    n = aff_ref.shape[1]
    v = aff_ref[...]
    thr = jnp.zeros((N_EXPERTS, 1), jnp.int32)
    for bit in range(30, -1, -1):
        cand = thr | (1 << bit)
        cnt = jnp.sum((v >= pltpu.bitcast(cand, F32)).astype(jnp.int32), axis=1, keepdims=True)
        thr = jnp.where(cnt >= cap, cand, thr)
    ri = lax.broadcasted_iota(jnp.int32, (ROUTE_PREFIX_BLOCK, ROUTE_PREFIX_BLOCK), 0)
    ci = lax.broadcasted_iota(jnp.int32, (ROUTE_PREFIX_BLOCK, ROUTE_PREFIX_BLOCK), 1)
    upper = (ri < ci).astype(BF16)
    gt = v >= pltpu.bitcast(thr + 1, F32)
    eq = (v >= pltpu.bitcast(thr, F32)) & ~gt
    need = (cap - jnp.sum(gt.astype(jnp.int32), axis=1, keepdims=True)).astype(F32)
    sel = gt | (eq & (_exclusive_prefix(eq.astype(F32), upper) < need))
    sel_f = sel.astype(F32)
    pos = _exclusive_prefix(sel_f, upper)
    for e in range(N_EXPERTS):
        pos3[e] = pos[e:e + 1, :].astype(jnp.int32)
        sel3[e] = sel_f[e:e + 1, :]
        val3[e] = v[e:e + 1, :]

    tok = lax.broadcasted_iota(jnp.int32, (1, n), 1)
    tok_hi = (tok >> 7).astype(F32)
    tok_lo = (tok & (LANES - 1)).astype(F32)
    hi_row = lax.broadcasted_iota(jnp.int32, (ROUTE_SLOT_HI, 1), 0)
    lo_row = lax.broadcasted_iota(jnp.int32, (LANES, 1), 0)

    def per_expert(e, carry):
        p, s, a = pos3[e], sel3[e], val3[e]
        a1 = a.astype(BF16).astype(F32)
        a2 = (a - a1).astype(BF16).astype(F32)
        a3 = a - a1 - a2
        hot_hi = jnp.where((p >> 7) == hi_row, s, 0.0)
        hot_lo = ((p & (LANES - 1)) == lo_row).astype(BF16)
        lhs = jnp.concatenate([hot_hi * tok_hi, hot_hi * tok_lo,
                               hot_hi * a1, hot_hi * a2, hot_hi * a3], axis=0).astype(BF16)
        out = _dot_nt(lhs, hot_lo)
        h = ROUTE_SLOT_HI
        idx_ref[e] = (out[0:h] * float(LANES) + out[h:2 * h]).astype(jnp.int32)
        gate_ref[e] = (out[2 * h:3 * h] + out[3 * h:4 * h]) + out[4 * h:5 * h]
        return carry

    lax.fori_loop(0, N_EXPERTS, per_expert, 0)


def _route(aff_t, n, cap, first_block, n_sets):
    assert cap <= ROUTE_SLOT_HI * LANES and n % ROUTE_PREFIX_BLOCK == 0 and n < (1 << 14)
    shape = (n_sets, N_EXPERTS, ROUTE_SLOT_HI, LANES)
    spec = pl.BlockSpec((None, N_EXPERTS, ROUTE_SLOT_HI, LANES), lambda s: (s, 0, 0, 0))
    return pl.pallas_call(
        functools.partial(_route_kernel, cap=cap),
        grid=(n_sets,),
        in_specs=[pl.BlockSpec((N_EXPERTS, n), lambda s: (0, first_block + s))],
        out_specs=[spec, spec],
        out_shape=[jax.ShapeDtypeStruct(shape, jnp.int32), jax.ShapeDtypeStruct(shape, F32)],
        scratch_shapes=[pltpu.VMEM((N_EXPERTS, 1, n), jnp.int32),
                        pltpu.VMEM((N_EXPERTS, 1, n), F32),
                        pltpu.VMEM((N_EXPERTS, 1, n), F32)],
        compiler_params=_cparams(("arbitrary",)),
        name="route",
    )(aff_t)


def _expert_choice(h2, x1, aff_t, gate2, w_gate, w_up, w_down, layer, with_ctx):
    def slots(idx, gate, cap, row0, set_len):
        idx = idx.reshape(BATCH, N_EXPERTS, -1)[:, :, :cap]
        gate = gate.reshape(BATCH, N_EXPERTS, -1)[:, :, :cap]
        idx = idx + row0 + (jnp.arange(BATCH, dtype=jnp.int32) * set_len)[:, None, None]
        return (jnp.swapaxes(idx, 0, 1).reshape(N_EXPERTS, BATCH * cap),
                jnp.swapaxes(gate, 0, 1).reshape(N_EXPERTS, BATCH * cap))

    cap = EC_CAPACITY * SEQ // N_EXPERTS
    flat, gate = slots(*_route(aff_t, SEQ, cap, 0, BATCH), cap, 0, SEQ)
    if with_ctx:
        cap_c = EC_CAPACITY * CTX_LEN // N_EXPERTS
        flat_c, gate_c = slots(*_route(aff_t, CTX_LEN, cap_c, R_LAT // CTX_LEN, BATCH),
                               cap_c, R_LAT, CTX_LEN)
        flat = jnp.concatenate([flat, flat_c], axis=1)
        gate = jnp.concatenate([gate, gate_c], axis=1)
    gate = jnp.broadcast_to(gate[..., None], gate.shape + (LANES,))
    return _moe_ffn(flat, h2, x1, gate, gate2, w_gate, w_up, w_down, layer)


def _final_norm_kernel(x_ref, ng_ref, o_ref):
    x = _read_rows(x_ref)
    o_ref[...] = x * lax.rsqrt(jnp.mean(x * x, axis=-1, keepdims=True) + EPS) * ng_ref[...]


def _final_norm(x, norm_g):
    n_rows = x.size // D_MODEL
    return pl.pallas_call(
        _final_norm_kernel,
        grid=(n_rows // TM,),
        in_specs=[_row_spec(x, lambda i: i),
                  pl.BlockSpec((1, D_MODEL), lambda i: (0, 0))],
        out_specs=pl.BlockSpec((TM, D_MODEL), lambda i: (i, 0)),
        out_shape=jax.ShapeDtypeStruct((n_rows, D_MODEL), F32),
        compiler_params=_cparams(("arbitrary",)),
        name="final_norm",
    )(x, norm_g.reshape(1, D_MODEL))


def _retention_kernel(lg_ref, q_ref, k_ref, v_ref, kc_ref, vc_ref, *rest, backward):
    if backward:
        o_ref, s_ref, dmat_ref, qd_ref, kd_ref = rest
    else:
        ob_ref, g_ref, o_ref, s_ref, dmat_ref, qd_ref, kd_ref = rest
    c = RET_CHUNK
    heads = range(RET_HEADS_PER_STEP)
    lgs = [lg_ref[1 if backward else 0, pl.program_id(1) * RET_HEADS_PER_STEP + hh] for hh in heads]
    qks = [slice(hh * RET_QK_DIM, (hh + 1) * RET_QK_DIM) for hh in heads]
    vvs = [slice(hh * RET_V_DIM, (hh + 1) * RET_V_DIM) for hh in heads]

    @pl.when(pl.program_id(2) == 0)
    def _():
        ia = lax.broadcasted_iota(jnp.int32, (c, c), 0)
        ib = lax.broadcasted_iota(jnp.int32, (c, c), 1)
        diff = ((ib - ia) if backward else (ia - ib)).astype(F32)
        pos = lax.broadcasted_iota(jnp.int32, (c, RET_QK_DIM), 0).astype(F32)
        cpos = lax.broadcasted_iota(jnp.int32, (CTX_LEN, RET_QK_DIM), 0).astype(F32)
        for hh, lg in zip(heads, lgs):
            dmat_ref[hh] = jnp.where(diff >= 0, jnp.exp(jnp.maximum(diff, 0.0) * lg), 0.0)
            if backward:
                qd_ref[hh] = jnp.exp((c - pos) * lg)
                kd_ref[hh] = jnp.exp(pos * lg)
            else:
                qd_ref[hh] = jnp.exp((pos + 1.0) * lg)
                kd_ref[hh] = jnp.exp((c - 1.0 - pos) * lg)
            cw = jnp.exp(cpos * lg) if backward else jnp.exp((CTX_LEN - 1.0 - cpos) * lg)
            kcw = (kc_ref[:, qks[hh]].astype(F32) * cw).astype(BF16)
            s_ref[hh] = _dot_tn(kcw, vc_ref[:, vvs[hh]])

    for hh, lg, qk, vv in zip(heads, lgs, qks, vvs):
        qb = q_ref[:, qk]
        kb = k_ref[:, qk]
        v = v_ref[:, vv]
        att = _dot_nt(qb, kb) * dmat_ref[hh]
        s_old = s_ref[hh]
        o = (_dot(att.astype(BF16), v)
             + _dot((qb.astype(F32) * qd_ref[hh]).astype(BF16), s_old.astype(BF16)))
        chunk_decay = jnp.exp(jnp.zeros((1, RET_V_DIM), F32) + c * lg)
        s_ref[hh] = s_old * chunk_decay + _dot_tn((kb.astype(F32) * kd_ref[hh]).astype(BF16), v)
        if backward:
            o_ref[:, vv] = o.astype(o_ref.dtype)
        else:
            of = o + ob_ref[:, vv].astype(F32)
            y = of * lax.rsqrt(jnp.mean(of * of, axis=-1, keepdims=True) + EPS)
            g = g_ref[:, vv].astype(F32)
            o_ref[:, vv] = (g * _sigmoid(g) * y).astype(o_ref.dtype)


def _retention_pass(log_gamma, z, o_back):
    backward = o_back is None
    c = RET_CHUNK
    n = SEQ // c
    hp = RET_HEADS_PER_STEP
    qk_w, v_w = hp * RET_QK_DIM, hp * RET_V_DIM
    qk_cols = RET_QK_WIDTH // qk_w
    v_base = 2 * RET_QK_WIDTH // v_w
    g_base = v_base + RET_V_WIDTH // v_w
    chunk = (lambda t: n - 1 - t) if backward else (lambda t: t)
    row = lambda b, t: b * n + chunk(t)
    ctx_row = lambda b: R_LAT // CTX_LEN + b
    in_specs = [
        pl.BlockSpec((c, qk_w), lambda b, h, t, lg: (row(b, t), h)),
        pl.BlockSpec((c, qk_w), lambda b, h, t, lg: (row(b, t), qk_cols + h)),
        pl.BlockSpec((c, v_w), lambda b, h, t, lg: (row(b, t), v_base + h)),
        pl.BlockSpec((CTX_LEN, qk_w), lambda b, h, t, lg: (ctx_row(b), qk_cols + h)),
        pl.BlockSpec((CTX_LEN, v_w), lambda b, h, t, lg: (ctx_row(b), v_base + h)),
    ]
    args = [z, z, z, z, z]
    if not backward:
        in_specs += [
            pl.BlockSpec((c, v_w), lambda b, h, t, lg: (row(b, t), h)),
            pl.BlockSpec((c, v_w), lambda b, h, t, lg: (row(b, t), g_base + h)),
        ]
        args += [o_back, z]
    return pl.pallas_call(
        functools.partial(_retention_kernel, backward=backward),
        grid_spec=pltpu.PrefetchScalarGridSpec(
            num_scalar_prefetch=1,
            grid=(BATCH, RET_HEADS // hp, n),
            in_specs=in_specs,
            out_specs=pl.BlockSpec((c, v_w), lambda b, h, t, lg: (row(b, t), h)),
            scratch_shapes=[pltpu.VMEM((hp, RET_QK_DIM, RET_V_DIM), F32),
                            pltpu.VMEM((hp, c, c), F32),
                            pltpu.VMEM((hp, c, RET_QK_DIM), F32),
                            pltpu.VMEM((hp, c, RET_QK_DIM), F32)]),
        out_shape=jax.ShapeDtypeStruct((R_LAT, RET_V_WIDTH), BF16),
        compiler_params=_cparams(("arbitrary", "arbitrary", "arbitrary")),
        name="retention_bwd" if backward else "retention_fwd",
    )(log_gamma, *args)


def _rope_tables():
    axis_dim = RET_QK_DIM // 2
    inv = (1.0 / (np.float32(ROPE_BASE) ** (np.arange(0, axis_dim, 2, dtype=np.float32)
                                             / np.float32(axis_dim)))).astype(np.float32)
    t = np.arange(SEQ)
    r = (t // GRID_W).astype(np.float32)
    col = (t % GRID_W).astype(np.float32)
    ang = np.concatenate([r[:, None] * inv, col[:, None] * inv], axis=-1).astype(np.float32)
    ang = np.concatenate([ang, np.zeros((TM, ang.shape[1]), np.float32)], axis=0)
    return (jnp.asarray(np.cos(ang).astype(np.float32)),
            jnp.asarray(np.sin(ang).astype(np.float32)))


def _split_mod(mod_layer):
    m = mod_layer[:N_GROUPS].reshape(N_GROUPS, 6, 1, D_MODEL)
    return [m[:, k] for k in range(6)]


def _router_t(w_router):
    return w_router.T.astype(BF16)


def kernel(x, c, ctx, c_ctx, ada_w, ada_b, norm1_g, norm2_g, ab_w_in, ab_w_out, na_rpb, sgu_norm_g, sgu_w, sgu_b, ret_w_in, ret_w_out, ret_decay_logit, moe_router, moe_w_gate, moe_w_up, moe_w_down, final_norm_g):
    assert DEPTH == 2 and x.shape == (BATCH, SEQ, D_MODEL) and ctx.shape == (BATCH, CTX_LEN, D_MODEL)

    cvec = jnp.zeros((8, D_MODEL), F32).at[:BATCH].set(c).at[BATCH].set(c_ctx)
    mod = _adaln(cvec, ada_w, ada_b)
    xall = (x.reshape(R_LAT, D_MODEL), ctx.reshape(R_CTX, D_MODEL))

    sh1, sc1, g1, sh2, sc2, g2 = _split_mod(mod[0])
    z = _inproj(xall, norm1_g[0], sh1, sc1, ab_w_in[0].astype(BF16), tn=AB_IN)
    a_all = (_na_attention(z, _na_bias_tiles(na_rpb[0])), _ctx_attention(z))
    bias_t = jnp.repeat(sgu_b[0].T, SG_GROUP_DIM, axis=1)
    bsg = _sgu(z, sgu_norm_g[0], sgu_w[0].astype(BF16), bias_t)
    w_out = ab_w_out[0].astype(BF16)
    x1, h2, aff = _outproj([a_all, bsg], [w_out[:NA_WIDTH], w_out[NA_WIDTH:]], xall, g1,
                           norm2_g[0], sh2, sc2, _router_t(moe_router[0]), R_ALL)
    xall = _expert_choice(h2, x1, aff, g2, moe_w_gate, moe_w_up, moe_w_down, 0, with_ctx=True)

    sh1, sc1, g1, sh2, sc2, g2 = _split_mod(mod[1])
    z = _inproj_ret(xall, norm1_g[1], sh1, sc1, ret_w_in[0].astype(BF16), *_rope_tables())
    log_gamma = jax.nn.log_sigmoid(ret_decay_logit[0].astype(F32))
    o_back = _retention_pass(log_gamma, z, None)
    ypre = _retention_pass(log_gamma, z, o_back)
    x1, h2, aff = _outproj([ypre], [ret_w_out[0].astype(BF16)], xall, g1,
                           norm2_g[1], sh2, sc2, _router_t(moe_router[1]), R_LAT)
    x2 = _expert_choice(h2, x1, aff, g2, moe_w_gate, moe_w_up, moe_w_down, 1, with_ctx=False)
    return _final_norm(x2, final_norm_g).reshape(BATCH, SEQ, D_MODEL)
```

```python
import functools
from typing import NamedTuple

import numpy as np
import jax
import jax.numpy as jnp
from jax import lax
from jax.experimental import pallas as pl
from jax.experimental.pallas import tpu as pltpu

F32 = jnp.float32
BF16 = jnp.bfloat16

D_MODEL = 1024
BATCH = 2
SEQ = 8192
DEPTH = 2
GRID_W = 64
CTX_LEN = 256
EPS = 1e-6
NEG_INF = -1e30

NA_HEADS = 8
NA_HEAD_DIM = 64
NA_WIN_H = 8
NA_WIN_W = 16
NA_WIDTH = NA_HEADS * NA_HEAD_DIM
SG_GROUPS = 8
SG_GROUP_DIM = 64
SG_CHUNK = 128
SG_WIDTH = SG_GROUPS * SG_GROUP_DIM
AB_IN = 3 * NA_WIDTH + 2 * SG_WIDTH

RET_HEADS = 4
RET_QK_DIM = 256
RET_V_DIM = 512
RET_QK_WIDTH = RET_HEADS * RET_QK_DIM
RET_V_WIDTH = RET_HEADS * RET_V_DIM
RET_IN = 2 * RET_QK_WIDTH + 2 * RET_V_WIDTH
ROPE_BASE = 10000.0

N_EXPERTS = 16
EC_CAPACITY = 2
D_FF_EXPERT = 2816

ROWS = SEQ // GRID_W
R_LAT = BATCH * SEQ
R_CTX = BATCH * CTX_LEN
R_ALL = R_LAT + R_CTX
N_GROUPS = BATCH + 1

LANES = 128
TM = 512
VMEM_LIMIT = 56 * 1024 * 1024

NA_QROWS = 8
NA_KROWS = 16
NA_QT = NA_QROWS * GRID_W
NA_KT = 256
NA_NKB = NA_KROWS * GRID_W // NA_KT
NA_PAIRS_PER_STEP = 4
RET_CHUNK = 256
RET_HEADS_PER_STEP = 4
MOE_TF = 256
MOE_NF = D_FF_EXPERT // MOE_TF
MOE_ROW_SPLIT = 4
MOE_WRITE_STEPS = 4
MOE_DMA_UNROLL = 8
MOE_UPDATE_ROWS = 32
ROUTE_PREFIX_BLOCK = 256
ROUTE_SLOT_HI = 8


def _group_of_tile(i):
    return jnp.minimum(i // (SEQ // TM), BATCH)


def _cparams(sem, vmem=VMEM_LIMIT):
    return pltpu.CompilerParams(dimension_semantics=sem, vmem_limit_bytes=vmem)


def _dot(a, b):
    return jnp.dot(a, b, preferred_element_type=F32)


def _dot_nt(a, b):
    return lax.dot_general(a, b, (((1,), (1,)), ((), ())), preferred_element_type=F32)


def _dot_tn(a, b):
    return lax.dot_general(a, b, (((0,), (0,)), ((), ())), preferred_element_type=F32)


def _sigmoid(x):
    return 1.0 / (1.0 + jnp.exp(-x))


def _gelu_tanh(x):
    c = np.float32(np.sqrt(2.0 / np.pi))
    return 0.5 * x * (1.0 + jnp.tanh(c * (x + np.float32(0.044715) * (x * x * x))))


def _adaln_kernel(c_ref, w_ref, b_ref, o_ref):
    cv = c_ref[...]
    s = (cv * _sigmoid(cv)).astype(BF16)
    o_ref[...] = _dot(s, w_ref[...].astype(BF16)) + b_ref[...]


def _adaln(cvec, ada_w, ada_b):
    tn = 1024
    n = 6 * D_MODEL
    return pl.pallas_call(
        _adaln_kernel,
        grid=(DEPTH, n // tn),
        in_specs=[
            pl.BlockSpec((8, D_MODEL), lambda l, j: (0, 0)),
            pl.BlockSpec((None, D_MODEL, tn), lambda l, j: (l, 0, j)),
            pl.BlockSpec((None, 1, tn), lambda l, j: (l, 0, j)),
        ],
        out_specs=pl.BlockSpec((None, 8, tn), lambda l, j: (l, 0, j)),
        out_shape=jax.ShapeDtypeStruct((DEPTH, 8, n), F32),
        compiler_params=_cparams(("arbitrary", "arbitrary")),
        name="adaln",
    )(cvec, ada_w, ada_b.reshape(DEPTH, 1, n))


NSUB = D_MODEL // LANES


def _read_token_tiled(ref, start, rows):
    return jnp.concatenate(
        [ref[pl.ds(start * NSUB + s, rows, stride=NSUB), :] for s in range(NSUB)], axis=-1)


def _write_token_tiled(ref, start, val):
    for s in range(NSUB):
        ref[pl.ds(start * NSUB + s, val.shape[0], stride=NSUB), :] = val[:, s * LANES:(s + 1) * LANES]


def _read_rows(x_ref):
    if x_ref.shape[-1] != LANES:
        return x_ref[...]
    return _read_token_tiled(x_ref, 0, x_ref.shape[0] // NSUB)


def _row_spec(x, index_map):
    if x.shape[-1] != LANES:
        return pl.BlockSpec((TM, x.shape[-1]), lambda *a: (index_map(*a), 0))
    return pl.BlockSpec((TM * NSUB, LANES), lambda *a: (index_map(*a), 0))


def _row_specs(x, index_map):
    if not isinstance(x, tuple):
        return [_row_spec(x, index_map)]
    lat, ctx = x
    lat_tiles = lat.shape[0] // TM
    assert lat.shape[0] == R_LAT and ctx.shape[0] == TM and lat.shape[1] == ctx.shape[1]
    width = lat.shape[1]
    return [pl.BlockSpec((TM, width), lambda *a: (jnp.minimum(index_map(*a), lat_tiles - 1), 0)),
            pl.BlockSpec((TM, width), lambda *a: (0, 0))]


def _read_row_operand(refs, tile):
    if len(refs) == 1:
        return _read_rows(refs[0])
    lat_ref, ctx_ref = refs
    return jnp.where(tile < R_LAT // TM, lat_ref[...], ctx_ref[...])


def _inproj_kernel(*refs, n_x):
    x_refs = refs[:n_x]
    g_ref, sh_ref, sc_ref, w_ref, z_ref, hb_ref = refs[n_x:]

    @pl.when(pl.program_id(1) == 0)
    def _():
        x = _read_row_operand(x_refs, pl.program_id(0))
        y = x * lax.rsqrt(jnp.mean(x * x, axis=-1, keepdims=True) + EPS) * g_ref[...]
        hb_ref[...] = (y * (1.0 + sc_ref[...]) + sh_ref[...]).astype(BF16)

    z_ref[...] = _dot(hb_ref[...], w_ref[...]).astype(z_ref.dtype)


def _inproj(xall, norm_g, shift, scale, w_bf16, tn):
    n = w_bf16.shape[1]
    grp = lambda i, j: (_group_of_tile(i), 0, 0)
    x_specs = _row_specs(xall, lambda i, j: i)
    x_args = xall if isinstance(xall, tuple) else (xall,)
    return pl.pallas_call(
        functools.partial(_inproj_kernel, n_x=len(x_specs)),
        grid=(R_ALL // TM, n // tn),
        in_specs=x_specs + [
            pl.BlockSpec((1, D_MODEL), lambda i, j: (0, 0)),
            pl.BlockSpec((None, 1, D_MODEL), grp),
            pl.BlockSpec((None, 1, D_MODEL), grp),
            pl.BlockSpec((D_MODEL, tn), lambda i, j: (0, j)),
        ],
        out_specs=pl.BlockSpec((TM, tn), lambda i, j: (i, j)),
        out_shape=jax.ShapeDtypeStruct((R_ALL, n), BF16),
        scratch_shapes=[pltpu.VMEM((TM, D_MODEL), BF16)],
        compiler_params=_cparams(("arbitrary", "arbitrary")),
        name="inproj",
    )(*x_args, norm_g.reshape(1, D_MODEL), shift, scale, w_bf16)


def _rope(t, cos, sin):
    half = RET_QK_DIM // 2
    t1, t2 = t[:, :half], t[:, half:]
    return jnp.concatenate([t1 * cos - t2 * sin, t1 * sin + t2 * cos], axis=-1)


def _inproj_ret_kernel(x_ref, g_ref, sh_ref, sc_ref, cos_ref, sin_ref, w_ref, z_ref):
    x = _read_rows(x_ref)
    y = x * lax.rsqrt(jnp.mean(x * x, axis=-1, keepdims=True) + EPS) * g_ref[...]
    hb = (y * (1.0 + sc_ref[...]) + sh_ref[...]).astype(BF16)
    cos, sin = cos_ref[...], sin_ref[...]
    kscale = np.float32(RET_QK_DIM ** -0.5)
    for h in range(2 * RET_HEADS):
        cols = slice(h * RET_QK_DIM, (h + 1) * RET_QK_DIM)
        t = _rope(_dot(hb, w_ref[:, cols]), cos, sin)
        z_ref[:, cols] = (t if h < RET_HEADS else t * kscale).astype(z_ref.dtype)
    rest = slice(2 * RET_QK_WIDTH, RET_IN)
    z_ref[:, rest] = _dot(hb, w_ref[:, rest]).astype(z_ref.dtype)


def _inproj_ret(xall, norm_g, shift, scale, w_bf16, cos, sin):
    grp = lambda i: (_group_of_tile(i), 0, 0)
    lat_tiles = SEQ // TM
    pos = lambda i: (jnp.where(i < BATCH * lat_tiles, i % lat_tiles, lat_tiles), 0)
    half = RET_QK_DIM // 2
    return pl.pallas_call(
        _inproj_ret_kernel,
        grid=(R_ALL // TM,),
        in_specs=[
            _row_spec(xall, lambda i: i),
            pl.BlockSpec((1, D_MODEL), lambda i: (0, 0)),
            pl.BlockSpec((None, 1, D_MODEL), grp),
            pl.BlockSpec((None, 1, D_MODEL), grp),
            pl.BlockSpec((TM, half), pos),
            pl.BlockSpec((TM, half), pos),
            pl.BlockSpec((D_MODEL, RET_IN), lambda i: (0, 0)),
        ],
        out_specs=pl.BlockSpec((TM, RET_IN), lambda i: (i, 0)),
        out_shape=jax.ShapeDtypeStruct((R_ALL, RET_IN), BF16),
        compiler_params=_cparams(("arbitrary",)),
        name="inproj_ret",
    )(xall, norm_g.reshape(1, D_MODEL), shift, scale, cos, sin, w_bf16)


def _na_tile_codes():
    masked = 2 * NA_WIN_H - 1
    kinds = (0, 1, ROWS // NA_QROWS - 1)
    d = np.full((3, NA_QROWS, NA_KROWS), masked, np.int64)
    for v, j in enumerate(kinds):
        kb = int(np.clip(NA_QROWS * j - NA_WIN_H // 2, 0, ROWS - NA_KROWS))
        for a in range(NA_QROWS):
            i = NA_QROWS * j + a
            r0 = int(np.clip(i - NA_WIN_H // 2, 0, ROWS - NA_WIN_H))
            for r in range(NA_KROWS):
                if r0 <= kb + r < r0 + NA_WIN_H:
                    d[v, a, r] = kb + r - i + NA_WIN_H - 1
    pairs = d.reshape(-1, 2)
    uniq = sorted(set(map(tuple, pairs)))
    code = np.array([uniq.index(tuple(p)) for p in pairs], np.int32)
    return code, np.array(uniq, np.int64)


_NA_CODES, _NA_CODE_ROWS = _na_tile_codes()


def _na_bias_tiles(rpb):
    qc = np.arange(GRID_W)[:, None]
    kc = np.arange(GRID_W)[None, :]
    cstart = np.clip(qc - NA_WIN_W // 2, 0, GRID_W - NA_WIN_W)
    col_ok = (kc >= cstart) & (kc < cstart + NA_WIN_W)
    dcol = np.clip(kc - qc, 1 - NA_WIN_W, NA_WIN_W - 1) + NA_WIN_W - 1
    onehot = (dcol.reshape(1, -1) == np.arange(2 * NA_WIN_W - 1)[:, None]).astype(np.float32)
    toe = jnp.einsum('hdm,mq->hdq', rpb.astype(F32), jnp.asarray(onehot),
                     precision=lax.Precision.HIGHEST)
    toe = toe.reshape(NA_HEADS, 2 * NA_WIN_H - 1, GRID_W, GRID_W)
    toe = jnp.where(jnp.asarray(col_ok), toe, NEG_INF)
    toe = jnp.concatenate([toe, jnp.full((NA_HEADS, 1, GRID_W, GRID_W), NEG_INF, F32)], axis=1)
    return jnp.concatenate([toe[:, _NA_CODE_ROWS[:, 0]], toe[:, _NA_CODE_ROWS[:, 1]]], axis=-1)


def _pair_attention(q, keys, vals, add_bias=None):
    lane = lax.broadcasted_iota(jnp.int32, (1, LANES), 1)
    scale = NA_HEAD_DIM ** -0.5
    out = jnp.zeros((q.shape[0], LANES), F32)
    for hh in range(2):
        in_head = (lane < NA_HEAD_DIM) if hh == 0 else (lane >= NA_HEAD_DIM)
        qm = jnp.where(in_head, q, jnp.zeros_like(q)) * jnp.asarray(scale, q.dtype)
        s = []
        for t, k in enumerate(keys):
            st = _dot_nt(qm, k)
            if add_bias is not None:
                st = add_bias(hh, t, st)
            s.append(st)
        m = s[0].max(axis=-1, keepdims=True)
        for st in s[1:]:
            m = jnp.maximum(m, st.max(axis=-1, keepdims=True))
        den = jnp.zeros_like(m)
        acc = jnp.zeros((q.shape[0], LANES), F32)
        for st, v in zip(s, vals):
            p = jnp.exp(st - m)
            den = den + p.sum(axis=-1, keepdims=True)
            acc = acc + _dot(p.astype(BF16), v)
        out = jnp.where(in_head, acc / den, out)
    return out


def _na_kernel(code_ref, q_ref, k0, k1, k2, k3, v0, v1, v2, v3, kc_ref, vc_ref, tile_ref, o_ref):
    nj = pl.num_programs(1)
    j = pl.program_id(1)
    kind = jnp.minimum(j, 1) + jnp.maximum(j - (nj - 2), 0)
    tiles_per_row = NA_KROWS // 2
    code_base = kind * (NA_QROWS * tiles_per_row)
    for pp in range(NA_PAIRS_PER_STEP):
        lanes = slice(pp * LANES, (pp + 1) * LANES)
        keys = [r[:, lanes] for r in (k0, k1, k2, k3, kc_ref)]
        vals = [r[:, lanes] for r in (v0, v1, v2, v3, vc_ref)]

        def add_bias(hh, t, st, pp=pp):
            if t >= NA_NKB:
                return st
            rows = []
            for a in range(NA_QROWS):
                cols = []
                for u in range(NA_KT // LANES):
                    code = code_ref[code_base + a * tiles_per_row + t * (NA_KT // LANES) + u]
                    piece = st[a * GRID_W:(a + 1) * GRID_W, u * LANES:(u + 1) * LANES]
                    cols.append(piece + tile_ref[2 * pp + hh, code])
                rows.append(jnp.concatenate(cols, axis=1))
            return jnp.concatenate(rows, axis=0)

        o_ref[:, lanes] = _pair_attention(q_ref[:, lanes], keys, vals, add_bias).astype(o_ref.dtype)


def _na_attention(z, tiles):
    nj = ROWS // NA_QROWS
    kblocks = SEQ // NA_KT
    width = NA_PAIRS_PER_STEP * LANES
    qcol, kcol, vcol = 0, NA_WIDTH // width, 2 * NA_WIDTH // width
    n_codes = tiles.shape[1]

    def kbase(j):
        return jnp.clip(2 * j - 1, 0, kblocks - NA_NKB)

    def kspec(t, col):
        return pl.BlockSpec((NA_KT, width),
                            lambda p, j, b, code: (b * kblocks + kbase(j) + t, col + p))

    ctx_row = lambda b: R_LAT // CTX_LEN + b
    in_specs = (
        [pl.BlockSpec((NA_QT, width), lambda p, j, b, code: (b * nj + j, qcol + p))]
        + [kspec(t, kcol) for t in range(NA_NKB)]
        + [kspec(t, vcol) for t in range(NA_NKB)]
        + [pl.BlockSpec((CTX_LEN, width), lambda p, j, b, code: (ctx_row(b), kcol + p)),
           pl.BlockSpec((CTX_LEN, width), lambda p, j, b, code: (ctx_row(b), vcol + p)),
           pl.BlockSpec((2 * NA_PAIRS_PER_STEP, n_codes, GRID_W, LANES),
                        lambda p, j, b, code: (p, 0, 0, 0))])
    return pl.pallas_call(
        _na_kernel,
        grid_spec=pltpu.PrefetchScalarGridSpec(
            num_scalar_prefetch=1,
            grid=(NA_HEADS // (2 * NA_PAIRS_PER_STEP), nj, BATCH),
            in_specs=in_specs,
            out_specs=pl.BlockSpec((NA_QT, width), lambda p, j, b, code: (b * nj + j, p))),
        out_shape=jax.ShapeDtypeStruct((R_LAT, NA_WIDTH), BF16),
        compiler_params=_cparams(("arbitrary", "arbitrary", "arbitrary")),
        name="na_attention",
    )(jnp.asarray(_NA_CODES), *([z] * (3 + 2 * NA_NKB)), tiles)


def _ctx_attn_kernel(q_ref, k_ref, v_ref, o_ref):
    o_ref[...] = _pair_attention(q_ref[...], [k_ref[...]], [v_ref[...]]).astype(o_ref.dtype)


def _ctx_attention(z):
    qcol, kcol, vcol = 0, NA_WIDTH // LANES, 2 * NA_WIDTH // LANES
    row = lambda b: R_LAT // CTX_LEN + b
    return pl.pallas_call(
        _ctx_attn_kernel,
        grid=(BATCH, NA_HEADS // 2),
        in_specs=[pl.BlockSpec((CTX_LEN, LANES), lambda b, p: (row(b), qcol + p)),
                  pl.BlockSpec((CTX_LEN, LANES), lambda b, p: (row(b), kcol + p)),
                  pl.BlockSpec((CTX_LEN, LANES), lambda b, p: (row(b), vcol + p))],
        out_specs=pl.BlockSpec((CTX_LEN, LANES), lambda b, p: (b, p)),
        out_shape=jax.ShapeDtypeStruct((R_CTX, NA_WIDTH), BF16),
        compiler_params=_cparams(("arbitrary", "arbitrary")),
        name="ctx_attention",
    )(z, z, z)


def _sgu_kernel(u_ref, v_ref, ng_ref, w_ref, bt_ref, o_ref):
    lane = lax.broadcasted_iota(jnp.int32, (1, LANES), 1)
    first = lane < SG_GROUP_DIM
    for ch in range(TM // SG_CHUNK):
        rows = slice(ch * SG_CHUNK, (ch + 1) * SG_CHUNK)
        u = _gelu_tanh(u_ref[rows, :].astype(F32))
        v = _gelu_tanh(v_ref[rows, :].astype(F32))
        vn = v * lax.rsqrt(jnp.mean(v * v, axis=-1, keepdims=True) + EPS) * ng_ref[...]
        vnb = vn.astype(BF16)
        for pr in range(SG_GROUPS // 2):
            cols = slice(pr * LANES, (pr + 1) * LANES)
            slab = vnb[:, cols]
            mixed = jnp.where(first, _dot(w_ref[2 * pr], slab), _dot(w_ref[2 * pr + 1], slab))
            o_ref[rows, cols] = (u[:, cols] * (mixed + bt_ref[:, cols])).astype(o_ref.dtype)


def _sgu(z, norm_g, w_bf16, bias_t):
    ucol = 3 * NA_WIDTH // SG_WIDTH
    return pl.pallas_call(
        _sgu_kernel,
        grid=(R_ALL // TM,),
        in_specs=[pl.BlockSpec((TM, SG_WIDTH), lambda i: (i, ucol)),
                  pl.BlockSpec((TM, SG_WIDTH), lambda i: (i, ucol + 1)),
                  pl.BlockSpec((1, SG_WIDTH), lambda i: (0, 0)),
                  pl.BlockSpec((SG_GROUPS, SG_CHUNK, SG_CHUNK), lambda i: (0, 0, 0)),
                  pl.BlockSpec((SG_CHUNK, SG_WIDTH), lambda i: (0, 0))],
        out_specs=pl.BlockSpec((TM, SG_WIDTH), lambda i: (i, 0)),
        out_shape=jax.ShapeDtypeStruct((R_ALL, SG_WIDTH), BF16),
        compiler_params=_cparams(("arbitrary",)),
        name="sgu",
    )(z, z, norm_g.reshape(1, SG_WIDTH), w_bf16, bias_t)


def _outproj_kernel(*refs, operand_refs):
    tile = pl.program_id(0)
    operands, pos = [], 0
    for n in operand_refs:
        operands.append(_read_row_operand(refs[pos:pos + n], tile))
        pos += n
    lhs, x = operands[:-1], operands[-1]
    ws = refs[pos:pos + len(lhs)]
    g1_ref, ng_ref, sh_ref, sc_ref, wr_ref, xo_ref, h2_ref, aff_ref = refs[pos + len(lhs):]
    y = _dot(lhs[0], ws[0][...])
    for a, w in zip(lhs[1:], ws[1:]):
        y = y + _dot(a, w[...])
    xn = x + g1_ref[...] * y
    _write_token_tiled(xo_ref, 0, xn)
    hn = xn * lax.rsqrt(jnp.mean(xn * xn, axis=-1, keepdims=True) + EPS) * ng_ref[...]
    h2 = hn * (1.0 + sc_ref[...]) + sh_ref[...]
    _write_token_tiled(h2_ref, 0, h2)
    logits = _dot_nt(wr_ref[...], h2.astype(BF16))
    e = jnp.exp(logits - logits.max(axis=0, keepdims=True))
    aff_ref[...] = e / e.sum(axis=0, keepdims=True)


def _outproj(lhs_list, w_list, xall, gate1, norm_g, shift, scale, w_router_pad, n_rows):
    grp = lambda i: (_group_of_tile(i), 0, 0)
    row = lambda i: (i, 0)
    const2 = lambda i: (0, 0)
    row_operands = list(lhs_list) + [xall]
    operand_specs = [_row_specs(a, lambda i: i) for a in row_operands]
    row_args = [t for a in row_operands for t in (a if isinstance(a, tuple) else (a,))]
    in_specs = (
        [s for specs in operand_specs for s in specs]
        + [pl.BlockSpec(w.shape, const2) for w in w_list]
        + [pl.BlockSpec((None, 1, D_MODEL), grp),
           pl.BlockSpec((1, D_MODEL), const2),
           pl.BlockSpec((None, 1, D_MODEL), grp),
           pl.BlockSpec((None, 1, D_MODEL), grp),
           pl.BlockSpec((N_EXPERTS, D_MODEL), const2)])
    tiled = (n_rows * NSUB, LANES)
    tiled_spec = pl.BlockSpec((TM * NSUB, LANES), row)
    return pl.pallas_call(
        functools.partial(_outproj_kernel, operand_refs=tuple(len(s) for s in operand_specs)),
        grid=(n_rows // TM,),
        in_specs=in_specs,
        out_specs=[tiled_spec, tiled_spec, pl.BlockSpec((N_EXPERTS, TM), lambda i: (0, i))],
        out_shape=[jax.ShapeDtypeStruct(tiled, F32),
                   jax.ShapeDtypeStruct(tiled, F32),
                   jax.ShapeDtypeStruct((N_EXPERTS, n_rows), F32)],
        compiler_params=_cparams(("arbitrary",)),
        name="outproj",
    )(*row_args, *w_list, gate1, norm_g.reshape(1, D_MODEL), shift, scale, w_router_pad)


def _moe_segments(rows):
    cap = EC_CAPACITY * SEQ // N_EXPERTS
    segs = [(b * cap, (b + 1) * cap, b) for b in range(BATCH)]
    if rows > BATCH * cap:
        segs.append((BATCH * cap, rows, BATCH))
    return segs


class _MoeDmaPlan(NamedTuple):
    write_steps: int
    fetch_first: int
    last_step: int
    write_per_tick: int
    fetch_per_tick: int
    fetch_slots: int
    token_per_tick: int
    token_slots: int


def _moe_dma_plan(rows):
    write_ticks = MOE_WRITE_STEPS * MOE_ROW_SPLIT
    fetch_first = MOE_WRITE_STEPS + 1
    last_step = MOE_NF - 1
    fetch_ticks = (last_step - fetch_first) * MOE_ROW_SPLIT
    token_ticks = (last_step - 1) * MOE_ROW_SPLIT
    assert rows % write_ticks == 0 and fetch_ticks > 0
    fetch_per_tick = -(-rows // fetch_ticks)
    while (fetch_per_tick * fetch_ticks) % MOE_DMA_UNROLL:
        fetch_per_tick += 1
    token_per_tick = -(-rows // token_ticks)
    while (token_per_tick * token_ticks) % MOE_DMA_UNROLL:
        token_per_tick += 1
    return _MoeDmaPlan(MOE_WRITE_STEPS, fetch_first, last_step, rows // write_ticks,
                       fetch_per_tick, fetch_per_tick * fetch_ticks,
                       token_per_tick, token_per_tick * token_ticks)


def _moe_kernel(idx_ref, h_hbm, x_hbm, gate_ref, g2_ref, wg_ref, wu_ref, wd_ref, xo_hbm,
                xg, xb, acc, res, wg_s, wu_s, wd_s, sem_h, sem_r, sem_s, *, rows):
    del x_hbm
    e, f = pl.program_id(0), pl.program_id(1)
    nf = pl.num_programs(1)
    plan = _moe_dma_plan(rows)

    def issue(count, first, n_valid, expert, make_copy):
        base = expert * rows
        for u in range(count):
            i = first + u
            n = idx_ref[base + jnp.minimum(i, n_valid - 1)]
            make_copy(pl.ds(pl.multiple_of(i * NSUB, NSUB), NSUB), n).start(priority=u % 2)

    def issue_all(n_slots, expert, make_copy):
        def group(o, carry):
            issue(MOE_DMA_UNROLL, o * MOE_DMA_UNROLL, rows, expert, make_copy)
            return carry

        lax.fori_loop(0, n_slots // MOE_DMA_UNROLL, group, 0)

    def token_copy(slot, n):
        return pltpu.make_async_copy(h_hbm.at[n], xg.at[slot, :], sem_h)

    def fetch_copy(slot, n):
        return pltpu.make_async_copy(xo_hbm.at[n], res.at[slot, :], sem_r)

    def writeback_copy(slot, n):
        return pltpu.make_async_copy(res.at[slot, :], xo_hbm.at[n], sem_s)

    all_tokens = pltpu.make_async_copy(xg, xg, sem_h)
    all_fetched = pltpu.make_async_copy(res, res, sem_r)
    written_rows = res.at[pl.ds(0, rows * NSUB), :]
    all_written = pltpu.make_async_copy(written_rows, written_rows, sem_s)

    prev = jnp.maximum(e - 1, 0)
    nxt = jnp.where(e + 1 < N_EXPERTS, e + 1, 0)

    def apply_update():
        for lo, hi, g in _moe_segments(rows):
            for r0 in range(lo, hi, MOE_UPDATE_ROWS):
                nr = min(MOE_UPDATE_ROWS, hi - r0)
                gate = gate_ref[r0:r0 + nr, :]
                sums = []
                for s in range(NSUB):
                    cols = slice(s * LANES, (s + 1) * LANES)
                    tiled = pl.ds(r0 * NSUB + s, nr, stride=NSUB)
                    sums.append(res[tiled, :] + acc[r0:r0 + nr, cols] * gate * g2_ref[g, :, cols])
                for s in range(NSUB):
                    res[pl.ds(r0 * NSUB + s, nr, stride=NSUB), :] = sums[s]

    @pl.when(f == 0)
    def _():
        @pl.when(e == 0)
        def _():
            issue_all(plan.token_slots, 0, token_copy)
            issue_all(plan.fetch_slots, 0, fetch_copy)

        all_tokens.wait()
        all_fetched.wait()

        @pl.when(e > 0)
        def _():
            apply_update()

        @pl.when(e == N_EXPERTS)
        def _():
            issue_all(rows, N_EXPERTS - 1, writeback_copy)
            all_written.wait()

    @pl.when((f == plan.fetch_first) & (e < N_EXPERTS))
    def _():
        all_written.wait()

    def step_body(write_back=False, fetch=False, gather=False, first=False):
        wg_s[...] = wg_ref[...].astype(BF16)
        wu_s[...] = wu_ref[...].astype(BF16)
        wd_s[...] = wd_ref[...].astype(BF16)
        rc = rows // MOE_ROW_SPLIT
        for r in range(MOE_ROW_SPLIT):
            sl = slice(r * rc, (r + 1) * rc)
            if first:
                for r0 in range(r * rc, (r + 1) * rc, MOE_UPDATE_ROWS):
                    nr = min(MOE_UPDATE_ROWS, (r + 1) * rc - r0)
                    xb[r0:r0 + nr, :] = _read_token_tiled(xg, r0, nr).astype(BF16)
            x = xb[sl, :]
            a = _dot(x, wg_s[...])
            b = _dot(x, wu_s[...])
            hid = (a * _sigmoid(a) * b).astype(BF16)
            if first:
                acc[sl, :] = _dot(hid, wd_s[...])
            else:
                acc[sl, :] += _dot(hid, wd_s[...])
            tick = f * MOE_ROW_SPLIT + r
            if write_back:
                issue(plan.write_per_tick, tick * plan.write_per_tick, rows, prev, writeback_copy)
            if fetch:
                slot0 = (tick - plan.fetch_first * MOE_ROW_SPLIT) * plan.fetch_per_tick
                issue(plan.fetch_per_tick, slot0, rows, e, fetch_copy)
            if gather:
                slot0 = (tick - MOE_ROW_SPLIT) * plan.token_per_tick
                issue(plan.token_per_tick, slot0, rows, nxt, token_copy)

    real = e < N_EXPERTS

    @pl.when((f == 0) & real)
    def _():
        step_body(write_back=True, first=True)

    @pl.when((f > 0) & (f < plan.write_steps) & real)
    def _():
        step_body(write_back=True, gather=True)

    @pl.when((f >= plan.write_steps) & (f < plan.fetch_first) & real)
    def _():
        step_body(gather=True)

    @pl.when((f >= plan.fetch_first) & (f < plan.last_step) & real)
    def _():
        step_body(fetch=True, gather=True)

    @pl.when((f >= plan.last_step) & real)
    def _():
        step_body()


def _moe_ffn(flat_idx, h2, x1, gates, gate2, w_gate, w_up, w_down, layer):
    rows = flat_idx.shape[1]
    assert rows % MOE_DMA_UNROLL == 0 and rows % MOE_ROW_SPLIT == 0
    plan = _moe_dma_plan(rows)
    nf = MOE_NF
    n_tokens = x1.shape[0] // NSUB
    as_tokens = lambda t: t.reshape(n_tokens, NSUB, LANES)
    ew = lambda e: jnp.minimum(e, N_EXPERTS - 1)
    fw = lambda e, f: jnp.where(e < N_EXPERTS, f, nf - 1)
    out = pl.pallas_call(
        functools.partial(_moe_kernel, rows=rows),
        grid_spec=pltpu.PrefetchScalarGridSpec(
            num_scalar_prefetch=1,
            grid=(N_EXPERTS + 1, nf),
            in_specs=[pl.BlockSpec(memory_space=pl.ANY),
                      pl.BlockSpec(memory_space=pl.ANY),
                      pl.BlockSpec((None, rows, LANES), lambda e, f, idx: (jnp.maximum(e - 1, 0), 0, 0)),
                      pl.BlockSpec((N_GROUPS, 1, D_MODEL), lambda e, f, idx: (0, 0, 0)),
                      pl.BlockSpec((None, None, D_MODEL, MOE_TF),
                                   lambda e, f, idx: (layer, ew(e), 0, fw(e, f))),
                      pl.BlockSpec((None, None, D_MODEL, MOE_TF),
                                   lambda e, f, idx: (layer, ew(e), 0, fw(e, f))),
                      pl.BlockSpec((None, None, MOE_TF, D_MODEL),
                                   lambda e, f, idx: (layer, ew(e), fw(e, f), 0))],
            out_specs=pl.BlockSpec(memory_space=pl.ANY),
            scratch_shapes=[pltpu.VMEM((plan.token_slots * NSUB, LANES), F32),
                            pltpu.VMEM((rows, D_MODEL), BF16),
                            pltpu.VMEM((rows, D_MODEL), F32),
                            pltpu.VMEM((plan.fetch_slots * NSUB, LANES), F32),
                            pltpu.VMEM((D_MODEL, MOE_TF), BF16),
                            pltpu.VMEM((D_MODEL, MOE_TF), BF16),
                            pltpu.VMEM((MOE_TF, D_MODEL), BF16),
                            pltpu.SemaphoreType.DMA,
                            pltpu.SemaphoreType.DMA,
                            pltpu.SemaphoreType.DMA]),
        out_shape=jax.ShapeDtypeStruct((n_tokens, NSUB, LANES), F32),
        input_output_aliases={2: 0},
        compiler_params=_cparams(("arbitrary", "arbitrary")),
        name="moe_ffn",
    )(flat_idx.reshape(-1), as_tokens(h2), as_tokens(x1), gates, gate2, w_gate, w_up, w_down)
    return out.reshape(x1.shape)


def _exclusive_prefix(mask, upper):
    out = []
    run = jnp.zeros((mask.shape[0], 1), F32)
    for c in range(mask.shape[1] // ROUTE_PREFIX_BLOCK):
        m = mask[:, c * ROUTE_PREFIX_BLOCK:(c + 1) * ROUTE_PREFIX_BLOCK]
        out.append(_dot(m.astype(BF16), upper) + run)
        run = run + jnp.sum(m, axis=1, keepdims=True)
    return jnp.concatenate(out, axis=1)


def _route_kernel(aff_ref, idx_ref, gate_ref, pos3, sel3, val3, *, cap):
    n = aff_ref.shape[1]
    v = aff_ref[...]
    thr = jnp.zeros((N_EXPERTS, 1), jnp.int32)
    for bit in range(30, -1, -1):
        cand = thr | (1 << bit)
        cnt = jnp.sum((v >= pltpu.bitcast(cand, F32)).astype(jnp.int32), axis=1, keepdims=True)
        thr = jnp.where(cnt >= cap, cand, thr)
    ri = lax.broadcasted_iota(jnp.int32, (ROUTE_PREFIX_BLOCK, ROUTE_PREFIX_BLOCK), 0)
    ci = lax.broadcasted_iota(jnp.int32, (ROUTE_PREFIX_BLOCK, ROUTE_PREFIX_BLOCK), 1)
    upper = (ri < ci).astype(BF16)
    gt = v >= pltpu.bitcast(thr + 1, F32)
    eq = (v >= pltpu.bitcast(thr, F32)) & ~gt
    need = (cap - jnp.sum(gt.astype(jnp.int32), axis=1, keepdims=True)).astype(F32)
    sel = gt | (eq & (_exclusive_prefix(eq.astype(F32), upper) < need))
    sel_f = sel.astype(F32)
    pos = _exclusive_prefix(sel_f, upper)
    for e in range(N_EXPERTS):
        pos3[e] = pos[e:e + 1, :].astype(jnp.int32)
        sel3[e] = sel_f[e:e + 1, :]
        val3[e] = v[e:e + 1, :]

    tok = lax.broadcasted_iota(jnp.int32, (1, n), 1)
    tok_hi = (tok >> 7).astype(F32)
    tok_lo = (tok & (LANES - 1)).astype(F32)
    hi_row = lax.broadcasted_iota(jnp.int32, (ROUTE_SLOT_HI, 1), 0)
    lo_row = lax.broadcasted_iota(jnp.int32, (LANES, 1), 0)

    def per_expert(e, carry):
        p, s, a = pos3[e], sel3[e], val3[e]
        a1 = a.astype(BF16).astype(F32)
        a2 = (a - a1).astype(BF16).astype(F32)
        a3 = a - a1 - a2
        hot_hi = jnp.where((p >> 7) == hi_row, s, 0.0)
        hot_lo = ((p & (LANES - 1)) == lo_row).astype(BF16)
        lhs = jnp.concatenate([hot_hi * tok_hi, hot_hi * tok_lo,
                               hot_hi * a1, hot_hi * a2, hot_hi * a3], axis=0).astype(BF16)
        out = _dot_nt(lhs, hot_lo)
        h = ROUTE_SLOT_HI
        idx_ref[e] = (out[0:h] * float(LANES) + out[h:2 * h]).astype(jnp.int32)
        gate_ref[e] = (out[2 * h:3 * h] + out[3 * h:4 * h]) + out[4 * h:5 * h]
        return carry

    lax.fori_loop(0, N_EXPERTS, per_expert, 0)


def _route(aff_t, n, cap, first_block, n_sets):
    assert cap <= ROUTE_SLOT_HI * LANES and n % ROUTE_PREFIX_BLOCK == 0 and n < (1 << 14)
    shape = (n_sets, N_EXPERTS, ROUTE_SLOT_HI, LANES)
    spec = pl.BlockSpec((None, N_EXPERTS, ROUTE_SLOT_HI, LANES), lambda s: (s, 0, 0, 0))
    return pl.pallas_call(
        functools.partial(_route_kernel, cap=cap),
        grid=(n_sets,),
        in_specs=[pl.BlockSpec((N_EXPERTS, n), lambda s: (0, first_block + s))],
        out_specs=[spec, spec],
        out_shape=[jax.ShapeDtypeStruct(shape, jnp.int32), jax.ShapeDtypeStruct(shape, F32)],
        scratch_shapes=[pltpu.VMEM((N_EXPERTS, 1, n), jnp.int32),
                        pltpu.VMEM((N_EXPERTS, 1, n), F32),
                        pltpu.VMEM((N_EXPERTS, 1, n), F32)],
        compiler_params=_cparams(("arbitrary",)),
        name="route",
    )(aff_t)


def _expert_choice(h2, x1, aff_t, gate2, w_gate, w_up, w_down, layer, with_ctx):
    def slots(idx, gate, cap, row0, set_len):
        idx = idx.reshape(BATCH, N_EXPERTS, -1)[:, :, :cap]
        gate = gate.reshape(BATCH, N_EXPERTS, -1)[:, :, :cap]
        idx = idx + row0 + (jnp.arange(BATCH, dtype=jnp.int32) * set_len)[:, None, None]
        return (jnp.swapaxes(idx, 0, 1).reshape(N_EXPERTS, BATCH * cap),
                jnp.swapaxes(gate, 0, 1).reshape(N_EXPERTS, BATCH * cap))

    cap = EC_CAPACITY * SEQ // N_EXPERTS
    flat, gate = slots(*_route(aff_t, SEQ, cap, 0, BATCH), cap, 0, SEQ)
    if with_ctx:
        cap_c = EC_CAPACITY * CTX_LEN // N_EXPERTS
        flat_c, gate_c = slots(*_route(aff_t, CTX_LEN, cap_c, R_LAT // CTX_LEN, BATCH),
                               cap_c, R_LAT, CTX_LEN)
        flat = jnp.concatenate([flat, flat_c], axis=1)
        gate = jnp.concatenate([gate, gate_c], axis=1)
    gate = jnp.broadcast_to(gate[..., None], gate.shape + (LANES,))
    return _moe_ffn(flat, h2, x1, gate, gate2, w_gate, w_up, w_down, layer)


def _final_norm_kernel(x_ref, ng_ref, o_ref):
    x = _read_rows(x_ref)
    o_ref[...] = x * lax.rsqrt(jnp.mean(x * x, axis=-1, keepdims=True) + EPS) * ng_ref[...]


def _final_norm(x, norm_g):
    n_rows = x.size // D_MODEL
    return pl.pallas_call(
        _final_norm_kernel,
        grid=(n_rows // TM,),
        in_specs=[_row_spec(x, lambda i: i),
                  pl.BlockSpec((1, D_MODEL), lambda i: (0, 0))],
        out_specs=pl.BlockSpec((TM, D_MODEL), lambda i: (i, 0)),
        out_shape=jax.ShapeDtypeStruct((n_rows, D_MODEL), F32),
        compiler_params=_cparams(("arbitrary",)),
        name="final_norm",
    )(x, norm_g.reshape(1, D_MODEL))


def _retention_kernel(lg_ref, q_ref, k_ref, v_ref, kc_ref, vc_ref, *rest, backward):
    if backward:
        o_ref, s_ref, dmat_ref, qd_ref, kd_ref = rest
    else:
        ob_ref, g_ref, o_ref, s_ref, dmat_ref, qd_ref, kd_ref = rest
    c = RET_CHUNK
    heads = range(RET_HEADS_PER_STEP)
    lgs = [lg_ref[1 if backward else 0, pl.program_id(1) * RET_HEADS_PER_STEP + hh] for hh in heads]
    qks = [slice(hh * RET_QK_DIM, (hh + 1) * RET_QK_DIM) for hh in heads]
    vvs = [slice(hh * RET_V_DIM, (hh + 1) * RET_V_DIM) for hh in heads]

    @pl.when(pl.program_id(2) == 0)
    def _():
        ia = lax.broadcasted_iota(jnp.int32, (c, c), 0)
        ib = lax.broadcasted_iota(jnp.int32, (c, c), 1)
        diff = ((ib - ia) if backward else (ia - ib)).astype(F32)
        pos = lax.broadcasted_iota(jnp.int32, (c, RET_QK_DIM), 0).astype(F32)
        cpos = lax.broadcasted_iota(jnp.int32, (CTX_LEN, RET_QK_DIM), 0).astype(F32)
        for hh, lg in zip(heads, lgs):
            dmat_ref[hh] = jnp.where(diff >= 0, jnp.exp(jnp.maximum(diff, 0.0) * lg), 0.0)
            if backward:
                qd_ref[hh] = jnp.exp((c - pos) * lg)
                kd_ref[hh] = jnp.exp(pos * lg)
            else:
                qd_ref[hh] = jnp.exp((pos + 1.0) * lg)
                kd_ref[hh] = jnp.exp((c - 1.0 - pos) * lg)
            cw = jnp.exp(cpos * lg) if backward else jnp.exp((CTX_LEN - 1.0 - cpos) * lg)
            kcw = (kc_ref[:, qks[hh]].astype(F32) * cw).astype(BF16)
            s_ref[hh] = _dot_tn(kcw, vc_ref[:, vvs[hh]])

    for hh, lg, qk, vv in zip(heads, lgs, qks, vvs):
        qb = q_ref[:, qk]
        kb = k_ref[:, qk]
        v = v_ref[:, vv]
        att = _dot_nt(qb, kb) * dmat_ref[hh]
        s_old = s_ref[hh]
        o = (_dot(att.astype(BF16), v)
             + _dot((qb.astype(F32) * qd_ref[hh]).astype(BF16), s_old.astype(BF16)))
        chunk_decay = jnp.exp(jnp.zeros((1, RET_V_DIM), F32) + c * lg)
        s_ref[hh] = s_old * chunk_decay + _dot_tn((kb.astype(F32) * kd_ref[hh]).astype(BF16), v)
        if backward:
            o_ref[:, vv] = o.astype(o_ref.dtype)
        else:
            of = o + ob_ref[:, vv].astype(F32)
            y = of * lax.rsqrt(jnp.mean(of * of, axis=-1, keepdims=True) + EPS)
            g = g_ref[:, vv].astype(F32)
            o_ref[:, vv] = (g * _sigmoid(g) * y).astype(o_ref.dtype)


def _retention_pass(log_gamma, z, o_back):
    backward = o_back is None
    c = RET_CHUNK
    n = SEQ // c
    hp = RET_HEADS_PER_STEP
    qk_w, v_w = hp * RET_QK_DIM, hp * RET_V_DIM
    qk_cols = RET_QK_WIDTH // qk_w
    v_base = 2 * RET_QK_WIDTH // v_w
    g_base = v_base + RET_V_WIDTH // v_w
    chunk = (lambda t: n - 1 - t) if backward else (lambda t: t)
    row = lambda b, t: b * n + chunk(t)
    ctx_row = lambda b: R_LAT // CTX_LEN + b
    in_specs = [
        pl.BlockSpec((c, qk_w), lambda b, h, t, lg: (row(b, t), h)),
        pl.BlockSpec((c, qk_w), lambda b, h, t, lg: (row(b, t), qk_cols + h)),
        pl.BlockSpec((c, v_w), lambda b, h, t, lg: (row(b, t), v_base + h)),
        pl.BlockSpec((CTX_LEN, qk_w), lambda b, h, t, lg: (ctx_row(b), qk_cols + h)),
        pl.BlockSpec((CTX_LEN, v_w), lambda b, h, t, lg: (ctx_row(b), v_base + h)),
    ]
    args = [z, z, z, z, z]
    if not backward:
        in_specs += [
            pl.BlockSpec((c, v_w), lambda b, h, t, lg: (row(b, t), h)),
            pl.BlockSpec((c, v_w), lambda b, h, t, lg: (row(b, t), g_base + h)),
        ]
        args += [o_back, z]
    return pl.pallas_call(
        functools.partial(_retention_kernel, backward=backward),
        grid_spec=pltpu.PrefetchScalarGridSpec(
            num_scalar_prefetch=1,
            grid=(BATCH, RET_HEADS // hp, n),
            in_specs=in_specs,
            out_specs=pl.BlockSpec((c, v_w), lambda b, h, t, lg: (row(b, t), h)),
            scratch_shapes=[pltpu.VMEM((hp, RET_QK_DIM, RET_V_DIM), F32),
                            pltpu.VMEM((hp, c, c), F32),
                            pltpu.VMEM((hp, c, RET_QK_DIM), F32),
                            pltpu.VMEM((hp, c, RET_QK_DIM), F32)]),
        out_shape=jax.ShapeDtypeStruct((R_LAT, RET_V_WIDTH), BF16),
        compiler_params=_cparams(("arbitrary", "arbitrary", "arbitrary")),
        name="retention_bwd" if backward else "retention_fwd",
    )(log_gamma, *args)


def _rope_tables():
    axis_dim = RET_QK_DIM // 2
    inv = (1.0 / (np.float32(ROPE_BASE) ** (np.arange(0, axis_dim, 2, dtype=np.float32)
                                             / np.float32(axis_dim)))).astype(np.float32)
    t = np.arange(SEQ)
    r = (t // GRID_W).astype(np.float32)
    col = (t % GRID_W).astype(np.float32)
    ang = np.concatenate([r[:, None] * inv, col[:, None] * inv], axis=-1).astype(np.float32)
    ang = np.concatenate([ang, np.zeros((TM, ang.shape[1]), np.float32)], axis=0)
    return (jnp.asarray(np.cos(ang).astype(np.float32)),
            jnp.asarray(np.sin(ang).astype(np.float32)))


def _split_mod(mod_layer):
    m = mod_layer[:N_GROUPS].reshape(N_GROUPS, 6, 1, D_MODEL)
    return [m[:, k] for k in range(6)]


def _router_t(w_router):
    return w_router.T.astype(BF16)


def kernel(x, c, ctx, c_ctx, ada_w, ada_b, norm1_g, norm2_g, ab_w_in, ab_w_out, na_rpb, sgu_norm_g, sgu_w, sgu_b, ret_w_in, ret_w_out, ret_decay_logit, moe_router, moe_w_gate, moe_w_up, moe_w_down, final_norm_g):
    assert DEPTH == 2 and x.shape == (BATCH, SEQ, D_MODEL) and ctx.shape == (BATCH, CTX_LEN, D_MODEL)

    cvec = jnp.zeros((8, D_MODEL), F32).at[:BATCH].set(c).at[BATCH].set(c_ctx)
    mod = _adaln(cvec, ada_w, ada_b)
    xall = (x.reshape(R_LAT, D_MODEL), ctx.reshape(R_CTX, D_MODEL))

    sh1, sc1, g1, sh2, sc2, g2 = _split_mod(mod[0])
    z = _inproj(xall, norm1_g[0], sh1, sc1, ab_w_in[0].astype(BF16), tn=AB_IN)
    a_all = (_na_attention(z, _na_bias_tiles(na_rpb[0])), _ctx_attention(z))
    bias_t = jnp.repeat(sgu_b[0].T, SG_GROUP_DIM, axis=1)
    bsg = _sgu(z, sgu_norm_g[0], sgu_w[0].astype(BF16), bias_t)
    w_out = ab_w_out[0].astype(BF16)
    x1, h2, aff = _outproj([a_all, bsg], [w_out[:NA_WIDTH], w_out[NA_WIDTH:]], xall, g1,
                           norm2_g[0], sh2, sc2, _router_t(moe_router[0]), R_ALL)
    xall = _expert_choice(h2, x1, aff, g2, moe_w_gate, moe_w_up, moe_w_down, 0, with_ctx=True)

    sh1, sc1, g1, sh2, sc2, g2 = _split_mod(mod[1])
    z = _inproj_ret(xall, norm1_g[1], sh1, sc1, ret_w_in[0].astype(BF16), *_rope_tables())
    log_gamma = jax.nn.log_sigmoid(ret_decay_logit[0].astype(F32))
    o_back = _retention_pass(log_gamma, z, None)
    ypre = _retention_pass(log_gamma, z, o_back)
    x1, h2, aff = _outproj([ypre], [ret_w_out[0].astype(BF16)], xall, g1,
                           norm2_g[1], sh2, sc2, _router_t(moe_router[1]), R_LAT)
    x2 = _expert_choice(h2, x1, aff, g2, moe_w_gate, moe_w_up, moe_w_down, 1, with_ctx=False)
    return _final_norm(x2, final_norm_g).reshape(BATCH, SEQ, D_MODEL)
```

```python
import functools
from typing import NamedTuple

import numpy as np
import jax
import jax.numpy as jnp
from jax import lax
from jax.experimental import pallas as pl
from jax.experimental.pallas import tpu as pltpu

F32 = jnp.float32
BF16 = jnp.bfloat16

D_MODEL = 1024
BATCH = 2
SEQ = 8192
DEPTH = 2
GRID_W = 64
CTX_LEN = 256
EPS = 1e-6
NEG_INF = -1e30

NA_HEADS = 8
NA_HEAD_DIM = 64
NA_WIN_H = 8
NA_WIN_W = 16
NA_WIDTH = NA_HEADS * NA_HEAD_DIM
SG_GROUPS = 8
SG_GROUP_DIM = 64
SG_CHUNK = 128
SG_WIDTH = SG_GROUPS * SG_GROUP_DIM
AB_IN = 3 * NA_WIDTH + 2 * SG_WIDTH

RET_HEADS = 4
RET_QK_DIM = 256
RET_V_DIM = 512
RET_QK_WIDTH = RET_HEADS * RET_QK_DIM
RET_V_WIDTH = RET_HEADS * RET_V_DIM
RET_IN = 2 * RET_QK_WIDTH + 2 * RET_V_WIDTH
ROPE_BASE = 10000.0

N_EXPERTS = 16
EC_CAPACITY = 2
D_FF_EXPERT = 2816

ROWS = SEQ // GRID_W
R_LAT = BATCH * SEQ
R_CTX = BATCH * CTX_LEN
R_ALL = R_LAT + R_CTX
N_GROUPS = BATCH + 1

LANES = 128
TM = 512
VMEM_LIMIT = 56 * 1024 * 1024

NA_QROWS = 8
NA_KROWS = 16
NA_QT = NA_QROWS * GRID_W
NA_KT = 256
NA_NKB = NA_KROWS * GRID_W // NA_KT
NA_PAIRS_PER_STEP = 4
RET_CHUNK = 256
RET_HEADS_PER_STEP = 4
MOE_TF = 256
MOE_NF = D_FF_EXPERT // MOE_TF
MOE_WEIGHT_SLOTS = 3
MOE_ROW_SPLIT = 4
MOE_WRITE_STEPS = 4
MOE_DMA_UNROLL = 8
MOE_UPDATE_ROWS = 32
ROUTE_PREFIX_BLOCK = 256
ROUTE_SLOT_HI = 8


def _group_of_tile(i):
    return jnp.minimum(i // (SEQ // TM), BATCH)


def _cparams(sem, vmem=VMEM_LIMIT):
    return pltpu.CompilerParams(dimension_semantics=sem, vmem_limit_bytes=vmem)


def _dot(a, b):
    return jnp.dot(a, b, preferred_element_type=F32)


def _dot_nt(a, b):
    return lax.dot_general(a, b, (((1,), (1,)), ((), ())), preferred_element_type=F32)


def _dot_tn(a, b):
    return lax.dot_general(a, b, (((0,), (0,)), ((), ())), preferred_element_type=F32)


def _sigmoid(x):
    return 1.0 / (1.0 + jnp.exp(-x))


def _gelu_tanh(x):
    c = np.float32(np.sqrt(2.0 / np.pi))
    return 0.5 * x * (1.0 + jnp.tanh(c * (x + np.float32(0.044715) * (x * x * x))))


def _adaln_kernel(c_ref, w_ref, b_ref, o_ref):
    cv = c_ref[...]
    s = (cv * _sigmoid(cv)).astype(BF16)
    o_ref[...] = _dot(s, w_ref[...].astype(BF16)) + b_ref[...]


def _adaln(cvec, ada_w, ada_b):
    tn = 1024
    n = 6 * D_MODEL
    return pl.pallas_call(
        _adaln_kernel,
        grid=(DEPTH, n // tn),
        in_specs=[
            pl.BlockSpec((8, D_MODEL), lambda l, j: (0, 0)),
            pl.BlockSpec((None, D_MODEL, tn), lambda l, j: (l, 0, j)),
            pl.BlockSpec((None, 1, tn), lambda l, j: (l, 0, j)),
        ],
        out_specs=pl.BlockSpec((None, 8, tn), lambda l, j: (l, 0, j)),
        out_shape=jax.ShapeDtypeStruct((DEPTH, 8, n), F32),
        compiler_params=_cparams(("arbitrary", "arbitrary")),
        name="adaln",
    )(cvec, ada_w, ada_b.reshape(DEPTH, 1, n))


NSUB = D_MODEL // LANES


def _read_token_tiled(ref, start, rows):
    return jnp.concatenate(
        [ref[pl.ds(start * NSUB + s, rows, stride=NSUB), :] for s in range(NSUB)], axis=-1)


def _write_token_tiled(ref, start, val):
    for s in range(NSUB):
        ref[pl.ds(start * NSUB + s, val.shape[0], stride=NSUB), :] = val[:, s * LANES:(s + 1) * LANES]


def _read_rows(x_ref):
    if x_ref.shape[-1] != LANES:
        return x_ref[...]
    return _read_token_tiled(x_ref, 0, x_ref.shape[0] // NSUB)


def _row_spec(x, index_map):
    if x.shape[-1] != LANES:
        return pl.BlockSpec((TM, x.shape[-1]), lambda *a: (index_map(*a), 0))
    return pl.BlockSpec((TM * NSUB, LANES), lambda *a: (index_map(*a), 0))


def _row_specs(x, index_map):
    if not isinstance(x, tuple):
        return [_row_spec(x, index_map)]
    lat, ctx = x
    lat_tiles = lat.shape[0] // TM
    assert lat.shape[0] == R_LAT and ctx.shape[0] == TM and lat.shape[1] == ctx.shape[1]
    width = lat.shape[1]
    return [pl.BlockSpec((TM, width), lambda *a: (jnp.minimum(index_map(*a), lat_tiles - 1), 0)),
            pl.BlockSpec((TM, width), lambda *a: (0, 0))]


def _read_row_operand(refs, tile):
    if len(refs) == 1:
        return _read_rows(refs[0])
    lat_ref, ctx_ref = refs
    return jnp.where(tile < R_LAT // TM, lat_ref[...], ctx_ref[...])


def _inproj_kernel(*refs, n_x):
    x_refs = refs[:n_x]
    g_ref, sh_ref, sc_ref, w_ref, z_ref, hb_ref = refs[n_x:]

    @pl.when(pl.program_id(1) == 0)
    def _():
        x = _read_row_operand(x_refs, pl.program_id(0))
        y = x * lax.rsqrt(jnp.mean(x * x, axis=-1, keepdims=True) + EPS) * g_ref[...]
        hb_ref[...] = (y * (1.0 + sc_ref[...]) + sh_ref[...]).astype(BF16)

    z_ref[...] = _dot(hb_ref[...], w_ref[...]).astype(z_ref.dtype)


def _inproj(xall, norm_g, shift, scale, w_bf16, tn):
    n = w_bf16.shape[1]
    grp = lambda i, j: (_group_of_tile(i), 0, 0)
    x_specs = _row_specs(xall, lambda i, j: i)
    x_args = xall if isinstance(xall, tuple) else (xall,)
    return pl.pallas_call(
        functools.partial(_inproj_kernel, n_x=len(x_specs)),
        grid=(R_ALL // TM, n // tn),
        in_specs=x_specs + [
            pl.BlockSpec((1, D_MODEL), lambda i, j: (0, 0)),
            pl.BlockSpec((None, 1, D_MODEL), grp),
            pl.BlockSpec((None, 1, D_MODEL), grp),
            pl.BlockSpec((D_MODEL, tn), lambda i, j: (0, j)),
        ],
        out_specs=pl.BlockSpec((TM, tn), lambda i, j: (i, j)),
        out_shape=jax.ShapeDtypeStruct((R_ALL, n), BF16),
        scratch_shapes=[pltpu.VMEM((TM, D_MODEL), BF16)],
        compiler_params=_cparams(("arbitrary", "arbitrary")),
        name="inproj",
    )(*x_args, norm_g.reshape(1, D_MODEL), shift, scale, w_bf16)


def _rope(t, cos, sin):
    half = RET_QK_DIM // 2
    t1, t2 = t[:, :half], t[:, half:]
    return jnp.concatenate([t1 * cos - t2 * sin, t1 * sin + t2 * cos], axis=-1)


def _inproj_ret_kernel(x_ref, g_ref, sh_ref, sc_ref, cos_ref, sin_ref, w_ref, z_ref):
    x = _read_rows(x_ref)
    y = x * lax.rsqrt(jnp.mean(x * x, axis=-1, keepdims=True) + EPS) * g_ref[...]
    hb = (y * (1.0 + sc_ref[...]) + sh_ref[...]).astype(BF16)
    cos, sin = cos_ref[...], sin_ref[...]
    kscale = np.float32(RET_QK_DIM ** -0.5)
    for h in range(2 * RET_HEADS):
        cols = slice(h * RET_QK_DIM, (h + 1) * RET_QK_DIM)
        t = _rope(_dot(hb, w_ref[:, cols]), cos, sin)
        z_ref[:, cols] = (t if h < RET_HEADS else t * kscale).astype(z_ref.dtype)
    rest = slice(2 * RET_QK_WIDTH, RET_IN)
    z_ref[:, rest] = _dot(hb, w_ref[:, rest]).astype(z_ref.dtype)


def _inproj_ret(xall, norm_g, shift, scale, w_bf16, cos, sin):
    grp = lambda i: (_group_of_tile(i), 0, 0)
    lat_tiles = SEQ // TM
    pos = lambda i: (jnp.where(i < BATCH * lat_tiles, i % lat_tiles, lat_tiles), 0)
    half = RET_QK_DIM // 2
    return pl.pallas_call(
        _inproj_ret_kernel,
        grid=(R_ALL // TM,),
        in_specs=[
            _row_spec(xall, lambda i: i),
            pl.BlockSpec((1, D_MODEL), lambda i: (0, 0)),
            pl.BlockSpec((None, 1, D_MODEL), grp),
            pl.BlockSpec((None, 1, D_MODEL), grp),
            pl.BlockSpec((TM, half), pos),
            pl.BlockSpec((TM, half), pos),
            pl.BlockSpec((D_MODEL, RET_IN), lambda i: (0, 0)),
        ],
        out_specs=pl.BlockSpec((TM, RET_IN), lambda i: (i, 0)),
        out_shape=jax.ShapeDtypeStruct((R_ALL, RET_IN), BF16),
        compiler_params=_cparams(("arbitrary",)),
        name="inproj_ret",
    )(xall, norm_g.reshape(1, D_MODEL), shift, scale, cos, sin, w_bf16)


def _na_tile_codes():
    masked = 2 * NA_WIN_H - 1
    kinds = (0, 1, ROWS // NA_QROWS - 1)
    d = np.full((3, NA_QROWS, NA_KROWS), masked, np.int64)
    for v, j in enumerate(kinds):
        kb = int(np.clip(NA_QROWS * j - NA_WIN_H // 2, 0, ROWS - NA_KROWS))
        for a in range(NA_QROWS):
            i = NA_QROWS * j + a
            r0 = int(np.clip(i - NA_WIN_H // 2, 0, ROWS - NA_WIN_H))
            for r in range(NA_KROWS):
                if r0 <= kb + r < r0 + NA_WIN_H:
                    d[v, a, r] = kb + r - i + NA_WIN_H - 1
    pairs = d.reshape(-1, 2)
    uniq = sorted(set(map(tuple, pairs)))
    code = np.array([uniq.index(tuple(p)) for p in pairs], np.int32)
    return code, np.array(uniq, np.int64)


_NA_CODES, _NA_CODE_ROWS = _na_tile_codes()


def _na_bias_tiles(rpb):
    qc = np.arange(GRID_W)[:, None]
    kc = np.arange(GRID_W)[None, :]
    cstart = np.clip(qc - NA_WIN_W // 2, 0, GRID_W - NA_WIN_W)
    col_ok = (kc >= cstart) & (kc < cstart + NA_WIN_W)
    dcol = np.clip(kc - qc, 1 - NA_WIN_W, NA_WIN_W - 1) + NA_WIN_W - 1
    onehot = (dcol.reshape(1, -1) == np.arange(2 * NA_WIN_W - 1)[:, None]).astype(np.float32)
    toe = jnp.einsum('hdm,mq->hdq', rpb.astype(F32), jnp.asarray(onehot),
                     precision=lax.Precision.HIGHEST)
    toe = toe.reshape(NA_HEADS, 2 * NA_WIN_H - 1, GRID_W, GRID_W)
    toe = jnp.where(jnp.asarray(col_ok), toe, NEG_INF)
    toe = jnp.concatenate([toe, jnp.full((NA_HEADS, 1, GRID_W, GRID_W), NEG_INF, F32)], axis=1)
    return jnp.concatenate([toe[:, _NA_CODE_ROWS[:, 0]], toe[:, _NA_CODE_ROWS[:, 1]]], axis=-1)


def _pair_attention(q, keys, vals, add_bias=None):
    lane = lax.broadcasted_iota(jnp.int32, (1, LANES), 1)
    scale = NA_HEAD_DIM ** -0.5
    out = jnp.zeros((q.shape[0], LANES), F32)
    for hh in range(2):
        in_head = (lane < NA_HEAD_DIM) if hh == 0 else (lane >= NA_HEAD_DIM)
        qm = jnp.where(in_head, q, jnp.zeros_like(q)) * jnp.asarray(scale, q.dtype)
        s = []
        for t, k in enumerate(keys):
            st = _dot_nt(qm, k)
            if add_bias is not None:
                st = add_bias(hh, t, st)
            s.append(st)
        m = s[0].max(axis=-1, keepdims=True)
        for st in s[1:]:
            m = jnp.maximum(m, st.max(axis=-1, keepdims=True))
        den = jnp.zeros_like(m)
        acc = jnp.zeros((q.shape[0], LANES), F32)
        for st, v in zip(s, vals):
            p = jnp.exp(st - m)
            den = den + p.sum(axis=-1, keepdims=True)
            acc = acc + _dot(p.astype(BF16), v)
        out = jnp.where(in_head, acc / den, out)
    return out


def _na_kernel(code_ref, q_ref, k0, k1, k2, k3, v0, v1, v2, v3, kc_ref, vc_ref, tile_ref, o_ref):
    nj = pl.num_programs(1)
    j = pl.program_id(1)
    kind = jnp.minimum(j, 1) + jnp.maximum(j - (nj - 2), 0)
    tiles_per_row = NA_KROWS // 2
    code_base = kind * (NA_QROWS * tiles_per_row)
    for pp in range(NA_PAIRS_PER_STEP):
        lanes = slice(pp * LANES, (pp + 1) * LANES)
        keys = [r[:, lanes] for r in (k0, k1, k2, k3, kc_ref)]
        vals = [r[:, lanes] for r in (v0, v1, v2, v3, vc_ref)]

        def add_bias(hh, t, st, pp=pp):
            if t >= NA_NKB:
                return st
            rows = []
            for a in range(NA_QROWS):
                cols = []
                for u in range(NA_KT // LANES):
                    code = code_ref[code_base + a * tiles_per_row + t * (NA_KT // LANES) + u]
                    piece = st[a * GRID_W:(a + 1) * GRID_W, u * LANES:(u + 1) * LANES]
                    cols.append(piece + tile_ref[2 * pp + hh, code])
                rows.append(jnp.concatenate(cols, axis=1))
            return jnp.concatenate(rows, axis=0)

        o_ref[:, lanes] = _pair_attention(q_ref[:, lanes], keys, vals, add_bias).astype(o_ref.dtype)


def _na_attention(z, tiles):
    nj = ROWS // NA_QROWS
    kblocks = SEQ // NA_KT
    width = NA_PAIRS_PER_STEP * LANES
    qcol, kcol, vcol = 0, NA_WIDTH // width, 2 * NA_WIDTH // width
    n_codes = tiles.shape[1]

    def kbase(j):
        return jnp.clip(2 * j - 1, 0, kblocks - NA_NKB)

    def kspec(t, col):
        return pl.BlockSpec((NA_KT, width),
                            lambda p, j, b, code: (b * kblocks + kbase(j) + t, col + p))

    ctx_row = lambda b: R_LAT // CTX_LEN + b
    in_specs = (
        [pl.BlockSpec((NA_QT, width), lambda p, j, b, code: (b * nj + j, qcol + p))]
        + [kspec(t, kcol) for t in range(NA_NKB)]
        + [kspec(t, vcol) for t in range(NA_NKB)]
        + [pl.BlockSpec((CTX_LEN, width), lambda p, j, b, code: (ctx_row(b), kcol + p)),
           pl.BlockSpec((CTX_LEN, width), lambda p, j, b, code: (ctx_row(b), vcol + p)),
           pl.BlockSpec((2 * NA_PAIRS_PER_STEP, n_codes, GRID_W, LANES),
                        lambda p, j, b, code: (p, 0, 0, 0))])
    return pl.pallas_call(
        _na_kernel,
        grid_spec=pltpu.PrefetchScalarGridSpec(
            num_scalar_prefetch=1,
            grid=(NA_HEADS // (2 * NA_PAIRS_PER_STEP), nj, BATCH),
            in_specs=in_specs,
            out_specs=pl.BlockSpec((NA_QT, width), lambda p, j, b, code: (b * nj + j, p))),
        out_shape=jax.ShapeDtypeStruct((R_LAT, NA_WIDTH), BF16),
        compiler_params=_cparams(("arbitrary", "arbitrary", "arbitrary")),
        name="na_attention",
    )(jnp.asarray(_NA_CODES), *([z] * (3 + 2 * NA_NKB)), tiles)


def _ctx_attn_kernel(q_ref, k_ref, v_ref, o_ref):
    o_ref[...] = _pair_attention(q_ref[...], [k_ref[...]], [v_ref[...]]).astype(o_ref.dtype)


def _ctx_attention(z):
    qcol, kcol, vcol = 0, NA_WIDTH // LANES, 2 * NA_WIDTH // LANES
    row = lambda b: R_LAT // CTX_LEN + b
    return pl.pallas_call(
        _ctx_attn_kernel,
        grid=(BATCH, NA_HEADS // 2),
        in_specs=[pl.BlockSpec((CTX_LEN, LANES), lambda b, p: (row(b), qcol + p)),
                  pl.BlockSpec((CTX_LEN, LANES), lambda b, p: (row(b), kcol + p)),
                  pl.BlockSpec((CTX_LEN, LANES), lambda b, p: (row(b), vcol + p))],
        out_specs=pl.BlockSpec((CTX_LEN, LANES), lambda b, p: (b, p)),
        out_shape=jax.ShapeDtypeStruct((R_CTX, NA_WIDTH), BF16),
        compiler_params=_cparams(("arbitrary", "arbitrary")),
        name="ctx_attention",
    )(z, z, z)


def _sgu_kernel(u_ref, v_ref, ng_ref, w_ref, bt_ref, o_ref):
    lane = lax.broadcasted_iota(jnp.int32, (1, LANES), 1)
    first = lane < SG_GROUP_DIM
    for ch in range(TM // SG_CHUNK):
        rows = slice(ch * SG_CHUNK, (ch + 1) * SG_CHUNK)
        u = _gelu_tanh(u_ref[rows, :].astype(F32))
        v = _gelu_tanh(v_ref[rows, :].astype(F32))
        vn = v * lax.rsqrt(jnp.mean(v * v, axis=-1, keepdims=True) + EPS) * ng_ref[...]
        vnb = vn.astype(BF16)
        for pr in range(SG_GROUPS // 2):
            cols = slice(pr * LANES, (pr + 1) * LANES)
            slab = vnb[:, cols]
            mixed = jnp.where(first, _dot(w_ref[2 * pr], slab), _dot(w_ref[2 * pr + 1], slab))
            o_ref[rows, cols] = (u[:, cols] * (mixed + bt_ref[:, cols])).astype(o_ref.dtype)


def _sgu(z, norm_g, w_bf16, bias_t):
    ucol = 3 * NA_WIDTH // SG_WIDTH
    return pl.pallas_call(
        _sgu_kernel,
        grid=(R_ALL // TM,),
        in_specs=[pl.BlockSpec((TM, SG_WIDTH), lambda i: (i, ucol)),
                  pl.BlockSpec((TM, SG_WIDTH), lambda i: (i, ucol + 1)),
                  pl.BlockSpec((1, SG_WIDTH), lambda i: (0, 0)),
                  pl.BlockSpec((SG_GROUPS, SG_CHUNK, SG_CHUNK), lambda i: (0, 0, 0)),
                  pl.BlockSpec((SG_CHUNK, SG_WIDTH), lambda i: (0, 0))],
        out_specs=pl.BlockSpec((TM, SG_WIDTH), lambda i: (i, 0)),
        out_shape=jax.ShapeDtypeStruct((R_ALL, SG_WIDTH), BF16),
        compiler_params=_cparams(("arbitrary",)),
        name="sgu",
    )(z, z, norm_g.reshape(1, SG_WIDTH), w_bf16, bias_t)


def _outproj_kernel(*refs, operand_refs):
    tile = pl.program_id(0)
    operands, pos = [], 0
    for n in operand_refs:
        operands.append(_read_row_operand(refs[pos:pos + n], tile))
        pos += n
    lhs, x = operands[:-1], operands[-1]
    ws = refs[pos:pos + len(lhs)]
    g1_ref, ng_ref, sh_ref, sc_ref, wr_ref, xo_ref, h2_ref, aff_ref = refs[pos + len(lhs):]
    y = _dot(lhs[0], ws[0][...])
    for a, w in zip(lhs[1:], ws[1:]):
        y = y + _dot(a, w[...])
    xn = x + g1_ref[...] * y
    _write_token_tiled(xo_ref, 0, xn)
    hn = xn * lax.rsqrt(jnp.mean(xn * xn, axis=-1, keepdims=True) + EPS) * ng_ref[...]
    h2 = hn * (1.0 + sc_ref[...]) + sh_ref[...]
    _write_token_tiled(h2_ref, 0, h2)
    logits = _dot_nt(wr_ref[...], h2.astype(BF16))
    e = jnp.exp(logits - logits.max(axis=0, keepdims=True))
    aff_ref[...] = e / e.sum(axis=0, keepdims=True)


def _outproj(lhs_list, w_list, xall, gate1, norm_g, shift, scale, w_router_pad, n_rows):
    grp = lambda i: (_group_of_tile(i), 0, 0)
    row = lambda i: (i, 0)
    const2 = lambda i: (0, 0)
    row_operands = list(lhs_list) + [xall]
    operand_specs = [_row_specs(a, lambda i: i) for a in row_operands]
    row_args = [t for a in row_operands for t in (a if isinstance(a, tuple) else (a,))]
    in_specs = (
        [s for specs in operand_specs for s in specs]
        + [pl.BlockSpec(w.shape, const2) for w in w_list]
        + [pl.BlockSpec((None, 1, D_MODEL), grp),
           pl.BlockSpec((1, D_MODEL), const2),
           pl.BlockSpec((None, 1, D_MODEL), grp),
           pl.BlockSpec((None, 1, D_MODEL), grp),
           pl.BlockSpec((N_EXPERTS, D_MODEL), const2)])
    tiled = (n_rows * NSUB, LANES)
    tiled_spec = pl.BlockSpec((TM * NSUB, LANES), row)
    return pl.pallas_call(
        functools.partial(_outproj_kernel, operand_refs=tuple(len(s) for s in operand_specs)),
        grid=(n_rows // TM,),
        in_specs=in_specs,
        out_specs=[tiled_spec, tiled_spec, pl.BlockSpec((N_EXPERTS, TM), lambda i: (0, i))],
        out_shape=[jax.ShapeDtypeStruct(tiled, F32),
                   jax.ShapeDtypeStruct(tiled, F32),
                   jax.ShapeDtypeStruct((N_EXPERTS, n_rows), F32)],
        compiler_params=_cparams(("arbitrary",)),
        name="outproj",
    )(*row_args, *w_list, gate1, norm_g.reshape(1, D_MODEL), shift, scale, w_router_pad)


def _moe_segments(rows):
    cap = EC_CAPACITY * SEQ // N_EXPERTS
    segs = [(b * cap, (b + 1) * cap, b) for b in range(BATCH)]
    if rows > BATCH * cap:
        segs.append((BATCH * cap, rows, BATCH))
    return segs


class _MoeDmaPlan(NamedTuple):
    write_steps: int
    fetch_first: int
    last_step: int
    write_per_tick: int
    fetch_per_tick: int
    fetch_slots: int
    token_per_tick: int
    token_slots: int


def _moe_dma_plan(rows):
    write_ticks = MOE_WRITE_STEPS * MOE_ROW_SPLIT
    fetch_first = MOE_WRITE_STEPS + 1
    last_step = MOE_NF - 1
    fetch_ticks = (last_step - fetch_first) * MOE_ROW_SPLIT
    token_ticks = (last_step - 1) * MOE_ROW_SPLIT
    assert rows % write_ticks == 0 and fetch_ticks > 0
    fetch_per_tick = -(-rows // fetch_ticks)
    while (fetch_per_tick * fetch_ticks) % MOE_DMA_UNROLL:
        fetch_per_tick += 1
    token_per_tick = -(-rows // token_ticks)
    while (token_per_tick * token_ticks) % MOE_DMA_UNROLL:
        token_per_tick += 1
    return _MoeDmaPlan(MOE_WRITE_STEPS, fetch_first, last_step, rows // write_ticks,
                       fetch_per_tick, fetch_per_tick * fetch_ticks,
                       token_per_tick, token_per_tick * token_ticks)


def _moe_kernel(idx_ref, h_hbm, x_hbm, gate_ref, g2_ref, wg_hbm, wu_hbm, wd_hbm, xo_hbm,
                xg, xb, acc, res, wg_s, wu_s, wd_s, wg_ring, wu_ring, wd_ring,
                sem_h, sem_r, sem_s, sem_w, *, rows, layer):
    del x_hbm
    e, f = pl.program_id(0), pl.program_id(1)
    nf = pl.num_programs(1)
    plan = _moe_dma_plan(rows)

    def issue(count, first, n_valid, expert, make_copy):
        base = expert * rows
        for u in range(count):
            i = first + u
            n = idx_ref[base + jnp.minimum(i, n_valid - 1)]
            make_copy(pl.ds(pl.multiple_of(i * NSUB, NSUB), NSUB), n).start(priority=u % 2)

    def issue_all(n_slots, expert, make_copy):
        def group(o, carry):
            issue(MOE_DMA_UNROLL, o * MOE_DMA_UNROLL, rows, expert, make_copy)
            return carry

        lax.fori_loop(0, n_slots // MOE_DMA_UNROLL, group, 0)

    def token_copy(slot, n):
        return pltpu.make_async_copy(h_hbm.at[n], xg.at[slot, :], sem_h)

    def fetch_copy(slot, n):
        return pltpu.make_async_copy(xo_hbm.at[n], res.at[slot, :], sem_r)

    def writeback_copy(slot, n):
        return pltpu.make_async_copy(res.at[slot, :], xo_hbm.at[n], sem_s)

    all_tokens = pltpu.make_async_copy(xg, xg, sem_h)
    all_fetched = pltpu.make_async_copy(res, res, sem_r)
    written_rows = res.at[pl.ds(0, rows * NSUB), :]
    all_written = pltpu.make_async_copy(written_rows, written_rows, sem_s)

    prev = jnp.maximum(e - 1, 0)
    nxt = jnp.where(e + 1 < N_EXPERTS, e + 1, 0)

    n_steps = N_EXPERTS * MOE_NF
    g_now = e * MOE_NF + f
    w_slot = g_now % MOE_WEIGHT_SLOTS

    def weight_copies(g):
        eg, fg = g // MOE_NF, g % MOE_NF
        slot = g % MOE_WEIGHT_SLOTS
        cols = pl.ds(pl.multiple_of(fg * MOE_TF, MOE_TF), MOE_TF)
        return (pltpu.make_async_copy(wg_hbm.at[layer, eg, :, cols], wg_ring.at[slot], sem_w.at[slot]),
                pltpu.make_async_copy(wu_hbm.at[layer, eg, :, cols], wu_ring.at[slot], sem_w.at[slot]),
                pltpu.make_async_copy(wd_hbm.at[layer, eg, cols, :], wd_ring.at[slot], sem_w.at[slot]))

    @pl.when(g_now == 0)
    def _():
        for g in range(MOE_WEIGHT_SLOTS - 1):
            for c in weight_copies(g):
                c.start()

    @pl.when(g_now + MOE_WEIGHT_SLOTS - 1 < n_steps)
    def _():
        for c in weight_copies(g_now + MOE_WEIGHT_SLOTS - 1):
            c.start()

    @pl.when(g_now < n_steps)
    def _():
        for c in weight_copies(g_now):
            c.wait()

    def apply_update():
        for lo, hi, g in _moe_segments(rows):
            for r0 in range(lo, hi, MOE_UPDATE_ROWS):
                nr = min(MOE_UPDATE_ROWS, hi - r0)
                gate = gate_ref[r0:r0 + nr, :]
                sums = []
                for s in range(NSUB):
                    cols = slice(s * LANES, (s + 1) * LANES)
                    tiled = pl.ds(r0 * NSUB + s, nr, stride=NSUB)
                    sums.append(res[tiled, :] + acc[r0:r0 + nr, cols] * gate * g2_ref[g, :, cols])
                for s in range(NSUB):
                    res[pl.ds(r0 * NSUB + s, nr, stride=NSUB), :] = sums[s]

    @pl.when(f == 0)
    def _():
        @pl.when(e == 0)
        def _():
            issue_all(plan.token_slots, 0, token_copy)
            issue_all(plan.fetch_slots, 0, fetch_copy)

        all_tokens.wait()
        all_fetched.wait()

        @pl.when(e > 0)
        def _():
            apply_update()

        @pl.when(e == N_EXPERTS)
        def _():
            issue_all(rows, N_EXPERTS - 1, writeback_copy)
            all_written.wait()

    @pl.when((f == plan.fetch_first) & (e < N_EXPERTS))
    def _():
        all_written.wait()

    def step_body(write_back=False, fetch=False, gather=False, first=False):
        wg_s[...] = wg_ring[w_slot].astype(BF16)
        wu_s[...] = wu_ring[w_slot].astype(BF16)
        wd_s[...] = wd_ring[w_slot].astype(BF16)
        rc = rows // MOE_ROW_SPLIT
        for r in range(MOE_ROW_SPLIT):
            sl = slice(r * rc, (r + 1) * rc)
            if first:
                for r0 in range(r * rc, (r + 1) * rc, MOE_UPDATE_ROWS):
                    nr = min(MOE_UPDATE_ROWS, (r + 1) * rc - r0)
                    xb[r0:r0 + nr, :] = _read_token_tiled(xg, r0, nr).astype(BF16)
            x = xb[sl, :]
            a = _dot(x, wg_s[...])
            b = _dot(x, wu_s[...])
            hid = (a * _sigmoid(a) * b).astype(BF16)
            if first:
                acc[sl, :] = _dot(hid, wd_s[...])
            else:
                acc[sl, :] += _dot(hid, wd_s[...])
            tick = f * MOE_ROW_SPLIT + r
            if write_back:
                issue(plan.write_per_tick, tick * plan.write_per_tick, rows, prev, writeback_copy)
            if fetch:
                slot0 = (tick - plan.fetch_first * MOE_ROW_SPLIT) * plan.fetch_per_tick
                issue(plan.fetch_per_tick, slot0, rows, e, fetch_copy)
            if gather:
                slot0 = (tick - MOE_ROW_SPLIT) * plan.token_per_tick
                issue(plan.token_per_tick, slot0, rows, nxt, token_copy)

    real = e < N_EXPERTS

    @pl.when((f == 0) & real)
    def _():
        step_body(write_back=True, first=True)

    @pl.when((f > 0) & (f < plan.write_steps) & real)
    def _():
        step_body(write_back=True, gather=True)

    @pl.when((f >= plan.write_steps) & (f < plan.fetch_first) & real)
    def _():
        step_body(gather=True)

    @pl.when((f >= plan.fetch_first) & (f < plan.last_step) & real)
    def _():
        step_body(fetch=True, gather=True)

    @pl.when((f >= plan.last_step) & real)
    def _():
        step_body()


def _moe_ffn(flat_idx, h2, x1, gates, gate2, w_gate, w_up, w_down, layer):
    rows = flat_idx.shape[1]
    assert rows % MOE_DMA_UNROLL == 0 and rows % MOE_ROW_SPLIT == 0
    plan = _moe_dma_plan(rows)
    nf = MOE_NF
    n_tokens = x1.shape[0] // NSUB
    as_tokens = lambda t: t.reshape(n_tokens, NSUB, LANES)
    out = pl.pallas_call(
        functools.partial(_moe_kernel, rows=rows, layer=layer),
        grid_spec=pltpu.PrefetchScalarGridSpec(
            num_scalar_prefetch=1,
            grid=(N_EXPERTS + 1, nf),
            in_specs=[pl.BlockSpec(memory_space=pl.ANY),
                      pl.BlockSpec(memory_space=pl.ANY),
                      pl.BlockSpec((None, rows, LANES), lambda e, f, idx: (jnp.maximum(e - 1, 0), 0, 0)),
                      pl.BlockSpec((N_GROUPS, 1, D_MODEL), lambda e, f, idx: (0, 0, 0)),
                      pl.BlockSpec(memory_space=pl.ANY),
                      pl.BlockSpec(memory_space=pl.ANY),
                      pl.BlockSpec(memory_space=pl.ANY)],
            out_specs=pl.BlockSpec(memory_space=pl.ANY),
            scratch_shapes=[pltpu.VMEM((plan.token_slots * NSUB, LANES), F32),
                            pltpu.VMEM((rows, D_MODEL), BF16),
                            pltpu.VMEM((rows, D_MODEL), F32),
                            pltpu.VMEM((plan.fetch_slots * NSUB, LANES), F32),
                            pltpu.VMEM((D_MODEL, MOE_TF), BF16),
                            pltpu.VMEM((D_MODEL, MOE_TF), BF16),
                            pltpu.VMEM((MOE_TF, D_MODEL), BF16),
                            pltpu.VMEM((MOE_WEIGHT_SLOTS, D_MODEL, MOE_TF), F32),
                            pltpu.VMEM((MOE_WEIGHT_SLOTS, D_MODEL, MOE_TF), F32),
                            pltpu.VMEM((MOE_WEIGHT_SLOTS, MOE_TF, D_MODEL), F32),
                            pltpu.SemaphoreType.DMA,
                            pltpu.SemaphoreType.DMA,
                            pltpu.SemaphoreType.DMA,
                            pltpu.SemaphoreType.DMA((MOE_WEIGHT_SLOTS,))]),
        out_shape=jax.ShapeDtypeStruct((n_tokens, NSUB, LANES), F32),
        input_output_aliases={2: 0},
        compiler_params=_cparams(("arbitrary", "arbitrary")),
        name="moe_ffn",
    )(flat_idx.reshape(-1), as_tokens(h2), as_tokens(x1), gates, gate2, w_gate, w_up, w_down)
    return out.reshape(x1.shape)


def _exclusive_prefix(mask, upper):
    out = []
    run = jnp.zeros((mask.shape[0], 1), F32)
    for c in range(mask.shape[1] // ROUTE_PREFIX_BLOCK):
        m = mask[:, c * ROUTE_PREFIX_BLOCK:(c + 1) * ROUTE_PREFIX_BLOCK]
        out.append(_dot(m.astype(BF16), upper) + run)
        run = run + jnp.sum(m, axis=1, keepdims=True)
    return jnp.concatenate(out, axis=1)


def _route_kernel(aff_ref, idx_ref, gate_ref, pos3, sel3, val3, *, cap):
    n = aff_ref.shape[1]
    v = aff_ref[...]
    thr = jnp.zeros((N_EXPERTS, 1), jnp.int32)
    for bit in range(30, -1, -1):
        cand = thr | (1 << bit)
        cnt = jnp.sum((v >= pltpu.bitcast(cand, F32)).astype(jnp.int32), axis=1, keepdims=True)
        thr = jnp.where(cnt >= cap, cand, thr)
    ri = lax.broadcasted_iota(jnp.int32, (ROUTE_PREFIX_BLOCK, ROUTE_PREFIX_BLOCK), 0)
    ci = lax.broadcasted_iota(jnp.int32, (ROUTE_PREFIX_BLOCK, ROUTE_PREFIX_BLOCK), 1)
    upper = (ri < ci).astype(BF16)
    gt = v >= pltpu.bitcast(thr + 1, F32)
    eq = (v >= pltpu.bitcast(thr, F32)) & ~gt
    need = (cap - jnp.sum(gt.astype(jnp.int32), axis=1, keepdims=True)).astype(F32)
    sel = gt | (eq & (_exclusive_prefix(eq.astype(F32), upper) < need))
    sel_f = sel.astype(F32)
    pos = _exclusive_prefix(sel_f, upper)
    for e in range(N_EXPERTS):
        pos3[e] = pos[e:e + 1, :].astype(jnp.int32)
        sel3[e] = sel_f[e:e + 1, :]
        val3[e] = v[e:e + 1, :]

    tok = lax.broadcasted_iota(jnp.int32, (1, n), 1)
    tok_hi = (tok >> 7).astype(F32)
    tok_lo = (tok & (LANES - 1)).astype(F32)
    hi_row = lax.broadcasted_iota(jnp.int32, (ROUTE_SLOT_HI, 1), 0)
    lo_row = lax.broadcasted_iota(jnp.int32, (LANES, 1), 0)

    def per_expert(e, carry):
        p, s, a = pos3[e], sel3[e], val3[e]
        a1 = a.astype(BF16).astype(F32)
        a2 = (a - a1).astype(BF16).astype(F32)
        a3 = a - a1 - a2
        hot_hi = jnp.where((p >> 7) == hi_row, s, 0.0)
        hot_lo = ((p & (LANES - 1)) == lo_row).astype(BF16)
        lhs = jnp.concatenate([hot_hi * tok_hi, hot_hi * tok_lo,
                               hot_hi * a1, hot_hi * a2, hot_hi * a3], axis=0).astype(BF16)
        out = _dot_nt(lhs, hot_lo)
        h = ROUTE_SLOT_HI
        idx_ref[e] = (out[0:h] * float(LANES) + out[h:2 * h]).astype(jnp.int32)
        gate_ref[e] = (out[2 * h:3 * h] + out[3 * h:4 * h]) + out[4 * h:5 * h]
        return carry

    lax.fori_loop(0, N_EXPERTS, per_expert, 0)


def _route(aff_t, n, cap, first_block, n_sets):
    assert cap <= ROUTE_SLOT_HI * LANES and n % ROUTE_PREFIX_BLOCK == 0 and n < (1 << 14)
    shape = (n_sets, N_EXPERTS, ROUTE_SLOT_HI, LANES)
    spec = pl.BlockSpec((None, N_EXPERTS, ROUTE_SLOT_HI, LANES), lambda s: (s, 0, 0, 0))
    return pl.pallas_call(
        functools.partial(_route_kernel, cap=cap),
        grid=(n_sets,),
        in_specs=[pl.BlockSpec((N_EXPERTS, n), lambda s: (0, first_block + s))],
        out_specs=[spec, spec],
        out_shape=[jax.ShapeDtypeStruct(shape, jnp.int32), jax.ShapeDtypeStruct(shape, F32)],
        scratch_shapes=[pltpu.VMEM((N_EXPERTS, 1, n), jnp.int32),
                        pltpu.VMEM((N_EXPERTS, 1, n), F32),
                        pltpu.VMEM((N_EXPERTS, 1, n), F32)],
        compiler_params=_cparams(("arbitrary",)),
        name="route",
    )(aff_t)


def _expert_choice(h2, x1, aff_t, gate2, w_gate, w_up, w_down, layer, with_ctx):
    def slots(idx, gate, cap, row0, set_len):
        idx = idx.reshape(BATCH, N_EXPERTS, -1)[:, :, :cap]
        gate = gate.reshape(BATCH, N_EXPERTS, -1)[:, :, :cap]
        idx = idx + row0 + (jnp.arange(BATCH, dtype=jnp.int32) * set_len)[:, None, None]
        return (jnp.swapaxes(idx, 0, 1).reshape(N_EXPERTS, BATCH * cap),
                jnp.swapaxes(gate, 0, 1).reshape(N_EXPERTS, BATCH * cap))

    cap = EC_CAPACITY * SEQ // N_EXPERTS
    flat, gate = slots(*_route(aff_t, SEQ, cap, 0, BATCH), cap, 0, SEQ)
    if with_ctx:
        cap_c = EC_CAPACITY * CTX_LEN // N_EXPERTS
        flat_c, gate_c = slots(*_route(aff_t, CTX_LEN, cap_c, R_LAT // CTX_LEN, BATCH),
                               cap_c, R_LAT, CTX_LEN)
        flat = jnp.concatenate([flat, flat_c], axis=1)
        gate = jnp.concatenate([gate, gate_c], axis=1)
    gate = jnp.broadcast_to(gate[..., None], gate.shape + (LANES,))
    return _moe_ffn(flat, h2, x1, gate, gate2, w_gate, w_up, w_down, layer)


def _final_norm_kernel(x_ref, ng_ref, o_ref):
    x = _read_rows(x_ref)
    o_ref[...] = x * lax.rsqrt(jnp.mean(x * x, axis=-1, keepdims=True) + EPS) * ng_ref[...]


def _final_norm(x, norm_g):
    n_rows = x.size // D_MODEL
    return pl.pallas_call(
        _final_norm_kernel,
        grid=(n_rows // TM,),
        in_specs=[_row_spec(x, lambda i: i),
                  pl.BlockSpec((1, D_MODEL), lambda i: (0, 0))],
        out_specs=pl.BlockSpec((TM, D_MODEL), lambda i: (i, 0)),
        out_shape=jax.ShapeDtypeStruct((n_rows, D_MODEL), F32),
        compiler_params=_cparams(("arbitrary",)),
        name="final_norm",
    )(x, norm_g.reshape(1, D_MODEL))


def _retention_kernel(lg_ref, q_ref, k_ref, v_ref, kc_ref, vc_ref, *rest, backward):
    if backward:
        o_ref, s_ref, dmat_ref, qd_ref, kd_ref = rest
    else:
        ob_ref, g_ref, o_ref, s_ref, dmat_ref, qd_ref, kd_ref = rest
    c = RET_CHUNK
    heads = range(RET_HEADS_PER_STEP)
    lgs = [lg_ref[1 if backward else 0, pl.program_id(1) * RET_HEADS_PER_STEP + hh] for hh in heads]
    qks = [slice(hh * RET_QK_DIM, (hh + 1) * RET_QK_DIM) for hh in heads]
    vvs = [slice(hh * RET_V_DIM, (hh + 1) * RET_V_DIM) for hh in heads]

    @pl.when(pl.program_id(2) == 0)
    def _():
        ia = lax.broadcasted_iota(jnp.int32, (c, c), 0)
        ib = lax.broadcasted_iota(jnp.int32, (c, c), 1)
        diff = ((ib - ia) if backward else (ia - ib)).astype(F32)
        pos = lax.broadcasted_iota(jnp.int32, (c, RET_QK_DIM), 0).astype(F32)
        cpos = lax.broadcasted_iota(jnp.int32, (CTX_LEN, RET_QK_DIM), 0).astype(F32)
        for hh, lg in zip(heads, lgs):
            dmat_ref[hh] = jnp.where(diff >= 0, jnp.exp(jnp.maximum(diff, 0.0) * lg), 0.0)
            if backward:
                qd_ref[hh] = jnp.exp((c - pos) * lg)
                kd_ref[hh] = jnp.exp(pos * lg)
            else:
                qd_ref[hh] = jnp.exp((pos + 1.0) * lg)
                kd_ref[hh] = jnp.exp((c - 1.0 - pos) * lg)
            cw = jnp.exp(cpos * lg) if backward else jnp.exp((CTX_LEN - 1.0 - cpos) * lg)
            kcw = (kc_ref[:, qks[hh]].astype(F32) * cw).astype(BF16)
            s_ref[hh] = _dot_tn(kcw, vc_ref[:, vvs[hh]])

    for hh, lg, qk, vv in zip(heads, lgs, qks, vvs):
        qb = q_ref[:, qk]
        kb = k_ref[:, qk]
        v = v_ref[:, vv]
        att = _dot_nt(qb, kb) * dmat_ref[hh]
        s_old = s_ref[hh]
        o = (_dot(att.astype(BF16), v)
             + _dot((qb.astype(F32) * qd_ref[hh]).astype(BF16), s_old.astype(BF16)))
        chunk_decay = jnp.exp(jnp.zeros((1, RET_V_DIM), F32) + c * lg)
        s_ref[hh] = s_old * chunk_decay + _dot_tn((kb.astype(F32) * kd_ref[hh]).astype(BF16), v)
        if backward:
            o_ref[:, vv] = o.astype(o_ref.dtype)
        else:
            of = o + ob_ref[:, vv].astype(F32)
            y = of * lax.rsqrt(jnp.mean(of * of, axis=-1, keepdims=True) + EPS)
            g = g_ref[:, vv].astype(F32)
            o_ref[:, vv] = (g * _sigmoid(g) * y).astype(o_ref.dtype)


def _retention_pass(log_gamma, z, o_back):
    backward = o_back is None
    c = RET_CHUNK
    n = SEQ // c
    hp = RET_HEADS_PER_STEP
    qk_w, v_w = hp * RET_QK_DIM, hp * RET_V_DIM
    qk_cols = RET_QK_WIDTH // qk_w
    v_base = 2 * RET_QK_WIDTH // v_w
    g_base = v_base + RET_V_WIDTH // v_w
    chunk = (lambda t: n - 1 - t) if backward else (lambda t: t)
    row = lambda b, t: b * n + chunk(t)
    ctx_row = lambda b: R_LAT // CTX_LEN + b
    in_specs = [
        pl.BlockSpec((c, qk_w), lambda b, h, t, lg: (row(b, t), h)),
        pl.BlockSpec((c, qk_w), lambda b, h, t, lg: (row(b, t), qk_cols + h)),
        pl.BlockSpec((c, v_w), lambda b, h, t, lg: (row(b, t), v_base + h)),
        pl.BlockSpec((CTX_LEN, qk_w), lambda b, h, t, lg: (ctx_row(b), qk_cols + h)),
        pl.BlockSpec((CTX_LEN, v_w), lambda b, h, t, lg: (ctx_row(b), v_base + h)),
    ]
    args = [z, z, z, z, z]
    if not backward:
        in_specs += [
            pl.BlockSpec((c, v_w), lambda b, h, t, lg: (row(b, t), h)),
            pl.BlockSpec((c, v_w), lambda b, h, t, lg: (row(b, t), g_base + h)),
        ]
        args += [o_back, z]
    return pl.pallas_call(
        functools.partial(_retention_kernel, backward=backward),
        grid_spec=pltpu.PrefetchScalarGridSpec(
            num_scalar_prefetch=1,
            grid=(BATCH, RET_HEADS // hp, n),
            in_specs=in_specs,
            out_specs=pl.BlockSpec((c, v_w), lambda b, h, t, lg: (row(b, t), h)),
            scratch_shapes=[pltpu.VMEM((hp, RET_QK_DIM, RET_V_DIM), F32),
                            pltpu.VMEM((hp, c, c), F32),
                            pltpu.VMEM((hp, c, RET_QK_DIM), F32),
                            pltpu.VMEM((hp, c, RET_QK_DIM), F32)]),
        out_shape=jax.ShapeDtypeStruct((R_LAT, RET_V_WIDTH), BF16),
        compiler_params=_cparams(("arbitrary", "arbitrary", "arbitrary")),
        name="retention_bwd" if backward else "retention_fwd",
    )(log_gamma, *args)


def _rope_tables():
    axis_dim = RET_QK_DIM // 2
    inv = (1.0 / (np.float32(ROPE_BASE) ** (np.arange(0, axis_dim, 2, dtype=np.float32)
                                             / np.float32(axis_dim)))).astype(np.float32)
    t = np.arange(SEQ)
    r = (t // GRID_W).astype(np.float32)
    col = (t % GRID_W).astype(np.float32)
    ang = np.concatenate([r[:, None] * inv, col[:, None] * inv], axis=-1).astype(np.float32)
    ang = np.concatenate([ang, np.zeros((TM, ang.shape[1]), np.float32)], axis=0)
    return (jnp.asarray(np.cos(ang).astype(np.float32)),
            jnp.asarray(np.sin(ang).astype(np.float32)))


def _split_mod(mod_layer):
    m = mod_layer[:N_GROUPS].reshape(N_GROUPS, 6, 1, D_MODEL)
    return [m[:, k] for k in range(6)]


def _router_t(w_router):
    return w_router.T.astype(BF16)


def kernel(x, c, ctx, c_ctx, ada_w, ada_b, norm1_g, norm2_g, ab_w_in, ab_w_out, na_rpb, sgu_norm_g, sgu_w, sgu_b, ret_w_in, ret_w_out, ret_decay_logit, moe_router, moe_w_gate, moe_w_up, moe_w_down, final_norm_g):
    assert DEPTH == 2 and x.shape == (BATCH, SEQ, D_MODEL) and ctx.shape == (BATCH, CTX_LEN, D_MODEL)

    cvec = jnp.zeros((8, D_MODEL), F32).at[:BATCH].set(c).at[BATCH].set(c_ctx)
    mod = _adaln(cvec, ada_w, ada_b)
    xall = (x.reshape(R_LAT, D_MODEL), ctx.reshape(R_CTX, D_MODEL))

    sh1, sc1, g1, sh2, sc2, g2 = _split_mod(mod[0])
    z = _inproj(xall, norm1_g[0], sh1, sc1, ab_w_in[0].astype(BF16), tn=AB_IN)
    a_all = (_na_attention(z, _na_bias_tiles(na_rpb[0])), _ctx_attention(z))
    bias_t = jnp.repeat(sgu_b[0].T, SG_GROUP_DIM, axis=1)
    bsg = _sgu(z, sgu_norm_g[0], sgu_w[0].astype(BF16), bias_t)
    w_out = ab_w_out[0].astype(BF16)
    x1, h2, aff = _outproj([a_all, bsg], [w_out[:NA_WIDTH], w_out[NA_WIDTH:]], xall, g1,
                           norm2_g[0], sh2, sc2, _router_t(moe_router[0]), R_ALL)
    xall = _expert_choice(h2, x1, aff, g2, moe_w_gate, moe_w_up, moe_w_down, 0, with_ctx=True)

    sh1, sc1, g1, sh2, sc2, g2 = _split_mod(mod[1])
    z = _inproj_ret(xall, norm1_g[1], sh1, sc1, ret_w_in[0].astype(BF16), *_rope_tables())
    log_gamma = jax.nn.log_sigmoid(ret_decay_logit[0].astype(F32))
    o_back = _retention_pass(log_gamma, z, None)
    ypre = _retention_pass(log_gamma, z, o_back)
    x1, h2, aff = _outproj([ypre], [ret_w_out[0].astype(BF16)], xall, g1,
                           norm2_g[1], sh2, sc2, _router_t(moe_router[1]), R_LAT)
    x2 = _expert_choice(h2, x1, aff, g2, moe_w_gate, moe_w_up, moe_w_down, 1, with_ctx=False)
    return _final_norm(x2, final_norm_g).reshape(BATCH, SEQ, D_MODEL)
```
